```python
import math
import jax, jax.numpy as jnp
from jax import lax
import numpy as np

D_MODEL = 1024
BATCH = 1
SEQ = 16384
DEPTH = 2

HEAD_DIM = 64
BLOCK_Q = 128
NORM_EPS = 1e-6
MLA_HEADS = 8
MLA_Q_RANK = 256
MLA_KV_RANK = 128
MLA_NOPE = 64
MLA_ROPE = 32
MLA_V = 64
ROPE_BASE = 10000.0
DIFF_HEADS = 4
DIFF_QK = 64
DIFF_V = 2 * DIFF_QK
DIL_CONFIGS = ((128, 1), (512, 4), (2048, 16))
DIL_GROUPS = len(DIL_CONFIGS)
DIL_HEADS_PER_GROUP = 4
SB_HEADS = 4
D_FF = 2816
CONV_WIDTH = 3

EVEN_SPLITS = (MLA_Q_RANK, MLA_KV_RANK, MLA_ROPE,
               DIFF_HEADS * 2 * DIFF_QK, DIFF_HEADS * 2 * DIFF_QK, DIFF_HEADS * DIFF_V)
EVEN_IN = sum(EVEN_SPLITS)
EVEN_OUT = MLA_HEADS * MLA_V + DIFF_HEADS * DIFF_V
DIL_QKV = DIL_GROUPS * DIL_HEADS_PER_GROUP * HEAD_DIM
SB_QKV = SB_HEADS * HEAD_DIM
ODD_SPLITS = (DIL_QKV, DIL_QKV, DIL_QKV, SB_QKV, SB_QKV, SB_QKV)
ODD_IN = sum(ODD_SPLITS)
ODD_OUT = DIL_HEADS_PER_GROUP * HEAD_DIM + SB_QKV
N_EVEN = (DEPTH + 1) // 2
N_ODD = DEPTH // 2

kernel_name = "hybrid_mla_diff_dilated_stickbreak_convffn"


def rms_norm(x, g):
    xf = x.astype(jnp.float32)
    y = xf * lax.rsqrt(jnp.mean(xf * xf, axis=-1, keepdims=True) + NORM_EPS)
    return (y * g.astype(jnp.float32)).astype(x.dtype)


def alibi_slopes(n):
    return jnp.asarray(2.0 ** (-8.0 * np.arange(1, n + 1) / n), dtype=jnp.float32)


def split_cols(p, sizes):
    return jnp.split(p, np.cumsum(sizes)[:-1].tolist(), axis=-1)


def rope(x, pos):
    half = x.shape[-1] // 2
    inv = ROPE_BASE ** (-jnp.arange(half, dtype=jnp.float32) / half)
    ang = pos.astype(jnp.float32)[:, None] * inv
    cos, sin = jnp.cos(ang), jnp.sin(ang)
    xf = x.astype(jnp.float32)
    x1, x2 = xf[..., :half], xf[..., half:]
    return jnp.concatenate([x1 * cos - x2 * sin, x2 * cos + x1 * sin], axis=-1).astype(x.dtype)


def to_blocks(t):
    b, h, s, d = t.shape
    return t.reshape(b, h, s // BLOCK_Q, BLOCK_Q, d).transpose(2, 0, 1, 3, 4)


def from_blocks(t):
    nb, b, h, bq, d = t.shape
    return t.transpose(1, 2, 0, 3, 4).reshape(b, h, nb * bq, d)


def heads_first(t, n_heads):
    b, s, _ = t.shape
    return t.reshape(b, s, n_heads, -1).transpose(0, 2, 1, 3)


def heads_last(t):
    b, h, s, d = t.shape
    return t.transpose(0, 2, 1, 3).reshape(b, s, h * d)


def mla_attention(q_nope, q_rope, k_nope, k_rope, v):
    s_len = v.shape[2]
    scale = (MLA_NOPE + MLA_ROPE) ** -0.5
    kpos = jnp.arange(s_len)

    def block(args):
        i, qn, qr = args
        qpos = i * BLOCK_Q + jnp.arange(BLOCK_Q)
        s = (jnp.einsum('bhqd,bhkd->bhqk', qn, k_nope)
             + jnp.einsum('bhqr,bkr->bhqk', qr, k_rope)).astype(jnp.float32) * scale
        s = jnp.where(kpos[None, :] <= qpos[:, None], s, -jnp.inf)
        p = jax.nn.softmax(s, axis=-1)
        return jnp.einsum('bhqk,bhkd->bhqd', p.astype(v.dtype), v)

    out = lax.map(block, (jnp.arange(s_len // BLOCK_Q), to_blocks(q_nope), to_blocks(q_rope)))
    return from_blocks(out)


def diff_attention(q1, q2, k1, k2, v, lam, slopes):
    s_len = v.shape[2]
    scale = DIFF_QK ** -0.5
    kpos = jnp.arange(s_len)

    def block(args):
        i, a1, a2 = args
        qpos = i * BLOCK_Q + jnp.arange(BLOCK_Q)
        dist = (qpos[:, None] - kpos[None, :]).astype(jnp.float32)
        causal = dist >= 0
        bias = -slopes[:, None, None] * dist

        def probs(qb, kf):
            s = jnp.einsum('bhqd,bhkd->bhqk', qb, kf).astype(jnp.float32) * scale + bias
            return jax.nn.softmax(jnp.where(causal, s, -jnp.inf), axis=-1)

        p = probs(a1, k1) - lam * probs(a2, k2)
        return jnp.einsum('bhqk,bhkd->bhqd', p.astype(v.dtype), v)

    out = lax.map(block, (jnp.arange(s_len // BLOCK_Q), to_blocks(q1), to_blocks(q2)))
    return from_blocks(out)


def dilated_attention(q, k, v, slopes):
    b, g_n, hg, s_len, d = q.shape
    nb = s_len // BLOCK_Q
    scale = d ** -0.5
    qb = q.reshape(b, g_n, hg, nb, BLOCK_Q, d).transpose(3, 0, 1, 2, 4, 5)
    padded = []
    for g, (w, dil) in enumerate(DIL_CONFIGS):
        pad = ((0, 0), (0, 0), (w, 0), (0, 0))
        padded.append((jnp.pad(k[:, g], pad), jnp.pad(v[:, g], pad)))

    def block(args):
        i, qblk = args
        qpos = i * BLOCK_Q + jnp.arange(BLOCK_Q)
        outs, lses = [], []
        for g, (w, dil) in enumerate(DIL_CONFIGS):
            steps = np.arange(w // dil + 1) * dil
            local = np.arange(BLOCK_Q)[:, None] + w - steps[None, :]
            kp, vp = padded[g]
            kb = lax.dynamic_slice_in_dim(kp, i * BLOCK_Q, BLOCK_Q + w, axis=2)
            vb = lax.dynamic_slice_in_dim(vp, i * BLOCK_Q, BLOCK_Q + w, axis=2)
            kg = kb[:, :, local]
            vg = vb[:, :, local]
            steps_f = jnp.asarray(steps, dtype=jnp.float32)
            s = (jnp.einsum('bhqd,bhqnd->bhqn', qblk[:, g], kg).astype(jnp.float32) * scale
                 - slopes[g][:, None, None] * steps_f[None, None, :])
            valid = (qpos[:, None] - jnp.asarray(steps)[None, :]) >= 0
            s = jnp.where(valid, s, -jnp.inf)
            m = jnp.max(s, axis=-1, keepdims=True)
            e = jnp.exp(s - m)
            den = jnp.sum(e, axis=-1, keepdims=True)
            o = jnp.einsum('bhqn,bhqnd->bhqd', (e / den).astype(v.dtype), vg)
            outs.append(o.astype(jnp.float32))
            lses.append((m + jnp.log(den))[..., 0])
        wts = jax.nn.softmax(jnp.stack(lses, axis=0), axis=0)
        o = jnp.einsum('gbhq,gbhqd->bhqd', wts, jnp.stack(outs, axis=0))
        return o.astype(v.dtype)

    out = lax.map(block, (jnp.arange(nb), qb))
    return from_blocks(out)


def stick_breaking_attention(q, k, v):
    s_len = v.shape[2]
    scale = q.shape[-1] ** -0.5
    kpos = jnp.arange(s_len)

    def block(args):
        i, qblk = args
        qpos = i * BLOCK_Q + jnp.arange(BLOCK_Q)
        strict = kpos[None, :] < qpos[:, None]
        z = jnp.einsum('bhqd,bhkd->bhqk', qblk, k).astype(jnp.float32) * scale
        log_beta = jax.nn.log_sigmoid(z)
        log_keep = jnp.where(strict, jax.nn.log_sigmoid(-z), 0.0)
        after = lax.cumsum(log_keep, axis=3, reverse=True) - log_keep
        a = jnp.where(strict, jnp.exp(log_beta + after), 0.0)
        return jnp.einsum('bhqk,bhkd->bhqd', a.astype(v.dtype), v)

    out = lax.map(block, (jnp.arange(s_len // BLOCK_Q), to_blocks(q)))
    return from_blocks(out)


def causal_dwconv(h, w, b):
    c = h.shape[-1]
    y = lax.conv_general_dilated(h, w[:, None, :].astype(h.dtype), (1,), [(CONV_WIDTH - 1, 0)],
                                 dimension_numbers=('NWC', 'WIO', 'NWC'), feature_group_count=c)
    return y + b


def conv_ffn(h, w_up, conv_w, conv_b, w_down):
    u = causal_dwconv(h @ w_up, conv_w, conv_b)
    gate, up = jnp.split(u, 2, axis=-1)
    return (jax.nn.silu(gate) * up) @ w_down


def even_mixer(h, w_in, cq_g, w_uq, ckv_g, w_ukv, lam_q1, lam_k1, lam_q2, lam_k2,
               subln_g, w_out, layer_idx):
    b, s_len, _ = h.shape
    c_q, c_kv, k_rope, dq, dk, dv = split_cols(h @ w_in, EVEN_SPLITS)
    pos = jnp.arange(s_len)
    q = heads_first(rms_norm(c_q, cq_g) @ w_uq, MLA_HEADS)
    q_nope, q_rope = q[..., :MLA_NOPE], rope(q[..., MLA_NOPE:], pos)
    kv = heads_first(rms_norm(c_kv, ckv_g) @ w_ukv, MLA_HEADS)
    k_nope, v_mla = kv[..., :MLA_NOPE], kv[..., MLA_NOPE:]
    o_mla = mla_attention(q_nope, q_rope, k_nope, rope(k_rope, pos), v_mla)
    dq = dq.reshape(b, s_len, DIFF_HEADS, 2, DIFF_QK).transpose(0, 2, 3, 1, 4)
    dk = dk.reshape(b, s_len, DIFF_HEADS, 2, DIFF_QK).transpose(0, 2, 3, 1, 4)
    dv = heads_first(dv, DIFF_HEADS)
    lam_init = 0.8 - 0.6 * math.exp(-0.3 * layer_idx)
    f32 = jnp.float32
    lam = (jnp.exp(jnp.sum(lam_q1.astype(f32) * lam_k1.astype(f32)))
           - jnp.exp(jnp.sum(lam_q2.astype(f32) * lam_k2.astype(f32))) + lam_init)
    o_diff = diff_attention(dq[:, :, 0], dq[:, :, 1], dk[:, :, 0], dk[:, :, 1], dv, lam,
                            alibi_slopes(DIFF_HEADS))
    o_diff = rms_norm(o_diff, subln_g) * (1.0 - lam_init)
    return jnp.concatenate([heads_last(o_mla), heads_last(o_diff)], axis=-1) @ w_out


def odd_mixer(h, w_in, w_out):
    b, s_len, _ = h.shape
    cq, ck, cv, sq, sk, sv = split_cols(h @ w_in, ODD_SPLITS)

    def groups(t):
        return t.reshape(b, s_len, DIL_GROUPS, DIL_HEADS_PER_GROUP, HEAD_DIM).transpose(0, 2, 3, 1, 4)

    slopes = alibi_slopes(DIL_GROUPS * DIL_HEADS_PER_GROUP).reshape(DIL_GROUPS, DIL_HEADS_PER_GROUP)
    o_dil = dilated_attention(groups(cq), groups(ck), groups(cv), slopes)
    o_sb = stick_breaking_attention(heads_first(sq, SB_HEADS), heads_first(sk, SB_HEADS),
                                    heads_first(sv, SB_HEADS))
    return jnp.concatenate([heads_last(o_dil), heads_last(o_sb)], axis=-1) @ w_out


def setup_inputs(seed: int = 0) -> dict:
    key = jax.random.key(seed)
    ks = iter(jax.random.split(key, 32))

    def nrm(shape, scale):
        return jax.random.normal(next(ks), shape, jnp.float32) * scale

    def gain(shape):
        return 1.0 + nrm(shape, 0.05)

    D = D_MODEL
    return {
        "x": nrm((BATCH, SEQ, D), 1.0),
        "ev_pre_g": gain((N_EVEN, D)),
        "ev_w_in": nrm((N_EVEN, D, EVEN_IN), D ** -0.5),
        "ev_cq_g": gain((N_EVEN, MLA_Q_RANK)),
        "ev_w_uq": nrm((N_EVEN, MLA_Q_RANK, MLA_HEADS * (MLA_NOPE + MLA_ROPE)), MLA_Q_RANK ** -0.5),
        "ev_ckv_g": gain((N_EVEN, MLA_KV_RANK)),
        "ev_w_ukv": nrm((N_EVEN, MLA_KV_RANK, MLA_HEADS * (MLA_NOPE + MLA_V)), MLA_KV_RANK ** -0.5),
        "ev_lam_q1": nrm((N_EVEN, DIFF_QK), 0.1),
        "ev_lam_k1": nrm((N_EVEN, DIFF_QK), 0.1),
        "ev_lam_q2": nrm((N_EVEN, DIFF_QK), 0.1),
        "ev_lam_k2": nrm((N_EVEN, DIFF_QK), 0.1),
        "ev_subln_g": gain((N_EVEN, DIFF_V)),
        "ev_w_out": nrm((N_EVEN, EVEN_OUT, D), EVEN_OUT ** -0.5),
        "ev_post_g": gain((N_EVEN, D)),
        "od_pre_g": gain((N_ODD, D)),
        "od_w_in": nrm((N_ODD, D, ODD_IN), D ** -0.5),
        "od_w_out": nrm((N_ODD, ODD_OUT, D), ODD_OUT ** -0.5),
        "od_post_g": gain((N_ODD, D)),
        "ffn_pre_g": gain((DEPTH, D)),
        "ffn_w_up": nrm((DEPTH, D, 2 * D_FF), D ** -0.5),
        "ffn_conv_w": nrm((DEPTH, CONV_WIDTH, 2 * D_FF), CONV_WIDTH ** -0.5),
        "ffn_conv_b": nrm((DEPTH, 2 * D_FF), 0.01),
        "ffn_w_down": nrm((DEPTH, D_FF, D), D_FF ** -0.5),
        "ffn_post_g": gain((DEPTH, D)),
    }


def reference(x, ev_pre_g, ev_w_in, ev_cq_g, ev_w_uq, ev_ckv_g, ev_w_ukv, ev_lam_q1, ev_lam_k1,
              ev_lam_q2, ev_lam_k2, ev_subln_g, ev_w_out, ev_post_g, od_pre_g, od_w_in, od_w_out,
              od_post_g, ffn_pre_g, ffn_w_up, ffn_conv_w, ffn_conv_b, ffn_w_down, ffn_post_g):
    for i in range(DEPTH):
        j = i // 2
        if i % 2 == 0:
            h = rms_norm(x, ev_pre_g[j])
            h = even_mixer(h, ev_w_in[j], ev_cq_g[j], ev_w_uq[j], ev_ckv_g[j], ev_w_ukv[j],
                           ev_lam_q1[j], ev_lam_k1[j], ev_lam_q2[j], ev_lam_k2[j],
                           ev_subln_g[j], ev_w_out[j], i)
            x = x + rms_norm(h, ev_post_g[j])
        else:
            h = rms_norm(x, od_pre_g[j])
            h = odd_mixer(h, od_w_in[j], od_w_out[j])
            x = x + rms_norm(h, od_post_g[j])
        h = conv_ffn(rms_norm(x, ffn_pre_g[i]), ffn_w_up[i], ffn_conv_w[i], ffn_conv_b[i], ffn_w_down[i])
        x = x + rms_norm(h, ffn_post_g[i])
    return x
```

```python
import functools
import math

import numpy as np
import jax
import jax.numpy as jnp
from jax import lax
from jax.experimental import pallas as pl
from jax.experimental.pallas import tpu as pltpu

F32 = jnp.float32
BF16 = jnp.bfloat16

D_MODEL = 1024
NORM_EPS = 1e-6
MLA_HEADS = 8
MLA_Q_RANK = 256
MLA_KV_RANK = 128
MLA_NOPE = 64
MLA_ROPE = 32
MLA_V = 64
ROPE_BASE = 10000.0
DIFF_HEADS = 4
DIFF_QK = 64
DIFF_V = 128
DIL_CONFIGS = ((128, 1), (512, 4), (2048, 16))
DIL_HEADS_PER_GROUP = 4
HEAD_DIM = 64
SB_HEADS = 4
D_FF = 2816
CONV_WIDTH = 3

LANES = 128
SUBLANES = 8
VMEM_LIMIT_BYTES = 56 * 1024 * 1024
ROW_TILE = 512
ATT_T = 256
DIL_T = 128
FF_CHUNK = 256
NEG = -1e30
SB_NEG = -1e30

_TRANS_B = (((1,), (1,)), ((), ()))


def _dot(a, b):
    return jnp.dot(a, b, preferred_element_type=F32)


def _dot_t(a, b):
    return lax.dot_general(a, b, _TRANS_B, preferred_element_type=F32)


def _rms(xf, g):
    ms = jnp.mean(xf * xf, axis=-1, keepdims=True)
    return xf * lax.rsqrt(ms + NORM_EPS) * g


def _params(sem):
    return pltpu.CompilerParams(dimension_semantics=sem, vmem_limit_bytes=VMEM_LIMIT_BYTES)


def _const_spec(shape):
    nd = len(shape)
    return pl.BlockSpec(shape, lambda *_: (0,) * nd, pipeline_mode=pl.Buffered(1))


def _proj_even_kernel(x_ref, g_ref, win_ref, cqg_ref, wuq_ref, ckvg_ref, wkv_ref,
                      cosq_ref, sinq_ref, cosk_ref, sink_ref, pos_ref,
                      qm_ref, km_ref, vm_ref, dq_ref, dk_ref, dv_ref):
    h = _rms(x_ref[...], g_ref[...]).astype(BF16)
    proj = _dot(h, win_ref[...])
    nq = MLA_HEADS * LANES
    cq = _rms(proj[:, 0:MLA_Q_RANK], cqg_ref[...]).astype(BF16)
    qq = _dot(cq, wuq_ref[...])
    cosq, sinq = cosq_ref[...], sinq_ref[...]
    for hh in range(MLA_HEADS):
        a = qq[:, hh * LANES:(hh + 1) * LANES]
        b = qq[:, nq + hh * LANES:nq + (hh + 1) * LANES]
        qm_ref[:, hh * LANES:(hh + 1) * LANES] = (a * cosq + b * sinq).astype(BF16)
    o = MLA_Q_RANK
    ckv = _rms(proj[:, o:o + MLA_KV_RANK], ckvg_ref[...]).astype(BF16)
    kv = _dot(ckv, wkv_ref[...])
    o += MLA_KV_RANK
    krc = proj[:, o:o + LANES] * cosk_ref[...] + proj[:, o + LANES:o + 2 * LANES] * sink_ref[...]
    for hh in range(MLA_HEADS):
        km_ref[:, hh * LANES:(hh + 1) * LANES] = (kv[:, hh * LANES:(hh + 1) * LANES] + krc).astype(BF16)
    vm_ref[...] = kv[:, nq:nq + MLA_HEADS * MLA_V].astype(BF16)
    o += 2 * LANES
    nd = DIFF_HEADS * 2 * DIFF_QK
    dq_ref[...] = (proj[:, o:o + nd] * (DIFF_QK ** -0.5)).astype(BF16)
    o += nd
    pos = pos_ref[...]
    for hh in range(DIFF_HEADS):
        dk_ref[:, hh * 2 * LANES:hh * 2 * LANES + LANES] = proj[:, o + hh * LANES:o + (hh + 1) * LANES].astype(BF16)
        dk_ref[:, hh * 2 * LANES + LANES:(hh + 1) * 2 * LANES] = pos
    o += nd
    dv_ref[...] = proj[:, o:o + DIFF_HEADS * DIFF_V].astype(BF16)


def _proj_even(x, g, win, cqg, wuq, ckvg, wkv, cosq, sinq, cosk, sink, pos):
    s = x.shape[0]
    tm = ROW_TILE
    row = lambda c: pl.BlockSpec((tm, c), lambda i: (i, 0))
    outs = [(MLA_HEADS * LANES, BF16), (MLA_HEADS * LANES, BF16), (MLA_HEADS * MLA_V, BF16),
            (DIFF_HEADS * 2 * DIFF_QK, BF16), (DIFF_HEADS * 2 * LANES, BF16), (DIFF_HEADS * DIFF_V, BF16)]
    return pl.pallas_call(
        _proj_even_kernel,
        grid=(s // tm,),
        in_specs=[row(D_MODEL), _const_spec(g.shape), _const_spec(win.shape), _const_spec(cqg.shape),
                  _const_spec(wuq.shape), _const_spec(ckvg.shape), _const_spec(wkv.shape),
                  row(LANES), row(LANES), row(LANES), row(LANES), _const_spec(pos.shape)],
        out_specs=[row(c) for c, _ in outs],
        out_shape=[jax.ShapeDtypeStruct((s, c), dt) for c, dt in outs],
        compiler_params=_params(("parallel",)),
        name="proj_even",
    )(x, g, win, cqg, wuq, ckvg, wkv, cosq, sinq, cosk, sink, pos)


def _mla_kernel(q_ref, k_ref, v_ref, o_ref):
    i = pl.program_id(1)
    t = ATT_T
    q = q_ref[...]
    qs = (q[:, :LANES], q[:, LANES:])
    row = lax.broadcasted_iota(jnp.int32, (t, t), 0)
    col = lax.broadcasted_iota(jnp.int32, (t, t), 1)

    def step(j, carry, masked):
        start = pl.multiple_of(j * t, t)
        kb = k_ref[pl.ds(start, t), :]
        vb = v_ref[pl.ds(start, t), :]
        new = []
        for hh in range(2):
            m, l, acc = carry[hh]
            s = _dot_t(qs[hh], kb[:, hh * LANES:(hh + 1) * LANES])
            if masked:
                s = jnp.where(col <= row, s, NEG)
            m_new = jnp.maximum(m, jnp.max(s, axis=1, keepdims=True))
            alpha = jnp.exp(m - m_new)
            p = jnp.exp(s - m_new)
            l = alpha * l + jnp.sum(p, axis=1, keepdims=True)
            acc = alpha * acc + _dot(p.astype(BF16), vb)
            new.append((m_new, l, acc))
        return tuple(new)

    one = (jnp.full((t, 1), NEG, F32), jnp.zeros((t, 1), F32), jnp.zeros((t, LANES), F32))
    carry = lax.fori_loop(0, i, lambda j, c: step(j, c, False), (one, one))
    (_, l0, a0), (_, l1, a1) = step(i, carry, True)
    lane = lax.broadcasted_iota(jnp.int32, (t, LANES), 1)
    o_ref[...] = jnp.where(lane < MLA_V, a0 / l0, a1 / l1).astype(BF16)


def _mla_attention(qm, km, vm):
    s = qm.shape[0]
    t = ATT_T
    return pl.pallas_call(
        _mla_kernel,
        grid=(MLA_HEADS // 2, s // t),
        in_specs=[pl.BlockSpec((t, 2 * LANES), lambda p, i: (i, p)),
                  pl.BlockSpec((s, 2 * LANES), lambda p, i: (0, p)),
                  pl.BlockSpec((s, LANES), lambda p, i: (0, p))],
        out_specs=pl.BlockSpec((t, LANES), lambda p, i: (i, p)),
        out_shape=jax.ShapeDtypeStruct((s, MLA_HEADS * MLA_V), BF16),
        compiler_params=_params(("parallel", "parallel")),
        name="mla_attn",
    )(qm, km, vm)


def _diff_kernel(lam_ref, slope_ref, qf_ref, g_ref, q_ref, k_ref, v_ref, o_ref, *, lam_init):
    i = pl.program_id(1)
    t = ATT_T
    lane = lax.broadcasted_iota(jnp.int32, (t, LANES), 1)
    qf32 = q_ref[...].astype(F32)
    feat = qf_ref[...]
    zero = jnp.zeros_like(qf32)
    qa = jnp.concatenate([jnp.where(lane < DIFF_QK, qf32, zero).astype(BF16), feat], axis=1)
    qb = jnp.concatenate([jnp.where(lane >= DIFF_QK, qf32, zero).astype(BF16), feat], axis=1)
    qs = (qa, qb)
    slope = slope_ref[0:1, 0:1]
    row = lax.broadcasted_iota(jnp.int32, (t, t), 0)
    col = lax.broadcasted_iota(jnp.int32, (t, t), 1)

    def step(j, carry, masked):
        start = pl.multiple_of(j * t, t)
        kb = k_ref[pl.ds(start, t), :]
        vb = v_ref[pl.ds(start, t), :]
        cj = slope * ((j - i) * t).astype(F32)
        new = []
        for hh in range(2):
            m, l, acc = carry[hh]
            s = _dot_t(qs[hh], kb)
            if masked:
                s = jnp.where(col <= row, s, NEG)
            m_new = jnp.maximum(m, jnp.max(s, axis=1, keepdims=True) + cj)
            alpha = jnp.exp(m - m_new)
            p = jnp.exp(s - (m_new - cj))
            l = alpha * l + jnp.sum(p, axis=1, keepdims=True)
            acc = alpha * acc + _dot(p.astype(BF16), vb)
            new.append((m_new, l, acc))
        return tuple(new)

    one = (jnp.full((t, 1), NEG, F32), jnp.zeros((t, 1), F32), jnp.zeros((t, LANES), F32))
    carry = lax.fori_loop(0, i, lambda j, c: step(j, c, False), (one, one))
    (_, l0, a0), (_, l1, a1) = step(i, carry, True)
    lp = lam_ref[...]
    s1 = jnp.sum(lp[0:1, :] * lp[1:2, :], axis=1, keepdims=True)
    s2 = jnp.sum(lp[2:3, :] * lp[3:4, :], axis=1, keepdims=True)
    lam = jnp.exp(s1) - jnp.exp(s2) + lam_init
    o = a0 / l0 - lam * (a1 / l1)
    o_ref[...] = (_rms(o, g_ref[...]) * (1.0 - lam_init)).astype(BF16)


def _diff_attention(lam_p, slope_arr, qfeat, subln_g, dq, dk, dv, lam_init):
    s = dq.shape[0]
    t = ATT_T
    return pl.pallas_call(
        functools.partial(_diff_kernel, lam_init=lam_init),
        grid=(DIFF_HEADS, s // t),
        in_specs=[pl.BlockSpec(lam_p.shape, lambda h, i: (0, 0)),
                  pl.BlockSpec((None, SUBLANES, LANES), lambda h, i: (h, 0, 0)),
                  pl.BlockSpec((None, t, LANES), lambda h, i: (h, 0, 0)),
                  pl.BlockSpec(subln_g.shape, lambda h, i: (0, 0)),
                  pl.BlockSpec((t, LANES), lambda h, i: (i, h)),
                  pl.BlockSpec((s, 2 * LANES), lambda h, i: (0, h)),
                  pl.BlockSpec((s, LANES), lambda h, i: (0, h))],
        out_specs=pl.BlockSpec((t, LANES), lambda h, i: (i, h)),
        out_shape=jax.ShapeDtypeStruct((s, DIFF_HEADS * DIFF_V), BF16),
        compiler_params=_params(("parallel", "parallel")),
        name="diff_attn",
    )(lam_p, slope_arr, qfeat, subln_g, dq, dk, dv)


def _shift_rows(u, k, prev):
    top = jnp.where(lax.broadcasted_iota(jnp.int32, prev.shape, 0) < k,
                    pltpu.roll(prev, k, 0), pltpu.roll(u[0:SUBLANES, :], k, 0))
    return jnp.concatenate([top, pltpu.roll(u, k, 0)[SUBLANES:, :]], axis=0)


def _ffn_tile(x1, fpre, wup_ref, cw_ref, cb_ref, wdn_ref, fpost, carry_ref):
    tm = x1.shape[0]
    hf = _rms(x1, fpre).astype(BF16)
    acc = jnp.zeros((tm, D_MODEL), F32)
    for c in range(D_FF // FF_CHUNK):
        ys = []
        for part in range(2):
            c0 = part * D_FF + c * FF_CHUNK
            u = _dot(hf, wup_ref[:, c0:c0 + FF_CHUNK])
            prev = carry_ref[:, c0:c0 + FF_CHUNK]
            carry_ref[:, c0:c0 + FF_CHUNK] = u[tm - SUBLANES:, :]
            ys.append(cw_ref[2:3, c0:c0 + FF_CHUNK] * u
                      + cw_ref[1:2, c0:c0 + FF_CHUNK] * _shift_rows(u, 1, prev)
                      + cw_ref[0:1, c0:c0 + FF_CHUNK] * _shift_rows(u, 2, prev)
                      + cb_ref[:, c0:c0 + FF_CHUNK])
        gate, up = ys
        act = (gate * (1.0 / (1.0 + jnp.exp(-gate))) * up).astype(BF16)
        acc = acc + _dot(act, wdn_ref[c * FF_CHUNK:(c + 1) * FF_CHUNK, :])
    return x1 + _rms(acc, fpost)


def _out_ffn_even_kernel(x_ref, a_ref, b_ref, wout_ref, postg_ref, fpre_ref, wup_ref, cw_ref, cb_ref,
                         wdn_ref, fpost_ref, o_ref, carry_ref):
    @pl.when(pl.program_id(0) == 0)
    def _():
        carry_ref[...] = jnp.zeros_like(carry_ref)

    na = a_ref.shape[1]
    mix = _dot(a_ref[...], wout_ref[0:na, :]) + _dot(b_ref[...], wout_ref[na:, :])
    x1 = x_ref[...] + _rms(mix, postg_ref[...])
    o_ref[...] = _ffn_tile(x1, fpre_ref[...], wup_ref, cw_ref, cb_ref, wdn_ref, fpost_ref[...], carry_ref)


def _out_ffn_odd_kernel(x_ref, o0_ref, l0_ref, o1_ref, l1_ref, o2_ref, l2_ref, b_ref, wout_ref, postg_ref,
                        fpre_ref, wup_ref, cw_ref, cb_ref, wdn_ref, fpost_ref, o_ref, carry_ref):
    @pl.when(pl.program_id(0) == 0)
    def _():
        carry_ref[...] = jnp.zeros_like(carry_ref)

    l0, l1, l2 = l0_ref[...], l1_ref[...], l2_ref[...]
    m = jnp.maximum(jnp.maximum(l0, l1), l2)
    e0, e1, e2 = jnp.exp(l0 - m), jnp.exp(l1 - m), jnp.exp(l2 - m)
    dil = (e0 * o0_ref[...] + e1 * o1_ref[...] + e2 * o2_ref[...]) / (e0 + e1 + e2)
    na = dil.shape[1]
    mix = _dot(dil.astype(BF16), wout_ref[0:na, :]) + _dot(b_ref[...].astype(BF16), wout_ref[na:, :])
    x1 = x_ref[...] + _rms(mix, postg_ref[...])
    o_ref[...] = _ffn_tile(x1, fpre_ref[...], wup_ref, cw_ref, cb_ref, wdn_ref, fpost_ref[...], carry_ref)


def _out_ffn(kernel, name, x, acts, consts):
    s = x.shape[0]
    tm = ROW_TILE
    row = lambda c: pl.BlockSpec((tm, c), lambda i: (i, 0))
    return pl.pallas_call(
        kernel,
        grid=(s // tm,),
        in_specs=[row(D_MODEL)] + [row(a.shape[1]) for a in acts] + [_const_spec(c.shape) for c in consts],
        out_specs=row(D_MODEL),
        out_shape=jax.ShapeDtypeStruct((s, D_MODEL), F32),
        scratch_shapes=[pltpu.VMEM((SUBLANES, 2 * D_FF), F32)],
        compiler_params=_params(("arbitrary",)),
        name=name,
    )(x, *acts, *consts)


def _proj_odd_kernel(x_ref, g_ref, win_ref, *out_refs):
    h = _rms(x_ref[...], g_ref[...]).astype(BF16)
    proj = _dot(h, win_ref[...])
    c = DIL_HEADS_PER_GROUP * HEAD_DIM
    scale = HEAD_DIM ** -0.5
    for n, ref in enumerate(out_refs):
        blk = proj[:, n * c:(n + 1) * c]
        if n < 3 or n == 9:
            blk = blk * scale
        ref[...] = blk.astype(BF16)


def _proj_odd(x, g, win):
    s = x.shape[0]
    tm = ROW_TILE
    c = DIL_HEADS_PER_GROUP * HEAD_DIM
    row = lambda w: pl.BlockSpec((tm, w), lambda i: (i, 0))
    return pl.pallas_call(
        _proj_odd_kernel,
        grid=(s // tm,),
        in_specs=[row(D_MODEL), _const_spec(g.shape), _const_spec(win.shape)],
        out_specs=[row(c)] * 12,
        out_shape=[jax.ShapeDtypeStruct((s, c), BF16)] * 12,
        compiler_params=_params(("parallel",)),
        name="proj_odd",
    )(x, g, win)


def _dilated_kernel(bias_ref, q_ref, kp_ref, kc_ref, vp_ref, vc_ref, o_ref, lse_ref, *, nub):
    t = DIL_T
    ub = pl.program_id(0) % nub
    c = DIL_HEADS_PER_GROUP * HEAD_DIM
    qf = q_ref[...].astype(F32)
    kcat = jnp.concatenate([kp_ref[...], kc_ref[...]], axis=0)
    vcat = jnp.concatenate([vp_ref[...], vc_ref[...]], axis=0)
    lane = lax.broadcasted_iota(jnp.int32, (t, c), 1)
    col = lax.broadcasted_iota(jnp.int32, (t, 2 * t), 1)
    no_prev = col < jnp.where(ub == 0, t, 0)
    out = jnp.zeros((t, c), F32)
    lse = jnp.zeros((t, c), F32)
    for hh in range(DIL_HEADS_PER_GROUP):
        mine = jnp.logical_and(lane >= hh * HEAD_DIM, lane < (hh + 1) * HEAD_DIM)
        qh = jnp.where(mine, qf, 0.0).astype(BF16)
        s = _dot_t(qh, kcat) + bias_ref[hh]
        s = jnp.where(no_prev, NEG, s)
        m = jnp.max(s, axis=1, keepdims=True)
        e = jnp.exp(s - m)
        den = jnp.sum(e, axis=1, keepdims=True)
        oh = _dot(e.astype(BF16), vcat) / den
        out = jnp.where(mine, oh, out)
        lse = jnp.where(mine, m + jnp.log(den), lse)
    o_ref[...] = out
    lse_ref[...] = lse


def _dilated_group(q, k, v, dil, slopes):
    s, c = q.shape
    t = DIL_T
    nub = s // dil // t
    a = np.arange(t)[:, None]
    cc = np.arange(2 * t)[None, :]
    steps = t + a - cc
    ok = (steps >= 0) & (steps <= t)
    dist = (steps * dil).astype(np.float32)
    bias = np.where(ok[None], -np.asarray(slopes, np.float32)[:, None, None] * dist[None], np.float32(NEG))
    bias = jnp.asarray(bias, F32)
    view = lambda z: z.reshape(s // dil, dil * c)
    cur = pl.BlockSpec((t, c), lambda b: (b % nub, b // nub))
    prev = pl.BlockSpec((t, c), lambda b: (jnp.maximum(b % nub - 1, 0), b // nub))
    o, lse = pl.pallas_call(
        functools.partial(_dilated_kernel, nub=nub),
        grid=(dil * nub,),
        in_specs=[_const_spec(bias.shape), cur, prev, cur, prev, cur],
        out_specs=[cur, cur],
        out_shape=[jax.ShapeDtypeStruct((s // dil, dil * c), F32)] * 2,
        compiler_params=_params(("parallel",)),
        name=f"dilated_d{dil}",
    )(bias, view(q), view(k), view(k), view(v), view(v))
    return o.reshape(s, c), lse.reshape(s, c)


def _sb_kernel(tri_ref, q_ref, k_ref, v_ref, o_ref):
    i = pl.program_id(0)
    hh = pl.program_id(1)
    t = ATT_T
    c = SB_HEADS * HEAD_DIM
    tri = tri_ref[...]
    lane = lax.broadcasted_iota(jnp.int32, (t, c), 1)
    mine = jnp.logical_and(lane >= hh * HEAD_DIM, lane < (hh + 1) * HEAD_DIM)
    qh = jnp.where(mine, q_ref[...].astype(F32), 0.0).astype(BF16)
    row = lax.broadcasted_iota(jnp.int32, (t, t), 0)
    col = lax.broadcasted_iota(jnp.int32, (t, t), 1)
    strict = col < row

    def step(j, carry, masked):
        r, acc = carry
        start = pl.multiple_of(j * t, t)
        kb = k_ref[pl.ds(start, t), :]
        vb = v_ref[pl.ds(start, t), :]
        z = _dot_t(qh, kb)
        sp = jnp.log1p(jnp.exp(-jnp.abs(z)))
        log_beta = jnp.minimum(z, 0.0) - sp
        log_keep = log_beta - z
        if masked:
            log_keep = jnp.where(strict, log_keep, 0.0)
        hi = log_keep.astype(BF16)
        lo = (log_keep - hi.astype(F32)).astype(BF16)
        after = _dot(jnp.concatenate([hi, lo], axis=1), tri)
        a = jnp.exp(log_beta + after + r)
        if masked:
            a = jnp.where(strict, a, 0.0)
        acc = acc + _dot(a.astype(BF16), vb)
        r = r + jnp.sum(log_keep, axis=1, keepdims=True)
        return r, acc

    carry = step(i, (jnp.zeros((t, 1), F32), jnp.zeros((t, c), F32)), True)
    _, acc = lax.fori_loop(0, i, lambda jj, cr: step(i - 1 - jj, cr, False), carry)

    @pl.when(hh == 0)
    def _():
        o_ref[...] = jnp.zeros_like(o_ref)

    o_ref[...] = jnp.where(mine, acc, o_ref[...])


def _stickbreak(q, k, v):
    s, c = q.shape
    t = ATT_T
    j = np.arange(t)
    tri1 = (j[:, None] > j[None, :]).astype(np.float32)
    tri = jnp.asarray(np.concatenate([tri1, tri1], axis=0), BF16)
    return pl.pallas_call(
        _sb_kernel,
        grid=(s // t, SB_HEADS),
        in_specs=[_const_spec(tri.shape),
                  pl.BlockSpec((t, c), lambda i, h: (i, 0)),
                  _const_spec((s, c)), _const_spec((s, c))],
        out_specs=pl.BlockSpec((t, c), lambda i, h: (i, 0)),
        out_shape=jax.ShapeDtypeStruct((s, c), F32),
        compiler_params=_params(("parallel", "arbitrary")),
        name="stickbreak",
    )(tri, q, k, v)


def _alibi_slopes(n):
    return 2.0 ** (-8.0 * np.arange(1, n + 1) / n)


def _pad_cols(w, lo, width):
    return jnp.pad(w, ((0, 0), (lo, width - lo - w.shape[1])))


def _rot_half_cols(w):
    half = w.shape[1] // 2
    return jnp.concatenate([-w[:, half:], w[:, :half]], axis=1)


def _even_weights(w_in, w_uq, w_ukv):
    o_kr = MLA_Q_RANK + MLA_KV_RANK
    w_kr = w_in[:, o_kr:o_kr + MLA_ROPE]
    win = jnp.concatenate([w_in[:, :o_kr], _pad_cols(w_kr, MLA_NOPE, LANES),
                           _pad_cols(_rot_half_cols(w_kr), MLA_NOPE, LANES),
                           w_in[:, o_kr + MLA_ROPE:]], axis=1)
    hd = MLA_NOPE + MLA_ROPE
    q_plain, q_rot = [], []
    for hh in range(MLA_HEADS):
        wh = w_uq[:, hh * hd:(hh + 1) * hd]
        q_plain.append(_pad_cols(wh, 0, LANES))
        q_rot.append(_pad_cols(_rot_half_cols(wh[:, MLA_NOPE:]), MLA_NOPE, LANES))
    wuq = jnp.concatenate(q_plain + q_rot, axis=1)
    hk = MLA_NOPE + MLA_V
    k_cols = [_pad_cols(w_ukv[:, hh * hk:hh * hk + MLA_NOPE], 0, LANES) for hh in range(MLA_HEADS)]
    v_cols = [w_ukv[:, hh * hk + MLA_NOPE:(hh + 1) * hk] for hh in range(MLA_HEADS)]
    wkv = jnp.concatenate(k_cols + v_cols, axis=1)
    return win.astype(BF16), wuq.astype(BF16), wkv.astype(BF16)


def _rope_tables(s):
    half = MLA_ROPE // 2
    inv = ROPE_BASE ** (-jnp.arange(half, dtype=F32) / half)
    ang = jnp.arange(s).astype(F32)[:, None] * inv
    cos2 = jnp.tile(jnp.cos(ang), (1, 2))
    sin2 = jnp.tile(jnp.sin(ang), (1, 2))
    scale = (MLA_NOPE + MLA_ROPE) ** -0.5
    tail = LANES - MLA_NOPE - MLA_ROPE
    ones, zeros, ztail = jnp.ones((s, MLA_NOPE), F32), jnp.zeros((s, MLA_NOPE), F32), jnp.zeros((s, tail), F32)
    cosq = jnp.concatenate([ones, cos2, ztail], axis=1) * scale
    sinq = jnp.concatenate([zeros, sin2, ztail], axis=1) * scale
    cosk = jnp.concatenate([zeros, cos2, ztail], axis=1)
    sink = jnp.concatenate([zeros, sin2, ztail], axis=1)
    return cosq, sinq, cosk, sink


def _row2d(v):
    return v.reshape(1, -1).astype(F32)


def kernel(x, ev_pre_g, ev_w_in, ev_cq_g, ev_w_uq, ev_ckv_g, ev_w_ukv, ev_lam_q1, ev_lam_k1, ev_lam_q2,
           ev_lam_k2, ev_subln_g, ev_w_out, ev_post_g, od_pre_g, od_w_in, od_w_out, od_post_g, ffn_pre_g,
           ffn_w_up, ffn_conv_w, ffn_conv_b, ffn_w_down, ffn_post_g):
    b, s, _ = x.shape
    assert b == 1 and s % max(ROW_TILE, ATT_T, DIL_T * DIL_CONFIGS[-1][1]) == 0
    xs = x[0]

    def ffn_consts(i):
        return [_row2d(ffn_pre_g[i]), ffn_w_up[i].astype(BF16), ffn_conv_w[i].astype(F32),
                _row2d(ffn_conv_b[i]), ffn_w_down[i].astype(BF16), _row2d(ffn_post_g[i])]

    win, wuq, wkv = _even_weights(ev_w_in[0], ev_w_uq[0], ev_w_ukv[0])
    cosq, sinq, cosk, sink = _rope_tables(s)
    pos_tile = jnp.pad((jnp.arange(ROW_TILE) % ATT_T).astype(BF16)[:, None], ((0, 0), (0, LANES - 1)))
    qm, km, vm, dq, dk, dv = _proj_even(xs, _row2d(ev_pre_g[0]), win, _row2d(ev_cq_g[0]), wuq,
                                        _row2d(ev_ckv_g[0]), wkv, cosq, sinq, cosk, sink, pos_tile)
    o_mla = _mla_attention(qm, km, vm)
    lam_p = jnp.pad(jnp.stack([ev_lam_q1[0], ev_lam_k1[0], ev_lam_q2[0], ev_lam_k2[0]]).astype(F32),
                    ((0, SUBLANES - 4), (0, LANES - DIFF_QK)))
    d_slopes = _alibi_slopes(DIFF_HEADS)
    slope_arr = jnp.asarray(np.broadcast_to(d_slopes[:, None, None], (DIFF_HEADS, SUBLANES, LANES)), F32)
    qfeat_np = np.zeros((DIFF_HEADS, ATT_T, LANES), np.float32)
    qfeat_np[:, :, 0] = d_slopes[:, None]
    lam_init = 0.8 - 0.6 * math.exp(-0.3 * 0)
    o_diff = _diff_attention(lam_p, slope_arr, jnp.asarray(qfeat_np, BF16), _row2d(ev_subln_g[0]),
                             dq, dk, dv, lam_init)
    x2 = _out_ffn(_out_ffn_even_kernel, "out_ffn_even", xs, [o_mla, o_diff],
                  [ev_w_out[0].astype(BF16), _row2d(ev_post_g[0])] + ffn_consts(0))

    p = _proj_odd(x2, _row2d(od_pre_g[0]), od_w_in[0].astype(BF16))
    slopes = _alibi_slopes(len(DIL_CONFIGS) * DIL_HEADS_PER_GROUP).reshape(len(DIL_CONFIGS), -1)
    dil_outs = []
    for gi, (_, dil) in enumerate(DIL_CONFIGS):
        dil_outs += list(_dilated_group(p[gi], p[3 + gi], p[6 + gi], dil, slopes[gi]))
    o_sb = _stickbreak(p[9], p[10], p[11])
    x4 = _out_ffn(_out_ffn_odd_kernel, "out_ffn_odd", x2, dil_outs + [o_sb],
                  [od_w_out[0].astype(BF16), _row2d(od_post_g[0])] + ffn_consts(1))
    return x4[None]
```

```python
import functools
import math

import numpy as np
import jax
import jax.numpy as jnp
from jax import lax
from jax.experimental import pallas as pl
from jax.experimental.pallas import tpu as pltpu

F32 = jnp.float32
BF16 = jnp.bfloat16

D_MODEL = 1024
NORM_EPS = 1e-6
MLA_HEADS = 8
MLA_Q_RANK = 256
MLA_KV_RANK = 128
MLA_NOPE = 64
MLA_ROPE = 32
MLA_V = 64
ROPE_BASE = 10000.0
DIFF_HEADS = 4
DIFF_QK = 64
DIFF_V = 128
DIL_CONFIGS = ((128, 1), (512, 4), (2048, 16))
DIL_HEADS_PER_GROUP = 4
HEAD_DIM = 64
SB_HEADS = 4
D_FF = 2816
CONV_WIDTH = 3

LANES = 128
SUBLANES = 8
BF16_ROWS = 16
VMEM_LIMIT_BYTES = 56 * 1024 * 1024
ROW_TILE = 512
ATT_TQ = 1024
ATT_TK = 256
DIL_T = 128
FF_CHUNK = 256
NEG = -1e30

_TRANS_B = (((1,), (1,)), ((), ()))


def _dot(a, b):
    return jnp.dot(a, b, preferred_element_type=F32)


def _dot_t(a, b):
    return lax.dot_general(a, b, _TRANS_B, preferred_element_type=F32)


def _rms(xf, g):
    ms = jnp.mean(xf * xf, axis=-1, keepdims=True)
    return xf * lax.rsqrt(ms + NORM_EPS) * g


def _params(sem):
    return pltpu.CompilerParams(dimension_semantics=sem, vmem_limit_bytes=VMEM_LIMIT_BYTES)


def _const_spec(shape):
    nd = len(shape)
    return pl.BlockSpec(shape, lambda *_: (0,) * nd, pipeline_mode=pl.Buffered(1))


def _store_transposed_tiles(dst_ref, src, heads, width, ones_rows):
    tk = ATT_TK
    per = width + ones_rows
    for b in range(src.shape[0] // tk):
        vt = src[b * tk:(b + 1) * tk, :].T.astype(BF16)
        for hh in range(heads):
            dst_ref[b, hh * per:hh * per + width, :] = vt[hh * width:(hh + 1) * width, :]
            if ones_rows:
                dst_ref[b, hh * per + width:(hh + 1) * per, :] = jnp.ones((ones_rows, tk), BF16)


def _proj_even_kernel(x_ref, g_ref, win_ref, cqg_ref, wuq_ref, ckvg_ref, wkv_ref,
                      cosq_ref, sinq_ref, cosk_ref, sink_ref, pos_ref,
                      qm_ref, km_ref, vmt_ref, dq_ref, dk_ref, dvt_ref):
    h = _rms(x_ref[...], g_ref[...]).astype(BF16)
    proj = _dot(h, win_ref[...])
    nq = MLA_HEADS * LANES
    cq = _rms(proj[:, 0:MLA_Q_RANK], cqg_ref[...]).astype(BF16)
    qq = _dot(cq, wuq_ref[...])
    cosq, sinq = cosq_ref[...], sinq_ref[...]
    for hh in range(MLA_HEADS):
        a = qq[:, hh * LANES:(hh + 1) * LANES]
        b = qq[:, nq + hh * LANES:nq + (hh + 1) * LANES]
        qm_ref[:, hh * LANES:(hh + 1) * LANES] = (a * cosq + b * sinq).astype(BF16)
    o = MLA_Q_RANK
    ckv = _rms(proj[:, o:o + MLA_KV_RANK], ckvg_ref[...]).astype(BF16)
    kv = _dot(ckv, wkv_ref[...])
    o += MLA_KV_RANK
    krc = proj[:, o:o + LANES] * cosk_ref[...] + proj[:, o + LANES:o + 2 * LANES] * sink_ref[...]
    for hh in range(MLA_HEADS):
        km_ref[:, hh * LANES:(hh + 1) * LANES] = (kv[:, hh * LANES:(hh + 1) * LANES] + krc).astype(BF16)
    _store_transposed_tiles(vmt_ref, kv[:, nq:nq + MLA_HEADS * MLA_V], MLA_HEADS, MLA_V, BF16_ROWS)
    o += 2 * LANES
    nd = DIFF_HEADS * 2 * DIFF_QK
    dq_ref[...] = (proj[:, o:o + nd] * (DIFF_QK ** -0.5)).astype(BF16)
    o += nd
    pos = pos_ref[...]
    for hh in range(DIFF_HEADS):
        dk_ref[:, hh * 2 * LANES:hh * 2 * LANES + LANES] = proj[:, o + hh * LANES:o + (hh + 1) * LANES].astype(BF16)
        dk_ref[:, hh * 2 * LANES + LANES:(hh + 1) * 2 * LANES] = pos
    o += nd
    _store_transposed_tiles(dvt_ref, proj[:, o:o + DIFF_HEADS * DIFF_V], DIFF_HEADS, DIFF_V, BF16_ROWS)


def _proj_even(x, g, win, cqg, wuq, ckvg, wkv, cosq, sinq, cosk, sink, pos):
    s = x.shape[0]
    tm, tk = ROW_TILE, ATT_TK
    row = lambda c: pl.BlockSpec((tm, c), lambda i: (i, 0))
    tile_t = lambda r: pl.BlockSpec((tm // tk, r, tk), lambda i: (i, 0, 0))
    mla_rows = MLA_HEADS * (MLA_V + BF16_ROWS)
    diff_rows = DIFF_HEADS * (DIFF_V + BF16_ROWS)
    flat = lambda c: jax.ShapeDtypeStruct((s, c), BF16)
    tiled = lambda r: jax.ShapeDtypeStruct((s // tk, r, tk), BF16)
    return pl.pallas_call(
        _proj_even_kernel,
        grid=(s // tm,),
        in_specs=[row(D_MODEL), _const_spec(g.shape), _const_spec(win.shape), _const_spec(cqg.shape),
                  _const_spec(wuq.shape), _const_spec(ckvg.shape), _const_spec(wkv.shape),
                  row(LANES), row(LANES), row(LANES), row(LANES), _const_spec(pos.shape)],
        out_specs=[row(MLA_HEADS * LANES), row(MLA_HEADS * LANES), tile_t(mla_rows),
                   row(DIFF_HEADS * 2 * DIFF_QK), row(DIFF_HEADS * 2 * LANES), tile_t(diff_rows)],
        out_shape=[flat(MLA_HEADS * LANES), flat(MLA_HEADS * LANES), tiled(mla_rows),
                   flat(DIFF_HEADS * 2 * DIFF_QK), flat(DIFF_HEADS * 2 * LANES), tiled(diff_rows)],
        compiler_params=_params(("parallel",)),
        name="proj_even",
    )(x, g, win, cqg, wuq, ckvg, wkv, cosq, sinq, cosk, sink, pos)


def _flash_cols(chains, i, rows, tile_bias=None):
    tq, tk = ATT_TQ, ATT_TK
    ratio = tq // tk
    krow = lax.broadcasted_iota(jnp.int32, (tk, tq), 0)
    qcol = lax.broadcasted_iota(jnp.int32, (tk, tq), 1)

    def step(j, carry, diag):
        out = []
        for (qh, load_k, load_vt), (m, acc) in zip(chains, carry):
            s = _dot_t(load_k(j), qh)
            if diag is not None:
                s = jnp.where(krow + diag * tk <= qcol, s, NEG)
            mb = jnp.max(s, axis=0, keepdims=True)
            if tile_bias is not None:
                c = tile_bias(j)
                mb = mb + c
            m_new = jnp.maximum(m, mb)
            alpha = jnp.exp(m - m_new)
            shift = m_new if tile_bias is None else m_new - c
            p = jnp.exp(s - shift).astype(BF16)
            out.append((m_new, alpha * acc + _dot(load_vt(j), p)))
        return tuple(out)

    carry = tuple((jnp.full((1, tq), NEG, F32), jnp.zeros((rows, tq), F32)) for _ in chains)
    carry = lax.fori_loop(0, i * ratio, lambda j, c: step(j, c, None), carry)
    carry = lax.fori_loop(0, ratio, lambda d, c: step(i * ratio + d, c, d), carry)
    return [acc for _, acc in carry]


def _k_tile(k_ref, j, lanes=slice(None)):
    return k_ref[pl.ds(pl.multiple_of(j * ATT_TK, ATT_TK), ATT_TK), lanes]


def _mla_kernel(q_ref, k_ref, vt_ref, o_ref):
    i = pl.program_id(1)
    rows = MLA_V + BF16_ROWS
    def chain(hh):
        lanes = slice(hh * LANES, (hh + 1) * LANES)
        return (q_ref[:, lanes], lambda j: _k_tile(k_ref, j, lanes),
                lambda j: vt_ref[j, hh * rows:(hh + 1) * rows, :])

    accs = _flash_cols([chain(0), chain(1)], i, rows)
    halves = [acc[0:MLA_V] / acc[MLA_V:MLA_V + 1] for acc in accs]
    o_ref[...] = jnp.concatenate(halves, axis=0).T.astype(BF16)


def _mla_attention(qm, km, vmt):
    s = qm.shape[0]
    tq = ATT_TQ
    rows = 2 * (MLA_V + BF16_ROWS)
    return pl.pallas_call(
        _mla_kernel,
        grid=(MLA_HEADS // 2, s // tq),
        in_specs=[pl.BlockSpec((tq, 2 * LANES), lambda p, i: (i, p)),
                  pl.BlockSpec((s, 2 * LANES), lambda p, i: (0, p)),
                  pl.BlockSpec((vmt.shape[0], rows, ATT_TK), lambda p, i: (0, p, 0))],
        out_specs=pl.BlockSpec((tq, LANES), lambda p, i: (i, p)),
        out_shape=jax.ShapeDtypeStruct((s, MLA_HEADS * MLA_V), BF16),
        compiler_params=_params(("parallel", "parallel")),
        name="mla_attn",
    )(qm, km, vmt)


def _diff_kernel(lam_ref, slope_ref, qf_ref, g_ref, q_ref, k_ref, vt_ref, o_ref, *, lam_init):
    i = pl.program_id(1)
    tq = ATT_TQ
    lane = lax.broadcasted_iota(jnp.int32, (tq, LANES), 1)
    qf32 = q_ref[...].astype(F32)
    feat = qf_ref[...]
    zero = jnp.zeros_like(qf32)
    qa = jnp.concatenate([jnp.where(lane < DIFF_QK, qf32, zero).astype(BF16), feat], axis=1)
    qb = jnp.concatenate([jnp.where(lane >= DIFF_QK, qf32, zero).astype(BF16), feat], axis=1)
    slope = slope_ref[0:1, 0:1]
    rows = DIFF_V + BF16_ROWS
    tile_bias = lambda j: slope * (j * ATT_TK - i * tq).astype(F32)
    load_k = lambda j: _k_tile(k_ref, j)
    load_vt = lambda j: vt_ref[j]
    acc1, acc2 = _flash_cols([(qa, load_k, load_vt), (qb, load_k, load_vt)], i, rows, tile_bias)
    lp = lam_ref[...]
    s1 = jnp.sum(lp[0:1, :] * lp[1:2, :], axis=1, keepdims=True)
    s2 = jnp.sum(lp[2:3, :] * lp[3:4, :], axis=1, keepdims=True)
    lam = jnp.exp(s1) - jnp.exp(s2) + lam_init
    ot = acc1[0:DIFF_V] / acc1[DIFF_V:DIFF_V + 1] - lam * (acc2[0:DIFF_V] / acc2[DIFF_V:DIFF_V + 1])
    o_ref[...] = (_rms(ot.T, g_ref[...]) * (1.0 - lam_init)).astype(BF16)


def _diff_attention(lam_p, slope_arr, qfeat, subln_g, dq, dk, dvt, lam_init):
    s = dq.shape[0]
    tq = ATT_TQ
    return pl.pallas_call(
        functools.partial(_diff_kernel, lam_init=lam_init),
        grid=(DIFF_HEADS, s // tq),
        in_specs=[pl.BlockSpec(lam_p.shape, lambda h, i: (0, 0)),
                  pl.BlockSpec((None, SUBLANES, LANES), lambda h, i: (h, 0, 0)),
                  pl.BlockSpec((None, tq, LANES), lambda h, i: (h, 0, 0)),
                  pl.BlockSpec(subln_g.shape, lambda h, i: (0, 0)),
                  pl.BlockSpec((tq, LANES), lambda h, i: (i, h)),
                  pl.BlockSpec((s, 2 * LANES), lambda h, i: (0, h)),
                  pl.BlockSpec((dvt.shape[0], DIFF_V + BF16_ROWS, ATT_TK), lambda h, i: (0, h, 0))],
        out_specs=pl.BlockSpec((tq, LANES), lambda h, i: (i, h)),
        out_shape=jax.ShapeDtypeStruct((s, DIFF_HEADS * DIFF_V), BF16),
        compiler_params=_params(("parallel", "parallel")),
        name="diff_attn",
    )(lam_p, slope_arr, qfeat, subln_g, dq, dk, dvt)


def _shift_rows(u, k, prev):
    top = jnp.where(lax.broadcasted_iota(jnp.int32, prev.shape, 0) < k,
                    pltpu.roll(prev, k, 0), pltpu.roll(u[0:SUBLANES, :], k, 0))
    return jnp.concatenate([top, pltpu.roll(u, k, 0)[SUBLANES:, :]], axis=0)


def _ffn_tile(x1, fpre, wup_ref, cw_ref, cb_ref, wdn_ref, fpost, carry_ref):
    tm = x1.shape[0]
    hf = _rms(x1, fpre).astype(BF16)
    acc = jnp.zeros((tm, D_MODEL), F32)
    for c in range(D_FF // FF_CHUNK):
        ys = []
        for part in range(2):
            c0 = part * D_FF + c * FF_CHUNK
            u = _dot(hf, wup_ref[:, c0:c0 + FF_CHUNK])
            prev = carry_ref[:, c0:c0 + FF_CHUNK]
            carry_ref[:, c0:c0 + FF_CHUNK] = u[tm - SUBLANES:, :]
            ys.append(cw_ref[2:3, c0:c0 + FF_CHUNK] * u
                      + cw_ref[1:2, c0:c0 + FF_CHUNK] * _shift_rows(u, 1, prev)
                      + cw_ref[0:1, c0:c0 + FF_CHUNK] * _shift_rows(u, 2, prev)
                      + cb_ref[:, c0:c0 + FF_CHUNK])
        gate, up = ys
        act = (gate * (1.0 / (1.0 + jnp.exp(-gate))) * up).astype(BF16)
        acc = acc + _dot(act, wdn_ref[c * FF_CHUNK:(c + 1) * FF_CHUNK, :])
    return x1 + _rms(acc, fpost)


def _out_ffn_even_kernel(x_ref, a_ref, b_ref, wout_ref, postg_ref, fpre_ref, wup_ref, cw_ref, cb_ref,
                         wdn_ref, fpost_ref, o_ref, carry_ref):
    @pl.when(pl.program_id(0) == 0)
    def _():
        carry_ref[...] = jnp.zeros_like(carry_ref)

    na = a_ref.shape[1]
    mix = _dot(a_ref[...], wout_ref[0:na, :]) + _dot(b_ref[...], wout_ref[na:, :])
    x1 = x_ref[...] + _rms(mix, postg_ref[...])
    o_ref[...] = _ffn_tile(x1, fpre_ref[...], wup_ref, cw_ref, cb_ref, wdn_ref, fpost_ref[...], carry_ref)


def _out_ffn_odd_kernel(x_ref, o0_ref, l0_ref, o1_ref, l1_ref, o2_ref, l2_ref, bt_ref, wout_ref, postg_ref,
                        fpre_ref, wup_ref, cw_ref, cb_ref, wdn_ref, fpost_ref, o_ref, carry_ref):
    @pl.when(pl.program_id(0) == 0)
    def _():
        carry_ref[...] = jnp.zeros_like(carry_ref)

    l0, l1, l2 = l0_ref[...], l1_ref[...], l2_ref[...]
    m = jnp.maximum(jnp.maximum(l0, l1), l2)
    e0, e1, e2 = jnp.exp(l0 - m), jnp.exp(l1 - m), jnp.exp(l2 - m)
    dil = (e0 * o0_ref[...] + e1 * o1_ref[...] + e2 * o2_ref[...]) / (e0 + e1 + e2)
    na = dil.shape[1]
    sb = bt_ref[...].T
    mix = _dot(dil.astype(BF16), wout_ref[0:na, :]) + _dot(sb.astype(BF16), wout_ref[na:, :])
    x1 = x_ref[...] + _rms(mix, postg_ref[...])
    o_ref[...] = _ffn_tile(x1, fpre_ref[...], wup_ref, cw_ref, cb_ref, wdn_ref, fpost_ref[...], carry_ref)


def _out_ffn(kernel, name, x, acts, acts_t, consts):
    s = x.shape[0]
    tm = ROW_TILE
    row = lambda c: pl.BlockSpec((tm, c), lambda i: (i, 0))
    col = lambda r: pl.BlockSpec((r, tm), lambda i: (0, i))
    return pl.pallas_call(
        kernel,
        grid=(s // tm,),
        in_specs=([row(D_MODEL)] + [row(a.shape[1]) for a in acts] + [col(a.shape[0]) for a in acts_t]
                  + [_const_spec(c.shape) for c in consts]),
        out_specs=row(D_MODEL),
        out_shape=jax.ShapeDtypeStruct((s, D_MODEL), F32),
        scratch_shapes=[pltpu.VMEM((SUBLANES, 2 * D_FF), F32)],
        compiler_params=_params(("arbitrary",)),
        name=name,
    )(x, *acts, *acts_t, *consts)


def _proj_odd_kernel(x_ref, g_ref, win_ref, *out_refs):
    h = _rms(x_ref[...], g_ref[...]).astype(BF16)
    proj = _dot(h, win_ref[...])
    c = DIL_HEADS_PER_GROUP * HEAD_DIM
    scale = HEAD_DIM ** -0.5
    for n, ref in enumerate(out_refs[:-1]):
        blk = proj[:, n * c:(n + 1) * c]
        if n < 3 or n == 9:
            blk = blk * scale
        ref[...] = blk.astype(BF16)
    _store_transposed_tiles(out_refs[-1], proj[:, 11 * c:12 * c], SB_HEADS, HEAD_DIM, 0)


def _proj_odd(x, g, win):
    s = x.shape[0]
    tm, tk = ROW_TILE, ATT_TK
    c = DIL_HEADS_PER_GROUP * HEAD_DIM
    row = lambda w: pl.BlockSpec((tm, w), lambda i: (i, 0))
    return pl.pallas_call(
        _proj_odd_kernel,
        grid=(s // tm,),
        in_specs=[row(D_MODEL), _const_spec(g.shape), _const_spec(win.shape)],
        out_specs=[row(c)] * 11 + [pl.BlockSpec((tm // tk, c, tk), lambda i: (i, 0, 0))],
        out_shape=[jax.ShapeDtypeStruct((s, c), BF16)] * 11 + [jax.ShapeDtypeStruct((s // tk, c, tk), BF16)],
        compiler_params=_params(("parallel",)),
        name="proj_odd",
    )(x, g, win)


def _dilated_kernel(bias_ref, q_ref, kp_ref, kc_ref, vp_ref, vc_ref, o_ref, lse_ref, *, nub):
    t = DIL_T
    ub = pl.program_id(0) % nub
    c = DIL_HEADS_PER_GROUP * HEAD_DIM
    qf = q_ref[...].astype(F32)
    kcat = jnp.concatenate([kp_ref[...], kc_ref[...]], axis=0)
    vcat = jnp.concatenate([vp_ref[...], vc_ref[...]], axis=0)
    lane = lax.broadcasted_iota(jnp.int32, (t, c), 1)
    col = lax.broadcasted_iota(jnp.int32, (t, 2 * t), 1)
    no_prev = col < jnp.where(ub == 0, t, 0)
    out = jnp.zeros((t, c), F32)
    lse = jnp.zeros((t, c), F32)
    for hh in range(DIL_HEADS_PER_GROUP):
        mine = jnp.logical_and(lane >= hh * HEAD_DIM, lane < (hh + 1) * HEAD_DIM)
        qh = jnp.where(mine, qf, 0.0).astype(BF16)
        s = _dot_t(qh, kcat) + bias_ref[hh]
        s = jnp.where(no_prev, NEG, s)
        m = jnp.max(s, axis=1, keepdims=True)
        e = jnp.exp(s - m)
        den = jnp.sum(e, axis=1, keepdims=True)
        oh = _dot(e.astype(BF16), vcat) / den
        out = jnp.where(mine, oh, out)
        lse = jnp.where(mine, m + jnp.log(den), lse)
    o_ref[...] = out
    lse_ref[...] = lse


def _dilated_group(q, k, v, dil, slopes):
    s, c = q.shape
    t = DIL_T
    nub = s // dil // t
    a = np.arange(t)[:, None]
    cc = np.arange(2 * t)[None, :]
    steps = t + a - cc
    ok = (steps >= 0) & (steps <= t)
    dist = (steps * dil).astype(np.float32)
    bias = np.where(ok[None], -np.asarray(slopes, np.float32)[:, None, None] * dist[None], np.float32(NEG))
    bias = jnp.asarray(bias, F32)
    view = lambda z: z.reshape(s // dil, dil * c)
    cur = pl.BlockSpec((t, c), lambda b: (b % nub, b // nub))
    prev = pl.BlockSpec((t, c), lambda b: (jnp.maximum(b % nub - 1, 0), b // nub))
    o, lse = pl.pallas_call(
        functools.partial(_dilated_kernel, nub=nub),
        grid=(dil * nub,),
        in_specs=[_const_spec(bias.shape), cur, prev, cur, prev, cur],
        out_specs=[cur, cur],
        out_shape=[jax.ShapeDtypeStruct((s // dil, dil * c), F32)] * 2,
        compiler_params=_params(("parallel",)),
        name=f"dilated_d{dil}",
    )(bias, view(q), view(k), view(k), view(v), view(v))
    return o.reshape(s, c), lse.reshape(s, c)


def _sb_kernel(tri_ref, q_ref, k_ref, vt_ref, o_ref):
    hh = pl.program_id(0)
    i = pl.program_id(1)
    tq, tk = ATT_TQ, ATT_TK
    ratio = tq // tk
    c = SB_HEADS * HEAD_DIM
    tri = tri_ref[...]
    lane = lax.broadcasted_iota(jnp.int32, (tq, c), 1)
    mine = jnp.logical_and(lane >= hh * HEAD_DIM, lane < (hh + 1) * HEAD_DIM)
    qh = jnp.where(mine, q_ref[...].astype(F32), 0.0).astype(BF16)
    krow = lax.broadcasted_iota(jnp.int32, (tk, tq), 0)
    qcol = lax.broadcasted_iota(jnp.int32, (tk, tq), 1)

    def step(j, carry, diag):
        r, acc = carry
        z = _dot_t(_k_tile(k_ref, j), qh)
        sp = jnp.log1p(jnp.exp(-jnp.abs(z)))
        log_beta = jnp.minimum(z, 0.0) - sp
        log_keep = log_beta - z
        if diag is not None:
            strict = krow + diag * tk < qcol
            log_keep = jnp.where(strict, log_keep, 0.0)
        hi = log_keep.astype(BF16)
        lo = (log_keep - hi.astype(F32)).astype(BF16)
        aft = _dot(tri, jnp.concatenate([hi, lo], axis=0))
        a = jnp.exp(log_beta + aft[0:tk] + r)
        if diag is not None:
            a = jnp.where(strict, a, 0.0)
        acc = acc + _dot(vt_ref[j], a.astype(BF16))
        return r + aft[tk:tk + 1], acc

    carry = (jnp.zeros((1, tq), F32), jnp.zeros((HEAD_DIM, tq), F32))
    n_full = i * ratio
    carry = lax.fori_loop(0, ratio, lambda dd, cr: step(n_full + ratio - 1 - dd, cr, ratio - 1 - dd), carry)
    _, acc = lax.fori_loop(0, n_full, lambda jj, cr: step(n_full - 1 - jj, cr, None), carry)
    o_ref[...] = acc


def _stickbreak(q, k, vt):
    s, c = q.shape
    tq, tk = ATT_TQ, ATT_TK
    j = np.arange(tk)
    later = (j[None, :] > j[:, None]).astype(np.float32)
    tri = np.concatenate([np.concatenate([later, later], axis=1), np.ones((BF16_ROWS, 2 * tk), np.float32)], axis=0)
    tri = jnp.asarray(tri, BF16)
    return pl.pallas_call(
        _sb_kernel,
        grid=(SB_HEADS, s // tq),
        in_specs=[_const_spec(tri.shape),
                  pl.BlockSpec((tq, c), lambda h, i: (i, 0)),
                  _const_spec((s, c)),
                  pl.BlockSpec((s // tk, HEAD_DIM, tk), lambda h, i: (0, h, 0))],
        out_specs=pl.BlockSpec((HEAD_DIM, tq), lambda h, i: (h, i)),
        out_shape=jax.ShapeDtypeStruct((c, s), F32),
        compiler_params=_params(("parallel", "parallel")),
        name="stickbreak",
    )(tri, q, k, vt)


def _alibi_slopes(n):
    return 2.0 ** (-8.0 * np.arange(1, n + 1) / n)


def _pad_cols(w, lo, width):
    return jnp.pad(w, ((0, 0), (lo, width - lo - w.shape[1])))


def _rot_half_cols(w):
    half = w.shape[1] // 2
    return jnp.concatenate([-w[:, half:], w[:, :half]], axis=1)


def _even_weights(w_in, w_uq, w_ukv):
    o_kr = MLA_Q_RANK + MLA_KV_RANK
    w_kr = w_in[:, o_kr:o_kr + MLA_ROPE]
    win = jnp.concatenate([w_in[:, :o_kr], _pad_cols(w_kr, MLA_NOPE, LANES),
                           _pad_cols(_rot_half_cols(w_kr), MLA_NOPE, LANES),
                           w_in[:, o_kr + MLA_ROPE:]], axis=1)
    hd = MLA_NOPE + MLA_ROPE
    q_plain, q_rot = [], []
    for hh in range(MLA_HEADS):
        wh = w_uq[:, hh * hd:(hh + 1) * hd]
        q_plain.append(_pad_cols(wh, 0, LANES))
        q_rot.append(_pad_cols(_rot_half_cols(wh[:, MLA_NOPE:]), MLA_NOPE, LANES))
    wuq = jnp.concatenate(q_plain + q_rot, axis=1)
    hk = MLA_NOPE + MLA_V
    k_cols = [_pad_cols(w_ukv[:, hh * hk:hh * hk + MLA_NOPE], 0, LANES) for hh in range(MLA_HEADS)]
    v_cols = [w_ukv[:, hh * hk + MLA_NOPE:(hh + 1) * hk] for hh in range(MLA_HEADS)]
    wkv = jnp.concatenate(k_cols + v_cols, axis=1)
    return win.astype(BF16), wuq.astype(BF16), wkv.astype(BF16)


def _rope_tables(s):
    half = MLA_ROPE // 2
    inv = ROPE_BASE ** (-jnp.arange(half, dtype=F32) / half)
    ang = jnp.arange(s).astype(F32)[:, None] * inv
    cos2 = jnp.tile(jnp.cos(ang), (1, 2))
    sin2 = jnp.tile(jnp.sin(ang), (1, 2))
    scale = (MLA_NOPE + MLA_ROPE) ** -0.5
    tail = LANES - MLA_NOPE - MLA_ROPE
    ones, zeros, ztail = jnp.ones((s, MLA_NOPE), F32), jnp.zeros((s, MLA_NOPE), F32), jnp.zeros((s, tail), F32)
    cosq = jnp.concatenate([ones, cos2, ztail], axis=1) * scale
    sinq = jnp.concatenate([zeros, sin2, ztail], axis=1) * scale
    cosk = jnp.concatenate([zeros, cos2, ztail], axis=1)
    sink = jnp.concatenate([zeros, sin2, ztail], axis=1)
    return cosq, sinq, cosk, sink


def _row2d(v):
    return v.reshape(1, -1).astype(F32)


def kernel(x, ev_pre_g, ev_w_in, ev_cq_g, ev_w_uq, ev_ckv_g, ev_w_ukv, ev_lam_q1, ev_lam_k1, ev_lam_q2,
           ev_lam_k2, ev_subln_g, ev_w_out, ev_post_g, od_pre_g, od_w_in, od_w_out, od_post_g, ffn_pre_g,
           ffn_w_up, ffn_conv_w, ffn_conv_b, ffn_w_down, ffn_post_g):
    b, s, _ = x.shape
    assert b == 1 and s % max(ROW_TILE, ATT_TQ, DIL_T * DIL_CONFIGS[-1][1]) == 0
    assert ROW_TILE % ATT_TK == 0 and ATT_TQ % ATT_TK == 0 and ATT_TK <= 256
    xs = x[0]

    def ffn_consts(i):
        return [_row2d(ffn_pre_g[i]), ffn_w_up[i].astype(BF16), ffn_conv_w[i].astype(F32),
                _row2d(ffn_conv_b[i]), ffn_w_down[i].astype(BF16), _row2d(ffn_post_g[i])]

    win, wuq, wkv = _even_weights(ev_w_in[0], ev_w_uq[0], ev_w_ukv[0])
    cosq, sinq, cosk, sink = _rope_tables(s)
    pos_tile = jnp.pad((jnp.arange(ROW_TILE) % ATT_TK).astype(BF16)[:, None], ((0, 0), (0, LANES - 1)))
    qm, km, vmt, dq, dk, dvt = _proj_even(xs, _row2d(ev_pre_g[0]), win, _row2d(ev_cq_g[0]), wuq,
                                          _row2d(ev_ckv_g[0]), wkv, cosq, sinq, cosk, sink, pos_tile)
    o_mla = _mla_attention(qm, km, vmt)
    lam_p = jnp.pad(jnp.stack([ev_lam_q1[0], ev_lam_k1[0], ev_lam_q2[0], ev_lam_k2[0]]).astype(F32),
                    ((0, SUBLANES - 4), (0, LANES - DIFF_QK)))
    d_slopes = _alibi_slopes(DIFF_HEADS)
    slope_arr = jnp.asarray(np.broadcast_to(d_slopes[:, None, None], (DIFF_HEADS, SUBLANES, LANES)), F32)
    qfeat_np = np.zeros((DIFF_HEADS, ATT_TQ, LANES), np.float32)
    qfeat_np[:, :, 0] = d_slopes[:, None]
    lam_init = 0.8 - 0.6 * math.exp(-0.3 * 0)
    o_diff = _diff_attention(lam_p, slope_arr, jnp.asarray(qfeat_np, BF16), _row2d(ev_subln_g[0]),
                             dq, dk, dvt, lam_init)
    x2 = _out_ffn(_out_ffn_even_kernel, "out_ffn_even", xs, [o_mla, o_diff], [],
                  [ev_w_out[0].astype(BF16), _row2d(ev_post_g[0])] + ffn_consts(0))

    p = _proj_odd(x2, _row2d(od_pre_g[0]), od_w_in[0].astype(BF16))
    slopes = _alibi_slopes(len(DIL_CONFIGS) * DIL_HEADS_PER_GROUP).reshape(len(DIL_CONFIGS), -1)
    dil_outs = []
    for gi, (_, dil) in enumerate(DIL_CONFIGS):
        dil_outs += list(_dilated_group(p[gi], p[3 + gi], p[6 + gi], dil, slopes[gi]))
    o_sbt = _stickbreak(p[9], p[10], p[11])
    x4 = _out_ffn(_out_ffn_odd_kernel, "out_ffn_odd", x2, dil_outs, [o_sbt],
                  [od_w_out[0].astype(BF16), _row2d(od_post_g[0])] + ffn_consts(1))
    return x4[None]
```

```python
import functools
import math

import numpy as np
import jax
import jax.numpy as jnp
from jax import lax
from jax.experimental import pallas as pl
from jax.experimental.pallas import tpu as pltpu

F32 = jnp.float32
BF16 = jnp.bfloat16

D_MODEL = 1024
NORM_EPS = 1e-6
MLA_HEADS = 8
MLA_Q_RANK = 256
MLA_KV_RANK = 128
MLA_NOPE = 64
MLA_ROPE = 32
MLA_V = 64
ROPE_BASE = 10000.0
DIFF_HEADS = 4
DIFF_QK = 64
DIFF_V = 128
DIL_CONFIGS = ((128, 1), (512, 4), (2048, 16))
DIL_HEADS_PER_GROUP = 4
HEAD_DIM = 64
SB_HEADS = 4
D_FF = 2816
CONV_WIDTH = 3

LANES = 128
SUBLANES = 8
BF16_ROWS = 16
VMEM_LIMIT_BYTES = 56 * 1024 * 1024
ROW_TILE = 512
ATT_TQ = 2048
SB_TQ = 512
ATT_TK = 256
DIL_T = 128
FF_CHUNK = 256
NEG = -1e30
LOG2E = 1.4426950408889634
SB_STOP = -110.0
SLOPE_PARTS = 3

_TRANS_B = (((1,), (1,)), ((), ()))


def _dot(a, b):
    return jnp.dot(a, b, preferred_element_type=F32)


def _dot_t(a, b):
    return lax.dot_general(a, b, _TRANS_B, preferred_element_type=F32)


def _rms(xf, g):
    ms = jnp.mean(xf * xf, axis=-1, keepdims=True)
    return xf * lax.rsqrt(ms + NORM_EPS) * g


def _params(sem):
    return pltpu.CompilerParams(dimension_semantics=sem, vmem_limit_bytes=VMEM_LIMIT_BYTES)


def _const_spec(shape):
    nd = len(shape)
    return pl.BlockSpec(shape, lambda *_: (0,) * nd, pipeline_mode=pl.Buffered(1))


def _store_transposed_tiles(dst_ref, src, heads, width, ones_rows):
    tk = ATT_TK
    per = width + ones_rows
    for b in range(src.shape[0] // tk):
        vt = src[b * tk:(b + 1) * tk, :].T.astype(BF16)
        for hh in range(heads):
            dst_ref[b, hh * per:hh * per + width, :] = vt[hh * width:(hh + 1) * width, :]
            if ones_rows:
                dst_ref[b, hh * per + width:(hh + 1) * per, :] = jnp.ones((ones_rows, tk), BF16)


def _proj_even_kernel(x_ref, g_ref, win_ref, cqg_ref, wuq_ref, ckvg_ref, wkv_ref,
                      cosq_ref, sinq_ref, cosk_ref, sink_ref, pos_ref,
                      qm_ref, km_ref, vmt_ref, dq_ref, dk_ref, dvt_ref):
    h = _rms(x_ref[...], g_ref[...]).astype(BF16)
    proj = _dot(h, win_ref[...])
    nq = MLA_HEADS * LANES
    cq = _rms(proj[:, 0:MLA_Q_RANK], cqg_ref[...]).astype(BF16)
    qq = _dot(cq, wuq_ref[...])
    cosq, sinq = cosq_ref[...], sinq_ref[...]
    for hh in range(MLA_HEADS):
        a = qq[:, hh * LANES:(hh + 1) * LANES]
        b = qq[:, nq + hh * LANES:nq + (hh + 1) * LANES]
        qm_ref[:, hh * LANES:(hh + 1) * LANES] = (a * cosq + b * sinq).astype(BF16)
    o = MLA_Q_RANK
    ckv = _rms(proj[:, o:o + MLA_KV_RANK], ckvg_ref[...]).astype(BF16)
    kv = _dot(ckv, wkv_ref[...])
    o += MLA_KV_RANK
    krc = proj[:, o:o + LANES] * cosk_ref[...] + proj[:, o + LANES:o + 2 * LANES] * sink_ref[...]
    for hh in range(MLA_HEADS):
        km_ref[:, hh * LANES:(hh + 1) * LANES] = (kv[:, hh * LANES:(hh + 1) * LANES] + krc).astype(BF16)
    _store_transposed_tiles(vmt_ref, kv[:, nq:nq + MLA_HEADS * MLA_V], MLA_HEADS, MLA_V, BF16_ROWS)
    o += 2 * LANES
    nd = DIFF_HEADS * 2 * DIFF_QK
    dq_ref[...] = (proj[:, o:o + nd] * (DIFF_QK ** -0.5 * LOG2E)).astype(BF16)
    o += nd
    pos = pos_ref[...]
    for hh in range(DIFF_HEADS):
        dk_ref[:, hh * 2 * LANES:hh * 2 * LANES + LANES] = proj[:, o + hh * LANES:o + (hh + 1) * LANES].astype(BF16)
        dk_ref[:, hh * 2 * LANES + LANES:(hh + 1) * 2 * LANES] = pos
    o += nd
    _store_transposed_tiles(dvt_ref, proj[:, o:o + DIFF_HEADS * DIFF_V], DIFF_HEADS, DIFF_V, BF16_ROWS)


def _proj_even(x, g, win, cqg, wuq, ckvg, wkv, cosq, sinq, cosk, sink, pos):
    s = x.shape[0]
    tm, tk = ROW_TILE, ATT_TK
    row = lambda c: pl.BlockSpec((tm, c), lambda i: (i, 0))
    tile_t = lambda r: pl.BlockSpec((tm // tk, r, tk), lambda i: (i, 0, 0))
    mla_rows = MLA_HEADS * (MLA_V + BF16_ROWS)
    diff_rows = DIFF_HEADS * (DIFF_V + BF16_ROWS)
    flat = lambda c: jax.ShapeDtypeStruct((s, c), BF16)
    tiled = lambda r: jax.ShapeDtypeStruct((s // tk, r, tk), BF16)
    return pl.pallas_call(
        _proj_even_kernel,
        grid=(s // tm,),
        in_specs=[row(D_MODEL), _const_spec(g.shape), _const_spec(win.shape), _const_spec(cqg.shape),
                  _const_spec(wuq.shape), _const_spec(ckvg.shape), _const_spec(wkv.shape),
                  row(LANES), row(LANES), row(LANES), row(LANES), _const_spec(pos.shape)],
        out_specs=[row(MLA_HEADS * LANES), row(MLA_HEADS * LANES), tile_t(mla_rows),
                   row(DIFF_HEADS * 2 * DIFF_QK), row(DIFF_HEADS * 2 * LANES), tile_t(diff_rows)],
        out_shape=[flat(MLA_HEADS * LANES), flat(MLA_HEADS * LANES), tiled(mla_rows),
                   flat(DIFF_HEADS * 2 * DIFF_QK), flat(DIFF_HEADS * 2 * LANES), tiled(diff_rows)],
        compiler_params=_params(("parallel",)),
        name="proj_even",
    )(x, g, win, cqg, wuq, ckvg, wkv, cosq, sinq, cosk, sink, pos)


def _flash_cols(chains, i, rows, tile_bias=None):
    tq, tk = ATT_TQ, ATT_TK
    ratio = tq // tk
    krow = lax.broadcasted_iota(jnp.int32, (tk, tq), 0)
    qcol = lax.broadcasted_iota(jnp.int32, (tk, tq), 1)

    def step(j, carry, diag):
        out = []
        for (qh, load_k, load_vt), (m, acc) in zip(chains, carry):
            s = _dot_t(load_k(j), qh)
            if diag is not None:
                s = jnp.where(krow + diag * tk <= qcol, s, NEG)
            mb = jnp.max(s, axis=0, keepdims=True)
            if tile_bias is not None:
                c = tile_bias(j)
                mb = mb + c
            m_new = jnp.maximum(m, mb)
            alpha = jnp.exp2(m - m_new)
            shift = m_new if tile_bias is None else m_new - c
            p = jnp.exp2(s - shift).astype(BF16)
            out.append((m_new, alpha * acc + _dot(load_vt(j), p)))
        return tuple(out)

    carry = tuple((jnp.full((1, tq), NEG, F32), jnp.zeros((rows, tq), F32)) for _ in chains)
    carry = lax.fori_loop(0, i * ratio, lambda j, c: step(j, c, None), carry)
    carry = lax.fori_loop(0, ratio, lambda d, c: step(i * ratio + d, c, d), carry)
    return [acc for _, acc in carry]


def _k_tile(k_ref, j, lanes=slice(None)):
    return k_ref[pl.ds(pl.multiple_of(j * ATT_TK, ATT_TK), ATT_TK), lanes]


def _mla_kernel(q_ref, k_ref, vt_ref, o_ref):
    i = pl.program_id(1)
    rows = MLA_V + BF16_ROWS
    def chain(hh):
        lanes = slice(hh * LANES, (hh + 1) * LANES)
        return (q_ref[:, lanes], lambda j: _k_tile(k_ref, j, lanes),
                lambda j: vt_ref[j, hh * rows:(hh + 1) * rows, :])

    accs = _flash_cols([chain(0), chain(1)], i, rows)
    halves = [acc[0:MLA_V] / acc[MLA_V:MLA_V + 1] for acc in accs]
    o_ref[...] = jnp.concatenate(halves, axis=0).T.astype(BF16)


def _mla_attention(qm, km, vmt):
    s = qm.shape[0]
    tq = ATT_TQ
    rows = 2 * (MLA_V + BF16_ROWS)
    return pl.pallas_call(
        _mla_kernel,
        grid=(MLA_HEADS // 2, s // tq),
        in_specs=[pl.BlockSpec((tq, 2 * LANES), lambda p, i: (i, p)),
                  pl.BlockSpec((s, 2 * LANES), lambda p, i: (0, p)),
                  pl.BlockSpec((vmt.shape[0], rows, ATT_TK), lambda p, i: (0, p, 0))],
        out_specs=pl.BlockSpec((tq, LANES), lambda p, i: (i, p)),
        out_shape=jax.ShapeDtypeStruct((s, MLA_HEADS * MLA_V), BF16),
        compiler_params=_params(("parallel", "parallel")),
        name="mla_attn",
    )(qm, km, vmt)


def _diff_kernel(lam_ref, slope_ref, qf_ref, g_ref, q_ref, k_ref, vt_ref, o_ref, *, lam_init):
    i = pl.program_id(1)
    tq = ATT_TQ
    lane = lax.broadcasted_iota(jnp.int32, (tq, LANES), 1)
    qf32 = q_ref[...].astype(F32)
    feat = qf_ref[...]
    zero = jnp.zeros_like(qf32)
    qa = jnp.concatenate([jnp.where(lane < DIFF_QK, qf32, zero).astype(BF16), feat], axis=1)
    qb = jnp.concatenate([jnp.where(lane >= DIFF_QK, qf32, zero).astype(BF16), feat], axis=1)
    slope = slope_ref[0:1, 0:1]
    rows = DIFF_V + BF16_ROWS
    tile_bias = lambda j: slope * (j * ATT_TK - i * tq).astype(F32)
    load_k = lambda j: _k_tile(k_ref, j)
    load_vt = lambda j: vt_ref[j]
    acc1, acc2 = _flash_cols([(qa, load_k, load_vt), (qb, load_k, load_vt)], i, rows, tile_bias)
    lp = lam_ref[...]
    s1 = jnp.sum(lp[0:1, :] * lp[1:2, :], axis=1, keepdims=True)
    s2 = jnp.sum(lp[2:3, :] * lp[3:4, :], axis=1, keepdims=True)
    lam = jnp.exp(s1) - jnp.exp(s2) + lam_init
    ot = acc1[0:DIFF_V] / acc1[DIFF_V:DIFF_V + 1] - lam * (acc2[0:DIFF_V] / acc2[DIFF_V:DIFF_V + 1])
    o_ref[...] = (_rms(ot.T, g_ref[...]) * (1.0 - lam_init)).astype(BF16)


def _diff_attention(lam_p, slope_arr, qfeat, subln_g, dq, dk, dvt, lam_init):
    s = dq.shape[0]
    tq = ATT_TQ
    return pl.pallas_call(
        functools.partial(_diff_kernel, lam_init=lam_init),
        grid=(DIFF_HEADS, s // tq),
        in_specs=[pl.BlockSpec(lam_p.shape, lambda h, i: (0, 0)),
                  pl.BlockSpec((None, SUBLANES, LANES), lambda h, i: (h, 0, 0)),
                  pl.BlockSpec((None, tq, LANES), lambda h, i: (h, 0, 0)),
                  pl.BlockSpec(subln_g.shape, lambda h, i: (0, 0)),
                  pl.BlockSpec((tq, LANES), lambda h, i: (i, h)),
                  pl.BlockSpec((s, 2 * LANES), lambda h, i: (0, h)),
                  pl.BlockSpec((dvt.shape[0], DIFF_V + BF16_ROWS, ATT_TK), lambda h, i: (0, h, 0))],
        out_specs=pl.BlockSpec((tq, LANES), lambda h, i: (i, h)),
        out_shape=jax.ShapeDtypeStruct((s, DIFF_HEADS * DIFF_V), BF16),
        compiler_params=_params(("parallel", "parallel")),
        name="diff_attn",
    )(lam_p, slope_arr, qfeat, subln_g, dq, dk, dvt)


def _shift_rows(u, k, prev):
    top = jnp.where(lax.broadcasted_iota(jnp.int32, prev.shape, 0) < k,
                    pltpu.roll(prev, k, 0), pltpu.roll(u[0:SUBLANES, :], k, 0))
    return jnp.concatenate([top, pltpu.roll(u, k, 0)[SUBLANES:, :]], axis=0)


def _ffn_tile(x1, fpre, wup_ref, cw_ref, cb_ref, wdn_ref, fpost, carry_ref):
    tm = x1.shape[0]
    hf = _rms(x1, fpre).astype(BF16)
    acc = jnp.zeros((tm, D_MODEL), F32)
    for c in range(D_FF // FF_CHUNK):
        ys = []
        for part in range(2):
            c0 = part * D_FF + c * FF_CHUNK
            u = _dot(hf, wup_ref[:, c0:c0 + FF_CHUNK])
            prev = carry_ref[:, c0:c0 + FF_CHUNK]
            carry_ref[:, c0:c0 + FF_CHUNK] = u[tm - SUBLANES:, :]
            ys.append(cw_ref[2:3, c0:c0 + FF_CHUNK] * u
                      + cw_ref[1:2, c0:c0 + FF_CHUNK] * _shift_rows(u, 1, prev)
                      + cw_ref[0:1, c0:c0 + FF_CHUNK] * _shift_rows(u, 2, prev)
                      + cb_ref[:, c0:c0 + FF_CHUNK])
        gate, up = ys
        act = (gate * (1.0 / (1.0 + jnp.exp(-gate))) * up).astype(BF16)
        acc = acc + _dot(act, wdn_ref[c * FF_CHUNK:(c + 1) * FF_CHUNK, :])
    return x1 + _rms(acc, fpost)


def _out_ffn_even_kernel(x_ref, a_ref, b_ref, wout_ref, postg_ref, fpre_ref, wup_ref, cw_ref, cb_ref,
                         wdn_ref, fpost_ref, o_ref, carry_ref):
    @pl.when(pl.program_id(0) == 0)
    def _():
        carry_ref[...] = jnp.zeros_like(carry_ref)

    na = a_ref.shape[1]
    mix = _dot(a_ref[...], wout_ref[0:na, :]) + _dot(b_ref[...], wout_ref[na:, :])
    x1 = x_ref[...] + _rms(mix, postg_ref[...])
    o_ref[...] = _ffn_tile(x1, fpre_ref[...], wup_ref, cw_ref, cb_ref, wdn_ref, fpost_ref[...], carry_ref)


def _out_ffn_odd_kernel(x_ref, o0_ref, l0_ref, o1_ref, l1_ref, o2_ref, l2_ref, bt_ref, wout_ref, postg_ref,
                        fpre_ref, wup_ref, cw_ref, cb_ref, wdn_ref, fpost_ref, o_ref, carry_ref):
    @pl.when(pl.program_id(0) == 0)
    def _():
        carry_ref[...] = jnp.zeros_like(carry_ref)

    l0, l1, l2 = l0_ref[...], l1_ref[...], l2_ref[...]
    m = jnp.maximum(jnp.maximum(l0, l1), l2)
    e0, e1, e2 = jnp.exp(l0 - m), jnp.exp(l1 - m), jnp.exp(l2 - m)
    dil = (e0 * o0_ref[...] + e1 * o1_ref[...] + e2 * o2_ref[...]) / (e0 + e1 + e2)
    na = dil.shape[1]
    sb = bt_ref[...].T
    mix = _dot(dil.astype(BF16), wout_ref[0:na, :]) + _dot(sb.astype(BF16), wout_ref[na:, :])
    x1 = x_ref[...] + _rms(mix, postg_ref[...])
    o_ref[...] = _ffn_tile(x1, fpre_ref[...], wup_ref, cw_ref, cb_ref, wdn_ref, fpost_ref[...], carry_ref)


def _out_ffn(kernel, name, x, acts, acts_t, consts):
    s = x.shape[0]
    tm = ROW_TILE
    row = lambda c: pl.BlockSpec((tm, c), lambda i: (i, 0))
    col = lambda r: pl.BlockSpec((r, tm), lambda i: (0, i))
    return pl.pallas_call(
        kernel,
        grid=(s // tm,),
        in_specs=([row(D_MODEL)] + [row(a.shape[1]) for a in acts] + [col(a.shape[0]) for a in acts_t]
                  + [_const_spec(c.shape) for c in consts]),
        out_specs=row(D_MODEL),
        out_shape=jax.ShapeDtypeStruct((s, D_MODEL), F32),
        scratch_shapes=[pltpu.VMEM((SUBLANES, 2 * D_FF), F32)],
        compiler_params=_params(("arbitrary",)),
        name=name,
    )(x, *acts, *acts_t, *consts)


def _proj_odd_kernel(x_ref, g_ref, win_ref, *out_refs):
    h = _rms(x_ref[...], g_ref[...]).astype(BF16)
    proj = _dot(h, win_ref[...])
    c = DIL_HEADS_PER_GROUP * HEAD_DIM
    scale = HEAD_DIM ** -0.5
    for n, ref in enumerate(out_refs[:-1]):
        blk = proj[:, n * c:(n + 1) * c]
        if n < 3 or n == 9:
            blk = blk * scale
        ref[...] = blk.astype(BF16)
    _store_transposed_tiles(out_refs[-1], proj[:, 11 * c:12 * c], SB_HEADS, HEAD_DIM, 0)


def _proj_odd(x, g, win):
    s = x.shape[0]
    tm, tk = ROW_TILE, ATT_TK
    c = DIL_HEADS_PER_GROUP * HEAD_DIM
    row = lambda w: pl.BlockSpec((tm, w), lambda i: (i, 0))
    return pl.pallas_call(
        _proj_odd_kernel,
        grid=(s // tm,),
        in_specs=[row(D_MODEL), _const_spec(g.shape), _const_spec(win.shape)],
        out_specs=[row(c)] * 11 + [pl.BlockSpec((tm // tk, c, tk), lambda i: (i, 0, 0))],
        out_shape=[jax.ShapeDtypeStruct((s, c), BF16)] * 11 + [jax.ShapeDtypeStruct((s // tk, c, tk), BF16)],
        compiler_params=_params(("parallel",)),
        name="proj_odd",
    )(x, g, win)


def _dilated_kernel(bias_ref, q_ref, kp_ref, kc_ref, vp_ref, vc_ref, o_ref, lse_ref, *, nub):
    t = DIL_T
    ub = pl.program_id(0) % nub
    c = DIL_HEADS_PER_GROUP * HEAD_DIM
    qf = q_ref[...].astype(F32)
    kcat = jnp.concatenate([kp_ref[...], kc_ref[...]], axis=0)
    vcat = jnp.concatenate([vp_ref[...], vc_ref[...]], axis=0)
    lane = lax.broadcasted_iota(jnp.int32, (t, c), 1)
    col = lax.broadcasted_iota(jnp.int32, (t, 2 * t), 1)
    no_prev = col < jnp.where(ub == 0, t, 0)
    out = jnp.zeros((t, c), F32)
    lse = jnp.zeros((t, c), F32)
    for hh in range(DIL_HEADS_PER_GROUP):
        mine = jnp.logical_and(lane >= hh * HEAD_DIM, lane < (hh + 1) * HEAD_DIM)
        qh = jnp.where(mine, qf, 0.0).astype(BF16)
        s = _dot_t(qh, kcat) + bias_ref[hh]
        s = jnp.where(no_prev, NEG, s)
        m = jnp.max(s, axis=1, keepdims=True)
        e = jnp.exp(s - m)
        den = jnp.sum(e, axis=1, keepdims=True)
        oh = _dot(e.astype(BF16), vcat) / den
        out = jnp.where(mine, oh, out)
        lse = jnp.where(mine, m + jnp.log(den), lse)
    o_ref[...] = out
    lse_ref[...] = lse


def _dilated_group(q, k, v, dil, slopes):
    s, c = q.shape
    t = DIL_T
    nub = s // dil // t
    a = np.arange(t)[:, None]
    cc = np.arange(2 * t)[None, :]
    steps = t + a - cc
    ok = (steps >= 0) & (steps <= t)
    dist = (steps * dil).astype(np.float32)
    bias = np.where(ok[None], -np.asarray(slopes, np.float32)[:, None, None] * dist[None], np.float32(NEG))
    bias = jnp.asarray(bias, F32)
    view = lambda z: z.reshape(s // dil, dil * c)
    cur = pl.BlockSpec((t, c), lambda b: (b % nub, b // nub))
    prev = pl.BlockSpec((t, c), lambda b: (jnp.maximum(b % nub - 1, 0), b // nub))
    o, lse = pl.pallas_call(
        functools.partial(_dilated_kernel, nub=nub),
        grid=(dil * nub,),
        in_specs=[_const_spec(bias.shape), cur, prev, cur, prev, cur],
        out_specs=[cur, cur],
        out_shape=[jax.ShapeDtypeStruct((s // dil, dil * c), F32)] * 2,
        compiler_params=_params(("parallel",)),
        name=f"dilated_d{dil}",
    )(bias, view(q), view(k), view(k), view(v), view(v))
    return o.reshape(s, c), lse.reshape(s, c)


def _sb_kernel(tri_ref, q_ref, k_ref, vt_ref, o_ref):
    hh = pl.program_id(0)
    i = pl.program_id(1)
    tq, tk = SB_TQ, ATT_TK
    ratio = tq // tk
    c = SB_HEADS * HEAD_DIM
    tri = tri_ref[...]
    lane = lax.broadcasted_iota(jnp.int32, (tq, c), 1)
    mine = jnp.logical_and(lane >= hh * HEAD_DIM, lane < (hh + 1) * HEAD_DIM)
    qh = jnp.where(mine, q_ref[...].astype(F32), 0.0).astype(BF16)
    krow = lax.broadcasted_iota(jnp.int32, (tk, tq), 0)
    qcol = lax.broadcasted_iota(jnp.int32, (tk, tq), 1)

    def step(j, carry, diag):
        r, acc = carry
        z = _dot_t(_k_tile(k_ref, j), qh)
        sp = jnp.log1p(jnp.exp(-jnp.abs(z)))
        log_beta = jnp.minimum(z, 0.0) - sp
        log_keep = log_beta - z
        if diag is not None:
            strict = krow + diag * tk < qcol
            log_keep = jnp.where(strict, log_keep, 0.0)
        hi = log_keep.astype(BF16)
        lo = (log_keep - hi.astype(F32)).astype(BF16)
        aft = _dot(tri, jnp.concatenate([hi, lo], axis=0))
        a = jnp.exp(log_beta + aft[0:tk] + r)
        if diag is not None:
            a = jnp.where(strict, a, 0.0)
        acc = acc + _dot(vt_ref[j], a.astype(BF16))
        return r + aft[tk:tk + 1], acc

    carry = (jnp.zeros((1, tq), F32), jnp.zeros((HEAD_DIM, tq), F32))
    n_full = i * ratio
    carry = lax.fori_loop(0, ratio, lambda dd, cr: step(n_full + ratio - 1 - dd, cr, ratio - 1 - dd), carry)

    def more(state):
        n, r, _ = state
        return jnp.logical_and(n < n_full, jnp.max(r) > SB_STOP)

    def walk(state):
        n, r, acc = state
        r, acc = step(n_full - 1 - n, (r, acc), None)
        return n + 1, r, acc

    o_ref[...] = lax.while_loop(more, walk, (jnp.int32(0),) + carry)[2]


def _stickbreak(q, k, vt):
    s, c = q.shape
    tq, tk = SB_TQ, ATT_TK
    j = np.arange(tk)
    later = (j[None, :] > j[:, None]).astype(np.float32)
    tri = np.concatenate([np.concatenate([later, later], axis=1), np.ones((BF16_ROWS, 2 * tk), np.float32)], axis=0)
    tri = jnp.asarray(tri, BF16)
    return pl.pallas_call(
        _sb_kernel,
        grid=(SB_HEADS, s // tq),
        in_specs=[_const_spec(tri.shape),
                  pl.BlockSpec((tq, c), lambda h, i: (i, 0)),
                  _const_spec((s, c)),
                  pl.BlockSpec((s // tk, HEAD_DIM, tk), lambda h, i: (0, h, 0))],
        out_specs=pl.BlockSpec((HEAD_DIM, tq), lambda h, i: (h, i)),
        out_shape=jax.ShapeDtypeStruct((c, s), F32),
        compiler_params=_params(("parallel", "parallel")),
        name="stickbreak",
    )(tri, q, k, vt)


def _alibi_slopes(n):
    return 2.0 ** (-8.0 * np.arange(1, n + 1) / n)


def _pad_cols(w, lo, width):
    return jnp.pad(w, ((0, 0), (lo, width - lo - w.shape[1])))


def _rot_half_cols(w):
    half = w.shape[1] // 2
    return jnp.concatenate([-w[:, half:], w[:, :half]], axis=1)


def _even_weights(w_in, w_uq, w_ukv):
    o_kr = MLA_Q_RANK + MLA_KV_RANK
    w_kr = w_in[:, o_kr:o_kr + MLA_ROPE]
    win = jnp.concatenate([w_in[:, :o_kr], _pad_cols(w_kr, MLA_NOPE, LANES),
                           _pad_cols(_rot_half_cols(w_kr), MLA_NOPE, LANES),
                           w_in[:, o_kr + MLA_ROPE:]], axis=1)
    hd = MLA_NOPE + MLA_ROPE
    q_plain, q_rot = [], []
    for hh in range(MLA_HEADS):
        wh = w_uq[:, hh * hd:(hh + 1) * hd]
        q_plain.append(_pad_cols(wh, 0, LANES))
        q_rot.append(_pad_cols(_rot_half_cols(wh[:, MLA_NOPE:]), MLA_NOPE, LANES))
    wuq = jnp.concatenate(q_plain + q_rot, axis=1)
    hk = MLA_NOPE + MLA_V
    k_cols = [_pad_cols(w_ukv[:, hh * hk:hh * hk + MLA_NOPE], 0, LANES) for hh in range(MLA_HEADS)]
    v_cols = [w_ukv[:, hh * hk + MLA_NOPE:(hh + 1) * hk] for hh in range(MLA_HEADS)]
    wkv = jnp.concatenate(k_cols + v_cols, axis=1)
    return win.astype(BF16), wuq.astype(BF16), wkv.astype(BF16)


def _rope_tables(s):
    half = MLA_ROPE // 2
    inv = ROPE_BASE ** (-jnp.arange(half, dtype=F32) / half)
    ang = jnp.arange(s).astype(F32)[:, None] * inv
    cos2 = jnp.tile(jnp.cos(ang), (1, 2))
    sin2 = jnp.tile(jnp.sin(ang), (1, 2))
    scale = (MLA_NOPE + MLA_ROPE) ** -0.5 * LOG2E
    tail = LANES - MLA_NOPE - MLA_ROPE
    ones, zeros, ztail = jnp.ones((s, MLA_NOPE), F32), jnp.zeros((s, MLA_NOPE), F32), jnp.zeros((s, tail), F32)
    cosq = jnp.concatenate([ones, cos2, ztail], axis=1) * scale
    sinq = jnp.concatenate([zeros, sin2, ztail], axis=1) * scale
    cosk = jnp.concatenate([zeros, cos2, ztail], axis=1)
    sink = jnp.concatenate([zeros, sin2, ztail], axis=1)
    return cosq, sinq, cosk, sink


def _row2d(v):
    return v.reshape(1, -1).astype(F32)


def kernel(x, ev_pre_g, ev_w_in, ev_cq_g, ev_w_uq, ev_ckv_g, ev_w_ukv, ev_lam_q1, ev_lam_k1, ev_lam_q2,
           ev_lam_k2, ev_subln_g, ev_w_out, ev_post_g, od_pre_g, od_w_in, od_w_out, od_post_g, ffn_pre_g,
           ffn_w_up, ffn_conv_w, ffn_conv_b, ffn_w_down, ffn_post_g):
    b, s, _ = x.shape
    assert b == 1 and s % max(ROW_TILE, ATT_TQ, SB_TQ, DIL_T * DIL_CONFIGS[-1][1]) == 0
    assert ROW_TILE % ATT_TK == 0 and ATT_TQ % ATT_TK == 0 and SB_TQ % ATT_TK == 0
    assert ATT_TK <= 256
    xs = x[0]

    def ffn_consts(i):
        return [_row2d(ffn_pre_g[i]), ffn_w_up[i].astype(BF16), ffn_conv_w[i].astype(F32),
                _row2d(ffn_conv_b[i]), ffn_w_down[i].astype(BF16), _row2d(ffn_post_g[i])]

    win, wuq, wkv = _even_weights(ev_w_in[0], ev_w_uq[0], ev_w_ukv[0])
    cosq, sinq, cosk, sink = _rope_tables(s)
    pos_col = (jnp.arange(ROW_TILE) % ATT_TK).astype(BF16)[:, None]
    pos_tile = jnp.pad(jnp.tile(pos_col, (1, SLOPE_PARTS)), ((0, 0), (0, LANES - SLOPE_PARTS)))
    qm, km, vmt, dq, dk, dvt = _proj_even(xs, _row2d(ev_pre_g[0]), win, _row2d(ev_cq_g[0]), wuq,
                                          _row2d(ev_ckv_g[0]), wkv, cosq, sinq, cosk, sink, pos_tile)
    o_mla = _mla_attention(qm, km, vmt)
    lam_p = jnp.pad(jnp.stack([ev_lam_q1[0], ev_lam_k1[0], ev_lam_q2[0], ev_lam_k2[0]]).astype(F32),
                    ((0, SUBLANES - 4), (0, LANES - DIFF_QK)))
    d_slopes = _alibi_slopes(DIFF_HEADS) * LOG2E
    slope_arr = jnp.asarray(np.broadcast_to(d_slopes[:, None, None], (DIFF_HEADS, SUBLANES, LANES)), F32)
    qfeat = jnp.zeros((DIFF_HEADS, ATT_TQ, LANES), F32)
    rest = jnp.asarray(d_slopes, F32)
    for part in range(SLOPE_PARTS):
        piece = rest.astype(BF16).astype(F32)
        qfeat = qfeat.at[:, :, part].set(piece[:, None])
        rest = rest - piece
    lam_init = 0.8 - 0.6 * math.exp(-0.3 * 0)
    o_diff = _diff_attention(lam_p, slope_arr, qfeat.astype(BF16), _row2d(ev_subln_g[0]),
                             dq, dk, dvt, lam_init)
    x2 = _out_ffn(_out_ffn_even_kernel, "out_ffn_even", xs, [o_mla, o_diff], [],
                  [ev_w_out[0].astype(BF16), _row2d(ev_post_g[0])] + ffn_consts(0))

    p = _proj_odd(x2, _row2d(od_pre_g[0]), od_w_in[0].astype(BF16))
    slopes = _alibi_slopes(len(DIL_CONFIGS) * DIL_HEADS_PER_GROUP).reshape(len(DIL_CONFIGS), -1)
    dil_outs = []
    for gi, (_, dil) in enumerate(DIL_CONFIGS):
        dil_outs += list(_dilated_group(p[gi], p[3 + gi], p[6 + gi], dil, slopes[gi]))
    o_sbt = _stickbreak(p[9], p[10], p[11])
    x4 = _out_ffn(_out_ffn_odd_kernel, "out_ffn_odd", x2, dil_outs, [o_sbt],
                  [od_w_out[0].astype(BF16), _row2d(od_post_g[0])] + ffn_consts(1))
    return x4[None]
```

```python
import functools
import math

import numpy as np
import jax
import jax.numpy as jnp
from jax import lax
from jax.experimental import pallas as pl
from jax.experimental.pallas import tpu as pltpu

F32 = jnp.float32
BF16 = jnp.bfloat16

D_MODEL = 1024
NORM_EPS = 1e-6
MLA_HEADS = 8
MLA_Q_RANK = 256
MLA_KV_RANK = 128
MLA_NOPE = 64
MLA_ROPE = 32
MLA_V = 64
ROPE_BASE = 10000.0
DIFF_HEADS = 4
DIFF_QK = 64
DIFF_V = 128
DIL_CONFIGS = ((128, 1), (512, 4), (2048, 16))
DIL_HEADS_PER_GROUP = 4
HEAD_DIM = 64
SB_HEADS = 4
D_FF = 2816
CONV_WIDTH = 3

LANES = 128
SUBLANES = 8
BF16_ROWS = 16
VMEM_LIMIT_BYTES = 56 * 1024 * 1024
ROW_TILE = 512
ATT_TQ = 2048
MAIN_UNROLL = 4
SB_TQ = 512
ATT_TK = 256
DIL_T = 128
DIL_TB = 512
FF_CHUNK = 256
NEG = -1e30
LOG2E = 1.4426950408889634
SB_STOP = -110.0
SLOPE_PARTS = 3

_TRANS_B = (((1,), (1,)), ((), ()))


def _dot(a, b):
    return jnp.dot(a, b, preferred_element_type=F32)


def _dot_t(a, b):
    return lax.dot_general(a, b, _TRANS_B, preferred_element_type=F32)


def _rms(xf, g):
    ms = jnp.mean(xf * xf, axis=-1, keepdims=True)
    return xf * lax.rsqrt(ms + NORM_EPS) * g


def _params(sem):
    return pltpu.CompilerParams(dimension_semantics=sem, vmem_limit_bytes=VMEM_LIMIT_BYTES)


def _const_spec(shape):
    nd = len(shape)
    return pl.BlockSpec(shape, lambda *_: (0,) * nd, pipeline_mode=pl.Buffered(1))


def _store_transposed_tiles(dst_ref, src, heads, width, ones_rows):
    tk = ATT_TK
    per = width + ones_rows
    for b in range(src.shape[0] // tk):
        vt = src[b * tk:(b + 1) * tk, :].T.astype(BF16)
        for hh in range(heads):
            dst_ref[b, hh * per:hh * per + width, :] = vt[hh * width:(hh + 1) * width, :]
            if ones_rows:
                dst_ref[b, hh * per + width:(hh + 1) * per, :] = jnp.ones((ones_rows, tk), BF16)


def _proj_even_kernel(x_ref, g_ref, win_ref, cqg_ref, wuq_ref, ckvg_ref, wkv_ref,
                      cosq_ref, sinq_ref, cosk_ref, sink_ref, pos_ref,
                      qm_ref, km_ref, vmt_ref, dq_ref, dk_ref, dvt_ref):
    h = _rms(x_ref[...], g_ref[...]).astype(BF16)
    proj = _dot(h, win_ref[...])
    nq = MLA_HEADS * LANES
    cq = _rms(proj[:, 0:MLA_Q_RANK], cqg_ref[...]).astype(BF16)
    qq = _dot(cq, wuq_ref[...])
    cosq, sinq = cosq_ref[...], sinq_ref[...]
    for hh in range(MLA_HEADS):
        a = qq[:, hh * LANES:(hh + 1) * LANES]
        b = qq[:, nq + hh * LANES:nq + (hh + 1) * LANES]
        qm_ref[:, hh * LANES:(hh + 1) * LANES] = (a * cosq + b * sinq).astype(BF16)
    o = MLA_Q_RANK
    ckv = _rms(proj[:, o:o + MLA_KV_RANK], ckvg_ref[...]).astype(BF16)
    kv = _dot(ckv, wkv_ref[...])
    o += MLA_KV_RANK
    krc = proj[:, o:o + LANES] * cosk_ref[...] + proj[:, o + LANES:o + 2 * LANES] * sink_ref[...]
    for hh in range(MLA_HEADS):
        km_ref[:, hh * LANES:(hh + 1) * LANES] = (kv[:, hh * LANES:(hh + 1) * LANES] + krc).astype(BF16)
    _store_transposed_tiles(vmt_ref, kv[:, nq:nq + MLA_HEADS * MLA_V], MLA_HEADS, MLA_V, BF16_ROWS)
    o += 2 * LANES
    nd = DIFF_HEADS * 2 * DIFF_QK
    dq_ref[...] = (proj[:, o:o + nd] * (DIFF_QK ** -0.5 * LOG2E)).astype(BF16)
    o += nd
    pos = pos_ref[...]
    for hh in range(DIFF_HEADS):
        dk_ref[:, hh * 2 * LANES:hh * 2 * LANES + LANES] = proj[:, o + hh * LANES:o + (hh + 1) * LANES].astype(BF16)
        dk_ref[:, hh * 2 * LANES + LANES:(hh + 1) * 2 * LANES] = pos
    o += nd
    _store_transposed_tiles(dvt_ref, proj[:, o:o + DIFF_HEADS * DIFF_V], DIFF_HEADS, DIFF_V, BF16_ROWS)


def _proj_even(x, g, win, cqg, wuq, ckvg, wkv, cosq, sinq, cosk, sink, pos):
    s = x.shape[0]
    tm, tk = ROW_TILE, ATT_TK
    row = lambda c: pl.BlockSpec((tm, c), lambda i: (i, 0))
    tile_t = lambda r: pl.BlockSpec((tm // tk, r, tk), lambda i: (i, 0, 0))
    mla_rows = MLA_HEADS * (MLA_V + BF16_ROWS)
    diff_rows = DIFF_HEADS * (DIFF_V + BF16_ROWS)
    flat = lambda c: jax.ShapeDtypeStruct((s, c), BF16)
    tiled = lambda r: jax.ShapeDtypeStruct((s // tk, r, tk), BF16)
    return pl.pallas_call(
        _proj_even_kernel,
        grid=(s // tm,),
        in_specs=[row(D_MODEL), _const_spec(g.shape), _const_spec(win.shape), _const_spec(cqg.shape),
                  _const_spec(wuq.shape), _const_spec(ckvg.shape), _const_spec(wkv.shape),
                  row(LANES), row(LANES), row(LANES), row(LANES), _const_spec(pos.shape)],
        out_specs=[row(MLA_HEADS * LANES), row(MLA_HEADS * LANES), tile_t(mla_rows),
                   row(DIFF_HEADS * 2 * DIFF_QK), row(DIFF_HEADS * 2 * LANES), tile_t(diff_rows)],
        out_shape=[flat(MLA_HEADS * LANES), flat(MLA_HEADS * LANES), tiled(mla_rows),
                   flat(DIFF_HEADS * 2 * DIFF_QK), flat(DIFF_HEADS * 2 * LANES), tiled(diff_rows)],
        compiler_params=_params(("parallel",)),
        name="proj_even",
    )(x, g, win, cqg, wuq, ckvg, wkv, cosq, sinq, cosk, sink, pos)


def _flash_cols(chains, i, rows, tile_bias=None):
    tq, tk = ATT_TQ, ATT_TK
    ratio = tq // tk
    krow = lax.broadcasted_iota(jnp.int32, (tk, tq), 0)
    qcol = lax.broadcasted_iota(jnp.int32, (tk, tq), 1)

    def step(j, carry, diag):
        out = []
        for (qh, load_k, load_vt), (m, acc) in zip(chains, carry):
            s = _dot_t(load_k(j), qh)
            if diag is not None:
                s = jnp.where(krow + diag * tk <= qcol, s, NEG)
            mb = jnp.max(s, axis=0, keepdims=True)
            if tile_bias is not None:
                c = tile_bias(j)
                mb = mb + c
            m_new = jnp.maximum(m, mb)
            alpha = jnp.exp2(m - m_new)
            shift = m_new if tile_bias is None else m_new - c
            p = jnp.exp2(s - shift).astype(BF16)
            out.append((m_new, alpha * acc + _dot(load_vt(j), p)))
        return tuple(out)

    carry = tuple((jnp.full((1, tq), NEG, F32), jnp.zeros((rows, tq), F32)) for _ in chains)

    def group(g, c):
        for t in range(MAIN_UNROLL):
            c = step(g * MAIN_UNROLL + t, c, None)
        return c

    carry = lax.fori_loop(0, i * (ratio // MAIN_UNROLL), group, carry)
    carry = lax.fori_loop(0, ratio, lambda d, c: step(i * ratio + d, c, d), carry)
    return [acc for _, acc in carry]


def _k_tile(k_ref, j, lanes=slice(None)):
    return k_ref[pl.ds(pl.multiple_of(j * ATT_TK, ATT_TK), ATT_TK), lanes]


def _mla_kernel(q_ref, k_ref, vt_ref, o_ref):
    i = pl.program_id(1)
    rows = MLA_V + BF16_ROWS
    def chain(hh):
        lanes = slice(hh * LANES, (hh + 1) * LANES)
        return (q_ref[:, lanes], lambda j: _k_tile(k_ref, j, lanes),
                lambda j: vt_ref[j, hh * rows:(hh + 1) * rows, :])

    accs = _flash_cols([chain(0), chain(1)], i, rows)
    halves = [acc[0:MLA_V] / acc[MLA_V:MLA_V + 1] for acc in accs]
    o_ref[...] = jnp.concatenate(halves, axis=0).T.astype(BF16)


def _mla_attention(qm, km, vmt):
    s = qm.shape[0]
    tq = ATT_TQ
    rows = 2 * (MLA_V + BF16_ROWS)
    return pl.pallas_call(
        _mla_kernel,
        grid=(MLA_HEADS // 2, s // tq),
        in_specs=[pl.BlockSpec((tq, 2 * LANES), lambda p, i: (i, p)),
                  pl.BlockSpec((s, 2 * LANES), lambda p, i: (0, p)),
                  pl.BlockSpec((vmt.shape[0], rows, ATT_TK), lambda p, i: (0, p, 0))],
        out_specs=pl.BlockSpec((tq, LANES), lambda p, i: (i, p)),
        out_shape=jax.ShapeDtypeStruct((s, MLA_HEADS * MLA_V), BF16),
        compiler_params=_params(("parallel", "parallel")),
        name="mla_attn",
    )(qm, km, vmt)


def _diff_kernel(lam_ref, slope_ref, qf_ref, g_ref, q_ref, k_ref, vt_ref, o_ref, *, lam_init):
    i = pl.program_id(1)
    tq = ATT_TQ
    lane = lax.broadcasted_iota(jnp.int32, (tq, LANES), 1)
    qf32 = q_ref[...].astype(F32)
    feat = qf_ref[...]
    zero = jnp.zeros_like(qf32)
    qa = jnp.concatenate([jnp.where(lane < DIFF_QK, qf32, zero).astype(BF16), feat], axis=1)
    qb = jnp.concatenate([jnp.where(lane >= DIFF_QK, qf32, zero).astype(BF16), feat], axis=1)
    slope = slope_ref[0:1, 0:1]
    rows = DIFF_V + BF16_ROWS
    tile_bias = lambda j: slope * (j * ATT_TK - i * tq).astype(F32)
    load_k = lambda j: _k_tile(k_ref, j)
    load_vt = lambda j: vt_ref[j]
    acc1, acc2 = _flash_cols([(qa, load_k, load_vt), (qb, load_k, load_vt)], i, rows, tile_bias)
    lp = lam_ref[...]
    s1 = jnp.sum(lp[0:1, :] * lp[1:2, :], axis=1, keepdims=True)
    s2 = jnp.sum(lp[2:3, :] * lp[3:4, :], axis=1, keepdims=True)
    lam = jnp.exp(s1) - jnp.exp(s2) + lam_init
    ot = acc1[0:DIFF_V] / acc1[DIFF_V:DIFF_V + 1] - lam * (acc2[0:DIFF_V] / acc2[DIFF_V:DIFF_V + 1])
    o_ref[...] = (_rms(ot.T, g_ref[...]) * (1.0 - lam_init)).astype(BF16)


def _diff_attention(lam_p, slope_arr, qfeat, subln_g, dq, dk, dvt, lam_init):
    s = dq.shape[0]
    tq = ATT_TQ
    return pl.pallas_call(
        functools.partial(_diff_kernel, lam_init=lam_init),
        grid=(DIFF_HEADS, s // tq),
        in_specs=[pl.BlockSpec(lam_p.shape, lambda h, i: (0, 0)),
                  pl.BlockSpec((None, SUBLANES, LANES), lambda h, i: (h, 0, 0)),
                  pl.BlockSpec((None, tq, LANES), lambda h, i: (h, 0, 0)),
                  pl.BlockSpec(subln_g.shape, lambda h, i: (0, 0)),
                  pl.BlockSpec((tq, LANES), lambda h, i: (i, h)),
                  pl.BlockSpec((s, 2 * LANES), lambda h, i: (0, h)),
                  pl.BlockSpec((dvt.shape[0], DIFF_V + BF16_ROWS, ATT_TK), lambda h, i: (0, h, 0))],
        out_specs=pl.BlockSpec((tq, LANES), lambda h, i: (i, h)),
        out_shape=jax.ShapeDtypeStruct((s, DIFF_HEADS * DIFF_V), BF16),
        compiler_params=_params(("parallel", "parallel")),
        name="diff_attn",
    )(lam_p, slope_arr, qfeat, subln_g, dq, dk, dvt)


def _shift_rows(u, k, prev):
    top = jnp.where(lax.broadcasted_iota(jnp.int32, prev.shape, 0) < k,
                    pltpu.roll(prev, k, 0), pltpu.roll(u[0:SUBLANES, :], k, 0))
    return jnp.concatenate([top, pltpu.roll(u, k, 0)[SUBLANES:, :]], axis=0)


def _ffn_tile(x1, fpre, wup_ref, cw_ref, cb_ref, wdn_ref, fpost, carry_ref):
    tm = x1.shape[0]
    hf = _rms(x1, fpre).astype(BF16)
    acc = jnp.zeros((tm, D_MODEL), F32)
    for c in range(D_FF // FF_CHUNK):
        ys = []
        for part in range(2):
            c0 = part * D_FF + c * FF_CHUNK
            u = _dot(hf, wup_ref[:, c0:c0 + FF_CHUNK])
            prev = carry_ref[:, c0:c0 + FF_CHUNK]
            carry_ref[:, c0:c0 + FF_CHUNK] = u[tm - SUBLANES:, :]
            ys.append(cw_ref[2:3, c0:c0 + FF_CHUNK] * u
                      + cw_ref[1:2, c0:c0 + FF_CHUNK] * _shift_rows(u, 1, prev)
                      + cw_ref[0:1, c0:c0 + FF_CHUNK] * _shift_rows(u, 2, prev)
                      + cb_ref[:, c0:c0 + FF_CHUNK])
        gate, up = ys
        act = (gate * (1.0 / (1.0 + jnp.exp(-gate))) * up).astype(BF16)
        acc = acc + _dot(act, wdn_ref[c * FF_CHUNK:(c + 1) * FF_CHUNK, :])
    return x1 + _rms(acc, fpost)


def _out_ffn_even_kernel(x_ref, a_ref, b_ref, wout_ref, postg_ref, fpre_ref, wup_ref, cw_ref, cb_ref,
                         wdn_ref, fpost_ref, o_ref, carry_ref):
    @pl.when(pl.program_id(0) == 0)
    def _():
        carry_ref[...] = jnp.zeros_like(carry_ref)

    na = a_ref.shape[1]
    mix = _dot(a_ref[...], wout_ref[0:na, :]) + _dot(b_ref[...], wout_ref[na:, :])
    x1 = x_ref[...] + _rms(mix, postg_ref[...])
    o_ref[...] = _ffn_tile(x1, fpre_ref[...], wup_ref, cw_ref, cb_ref, wdn_ref, fpost_ref[...], carry_ref)


def _out_ffn_odd_kernel(x_ref, o0_ref, l0_ref, o1_ref, l1_ref, o2_ref, l2_ref, bt_ref, wout_ref, postg_ref,
                        fpre_ref, wup_ref, cw_ref, cb_ref, wdn_ref, fpost_ref, o_ref, carry_ref):
    @pl.when(pl.program_id(0) == 0)
    def _():
        carry_ref[...] = jnp.zeros_like(carry_ref)

    l0, l1, l2 = l0_ref[...], l1_ref[...], l2_ref[...]
    m = jnp.maximum(jnp.maximum(l0, l1), l2)
    e0, e1, e2 = jnp.exp(l0 - m), jnp.exp(l1 - m), jnp.exp(l2 - m)
    dil = (e0 * o0_ref[...] + e1 * o1_ref[...] + e2 * o2_ref[...]) / (e0 + e1 + e2)
    na = dil.shape[1]
    sb = bt_ref[...].T
    mix = _dot(dil.astype(BF16), wout_ref[0:na, :]) + _dot(sb.astype(BF16), wout_ref[na:, :])
    x1 = x_ref[...] + _rms(mix, postg_ref[...])
    o_ref[...] = _ffn_tile(x1, fpre_ref[...], wup_ref, cw_ref, cb_ref, wdn_ref, fpost_ref[...], carry_ref)


def _out_ffn(kernel, name, x, acts, acts_t, consts):
    s = x.shape[0]
    tm = ROW_TILE
    row = lambda c: pl.BlockSpec((tm, c), lambda i: (i, 0))
    col = lambda r: pl.BlockSpec((r, tm), lambda i: (0, i))
    return pl.pallas_call(
        kernel,
        grid=(s // tm,),
        in_specs=([row(D_MODEL)] + [row(a.shape[1]) for a in acts] + [col(a.shape[0]) for a in acts_t]
                  + [_const_spec(c.shape) for c in consts]),
        out_specs=row(D_MODEL),
        out_shape=jax.ShapeDtypeStruct((s, D_MODEL), F32),
        scratch_shapes=[pltpu.VMEM((SUBLANES, 2 * D_FF), F32)],
        compiler_params=_params(("arbitrary",)),
        name=name,
    )(x, *acts, *acts_t, *consts)


def _proj_odd_kernel(x_ref, g_ref, win_ref, *out_refs):
    h = _rms(x_ref[...], g_ref[...]).astype(BF16)
    proj = _dot(h, win_ref[...])
    c = DIL_HEADS_PER_GROUP * HEAD_DIM
    scale = HEAD_DIM ** -0.5
    for n, ref in enumerate(out_refs[:-1]):
        blk = proj[:, n * c:(n + 1) * c]
        if n < 3 or n == 9:
            blk = blk * scale
        ref[...] = blk.astype(BF16)
    _store_transposed_tiles(out_refs[-1], proj[:, 11 * c:12 * c], SB_HEADS, HEAD_DIM, 0)


def _proj_odd(x, g, win):
    s = x.shape[0]
    tm, tk = ROW_TILE, ATT_TK
    c = DIL_HEADS_PER_GROUP * HEAD_DIM
    row = lambda w: pl.BlockSpec((tm, w), lambda i: (i, 0))
    return pl.pallas_call(
        _proj_odd_kernel,
        grid=(s // tm,),
        in_specs=[row(D_MODEL), _const_spec(g.shape), _const_spec(win.shape)],
        out_specs=[row(c)] * 11 + [pl.BlockSpec((tm // tk, c, tk), lambda i: (i, 0, 0))],
        out_shape=[jax.ShapeDtypeStruct((s, c), BF16)] * 11 + [jax.ShapeDtypeStruct((s // tk, c, tk), BF16)],
        compiler_params=_params(("parallel",)),
        name="proj_odd",
    )(x, g, win)


def _dilated_kernel(bias_ref, q_ref, kp_ref, kc_ref, vp_ref, vc_ref, o_ref, lse_ref, *, nub):
    t, tb = DIL_T, DIL_TB
    ub = pl.program_id(0) % nub
    c = DIL_HEADS_PER_GROUP * HEAD_DIM
    qf = q_ref[...].astype(F32)
    kcat = jnp.concatenate([kp_ref[...], kc_ref[...]], axis=0)
    vcat = jnp.concatenate([vp_ref[...], vc_ref[...]], axis=0)
    lane = lax.broadcasted_iota(jnp.int32, (tb, c), 1)
    col = lax.broadcasted_iota(jnp.int32, (tb, t + tb), 1)
    no_prev = col < jnp.where(ub == 0, t, 0)
    out = jnp.zeros((tb, c), F32)
    lse = jnp.zeros((tb, c), F32)
    for hh in range(DIL_HEADS_PER_GROUP):
        mine = jnp.logical_and(lane >= hh * HEAD_DIM, lane < (hh + 1) * HEAD_DIM)
        qh = jnp.where(mine, qf, 0.0).astype(BF16)
        s = _dot_t(qh, kcat) + bias_ref[hh]
        s = jnp.where(no_prev, NEG, s)
        m = jnp.max(s, axis=1, keepdims=True)
        e = jnp.exp(s - m)
        den = jnp.sum(e, axis=1, keepdims=True)
        oh = _dot(e.astype(BF16), vcat) / den
        out = jnp.where(mine, oh, out)
        lse = jnp.where(mine, m + jnp.log(den), lse)
    o_ref[...] = out
    lse_ref[...] = lse


def _dilated_group(q, k, v, dil, slopes):
    s, c = q.shape
    t, tb = DIL_T, DIL_TB
    nub = s // dil // tb
    a = np.arange(tb)[:, None]
    cc = np.arange(t + tb)[None, :]
    steps = t + a - cc
    ok = (steps >= 0) & (steps <= t)
    dist = (steps * dil).astype(np.float32)
    bias = np.where(ok[None], -np.asarray(slopes, np.float32)[:, None, None] * dist[None], np.float32(NEG))
    bias = jnp.asarray(bias, F32)
    view = lambda z: z.reshape(s // dil, dil * c)
    cur = pl.BlockSpec((tb, c), lambda b: (b % nub, b // nub))
    prev = pl.BlockSpec((t, c), lambda b: (jnp.maximum(b % nub * (tb // t) - 1, 0), b // nub))
    o, lse = pl.pallas_call(
        functools.partial(_dilated_kernel, nub=nub),
        grid=(dil * nub,),
        in_specs=[_const_spec(bias.shape), cur, prev, cur, prev, cur],
        out_specs=[cur, cur],
        out_shape=[jax.ShapeDtypeStruct((s // dil, dil * c), F32)] * 2,
        compiler_params=_params(("parallel",)),
        name=f"dilated_d{dil}",
    )(bias, view(q), view(k), view(k), view(v), view(v))
    return o.reshape(s, c), lse.reshape(s, c)


def _sb_kernel(tri_ref, q_ref, k_ref, vt_ref, o_ref):
    hh = pl.program_id(0)
    i = pl.program_id(1)
    tq, tk = SB_TQ, ATT_TK
    ratio = tq // tk
    c = SB_HEADS * HEAD_DIM
    tri = tri_ref[...]
    lane = lax.broadcasted_iota(jnp.int32, (tq, c), 1)
    mine = jnp.logical_and(lane >= hh * HEAD_DIM, lane < (hh + 1) * HEAD_DIM)
    qh = jnp.where(mine, q_ref[...].astype(F32), 0.0).astype(BF16)
    krow = lax.broadcasted_iota(jnp.int32, (tk, tq), 0)
    qcol = lax.broadcasted_iota(jnp.int32, (tk, tq), 1)

    def step(j, carry, diag):
        r, acc = carry
        z = _dot_t(_k_tile(k_ref, j), qh)
        sp = jnp.log1p(jnp.exp(-jnp.abs(z)))
        log_beta = jnp.minimum(z, 0.0) - sp
        log_keep = log_beta - z
        if diag is not None:
            strict = krow + diag * tk < qcol
            log_keep = jnp.where(strict, log_keep, 0.0)
        hi = log_keep.astype(BF16)
        lo = (log_keep - hi.astype(F32)).astype(BF16)
        aft = _dot(tri, jnp.concatenate([hi, lo], axis=0))
        a = jnp.exp(log_beta + aft[0:tk] + r)
        if diag is not None:
            a = jnp.where(strict, a, 0.0)
        acc = acc + _dot(vt_ref[j], a.astype(BF16))
        return r + aft[tk:tk + 1], acc

    carry = (jnp.zeros((1, tq), F32), jnp.zeros((HEAD_DIM, tq), F32))
    n_full = i * ratio
    carry = lax.fori_loop(0, ratio, lambda dd, cr: step(n_full + ratio - 1 - dd, cr, ratio - 1 - dd), carry)

    def more(state):
        n, r, _ = state
        return jnp.logical_and(n < n_full, jnp.max(r) > SB_STOP)

    def walk(state):
        n, r, acc = state
        r, acc = step(n_full - 1 - n, (r, acc), None)
        return n + 1, r, acc

    o_ref[...] = lax.while_loop(more, walk, (jnp.int32(0),) + carry)[2]


def _stickbreak(q, k, vt):
    s, c = q.shape
    tq, tk = SB_TQ, ATT_TK
    j = np.arange(tk)
    later = (j[None, :] > j[:, None]).astype(np.float32)
    tri = np.concatenate([np.concatenate([later, later], axis=1), np.ones((BF16_ROWS, 2 * tk), np.float32)], axis=0)
    tri = jnp.asarray(tri, BF16)
    return pl.pallas_call(
        _sb_kernel,
        grid=(SB_HEADS, s // tq),
        in_specs=[_const_spec(tri.shape),
                  pl.BlockSpec((tq, c), lambda h, i: (i, 0)),
                  _const_spec((s, c)),
                  pl.BlockSpec((s // tk, HEAD_DIM, tk), lambda h, i: (0, h, 0))],
        out_specs=pl.BlockSpec((HEAD_DIM, tq), lambda h, i: (h, i)),
        out_shape=jax.ShapeDtypeStruct((c, s), F32),
        compiler_params=_params(("parallel", "parallel")),
        name="stickbreak",
    )(tri, q, k, vt)


def _alibi_slopes(n):
    return 2.0 ** (-8.0 * np.arange(1, n + 1) / n)


def _pad_cols(w, lo, width):
    return jnp.pad(w, ((0, 0), (lo, width - lo - w.shape[1])))


def _rot_half_cols(w):
    half = w.shape[1] // 2
    return jnp.concatenate([-w[:, half:], w[:, :half]], axis=1)


def _even_weights(w_in, w_uq, w_ukv):
    o_kr = MLA_Q_RANK + MLA_KV_RANK
    w_kr = w_in[:, o_kr:o_kr + MLA_ROPE]
    win = jnp.concatenate([w_in[:, :o_kr], _pad_cols(w_kr, MLA_NOPE, LANES),
                           _pad_cols(_rot_half_cols(w_kr), MLA_NOPE, LANES),
                           w_in[:, o_kr + MLA_ROPE:]], axis=1)
    hd = MLA_NOPE + MLA_ROPE
    q_plain, q_rot = [], []
    for hh in range(MLA_HEADS):
        wh = w_uq[:, hh * hd:(hh + 1) * hd]
        q_plain.append(_pad_cols(wh, 0, LANES))
        q_rot.append(_pad_cols(_rot_half_cols(wh[:, MLA_NOPE:]), MLA_NOPE, LANES))
    wuq = jnp.concatenate(q_plain + q_rot, axis=1)
    hk = MLA_NOPE + MLA_V
    k_cols = [_pad_cols(w_ukv[:, hh * hk:hh * hk + MLA_NOPE], 0, LANES) for hh in range(MLA_HEADS)]
    v_cols = [w_ukv[:, hh * hk + MLA_NOPE:(hh + 1) * hk] for hh in range(MLA_HEADS)]
    wkv = jnp.concatenate(k_cols + v_cols, axis=1)
    return win.astype(BF16), wuq.astype(BF16), wkv.astype(BF16)


def _rope_tables(s):
    half = MLA_ROPE // 2
    inv = ROPE_BASE ** (-jnp.arange(half, dtype=F32) / half)
    ang = jnp.arange(s).astype(F32)[:, None] * inv
    cos2 = jnp.tile(jnp.cos(ang), (1, 2))
    sin2 = jnp.tile(jnp.sin(ang), (1, 2))
    scale = (MLA_NOPE + MLA_ROPE) ** -0.5 * LOG2E
    tail = LANES - MLA_NOPE - MLA_ROPE
    ones, zeros, ztail = jnp.ones((s, MLA_NOPE), F32), jnp.zeros((s, MLA_NOPE), F32), jnp.zeros((s, tail), F32)
    cosq = jnp.concatenate([ones, cos2, ztail], axis=1) * scale
    sinq = jnp.concatenate([zeros, sin2, ztail], axis=1) * scale
    cosk = jnp.concatenate([zeros, cos2, ztail], axis=1)
    sink = jnp.concatenate([zeros, sin2, ztail], axis=1)
    return cosq, sinq, cosk, sink


def _row2d(v):
    return v.reshape(1, -1).astype(F32)


def kernel(x, ev_pre_g, ev_w_in, ev_cq_g, ev_w_uq, ev_ckv_g, ev_w_ukv, ev_lam_q1, ev_lam_k1, ev_lam_q2,
           ev_lam_k2, ev_subln_g, ev_w_out, ev_post_g, od_pre_g, od_w_in, od_w_out, od_post_g, ffn_pre_g,
           ffn_w_up, ffn_conv_w, ffn_conv_b, ffn_w_down, ffn_post_g):
    b, s, _ = x.shape
    assert b == 1 and s % max(ROW_TILE, ATT_TQ, SB_TQ, DIL_TB * DIL_CONFIGS[-1][1]) == 0
    assert ROW_TILE % ATT_TK == 0 and ATT_TQ % ATT_TK == 0 and SB_TQ % ATT_TK == 0
    assert (ATT_TQ // ATT_TK) % MAIN_UNROLL == 0 and DIL_TB % DIL_T == 0
    assert ATT_TK <= 256
    xs = x[0]

    def ffn_consts(i):
        return [_row2d(ffn_pre_g[i]), ffn_w_up[i].astype(BF16), ffn_conv_w[i].astype(F32),
                _row2d(ffn_conv_b[i]), ffn_w_down[i].astype(BF16), _row2d(ffn_post_g[i])]

    win, wuq, wkv = _even_weights(ev_w_in[0], ev_w_uq[0], ev_w_ukv[0])
    cosq, sinq, cosk, sink = _rope_tables(s)
    pos_col = (jnp.arange(ROW_TILE) % ATT_TK).astype(BF16)[:, None]
    pos_tile = jnp.pad(jnp.tile(pos_col, (1, SLOPE_PARTS)), ((0, 0), (0, LANES - SLOPE_PARTS)))
    qm, km, vmt, dq, dk, dvt = _proj_even(xs, _row2d(ev_pre_g[0]), win, _row2d(ev_cq_g[0]), wuq,
                                          _row2d(ev_ckv_g[0]), wkv, cosq, sinq, cosk, sink, pos_tile)
    o_mla = _mla_attention(qm, km, vmt)
    lam_p = jnp.pad(jnp.stack([ev_lam_q1[0], ev_lam_k1[0], ev_lam_q2[0], ev_lam_k2[0]]).astype(F32),
                    ((0, SUBLANES - 4), (0, LANES - DIFF_QK)))
    d_slopes = _alibi_slopes(DIFF_HEADS) * LOG2E
    slope_arr = jnp.asarray(np.broadcast_to(d_slopes[:, None, None], (DIFF_HEADS, SUBLANES, LANES)), F32)
    qfeat = jnp.zeros((DIFF_HEADS, ATT_TQ, LANES), F32)
    rest = jnp.asarray(d_slopes, F32)
    for part in range(SLOPE_PARTS):
        piece = rest.astype(BF16).astype(F32)
        qfeat = qfeat.at[:, :, part].set(piece[:, None])
        rest = rest - piece
    lam_init = 0.8 - 0.6 * math.exp(-0.3 * 0)
    o_diff = _diff_attention(lam_p, slope_arr, qfeat.astype(BF16), _row2d(ev_subln_g[0]),
                             dq, dk, dvt, lam_init)
    x2 = _out_ffn(_out_ffn_even_kernel, "out_ffn_even", xs, [o_mla, o_diff], [],
                  [ev_w_out[0].astype(BF16), _row2d(ev_post_g[0])] + ffn_consts(0))

    p = _proj_odd(x2, _row2d(od_pre_g[0]), od_w_in[0].astype(BF16))
    slopes = _alibi_slopes(len(DIL_CONFIGS) * DIL_HEADS_PER_GROUP).reshape(len(DIL_CONFIGS), -1)
    dil_outs = []
    for gi, (_, dil) in enumerate(DIL_CONFIGS):
        dil_outs += list(_dilated_group(p[gi], p[3 + gi], p[6 + gi], dil, slopes[gi]))
    o_sbt = _stickbreak(p[9], p[10], p[11])
    x4 = _out_ffn(_out_ffn_odd_kernel, "out_ffn_odd", x2, dil_outs, [o_sbt],
                  [od_w_out[0].astype(BF16), _row2d(od_post_g[0])] + ffn_consts(1))
    return x4[None]
```

```python
import functools
import math

import numpy as np
import jax
import jax.numpy as jnp
from jax import lax
from jax.experimental import pallas as pl
from jax.experimental.pallas import tpu as pltpu

F32 = jnp.float32
BF16 = jnp.bfloat16

D_MODEL = 1024
NORM_EPS = 1e-6
MLA_HEADS = 8
MLA_Q_RANK = 256
MLA_KV_RANK = 128
MLA_NOPE = 64
MLA_ROPE = 32
MLA_V = 64
ROPE_BASE = 10000.0
DIFF_HEADS = 4
DIFF_QK = 64
DIFF_V = 128
DIL_CONFIGS = ((128, 1), (512, 4), (2048, 16))
DIL_HEADS_PER_GROUP = 4
HEAD_DIM = 64
SB_HEADS = 4
D_FF = 2816
CONV_WIDTH = 3

LANES = 128
SUBLANES = 8
BF16_ROWS = 16
VMEM_LIMIT_BYTES = 56 * 1024 * 1024
ROW_TILE = 512
ATT_TQ = 2048
MAIN_UNROLL = 4
SB_TQ = 512
ATT_TK = 256
DIL_T = 128
DIL_TB = 512
FF_CHUNKS = (512,) * 5 + (256,)
NEG = -1e30
LOG2E = 1.4426950408889634
MAX_JUMP = 32.0
SB_STOP = -110.0
SLOPE_PARTS = 3

_TRANS_B = (((1,), (1,)), ((), ()))


def _dot(a, b):
    return jnp.dot(a, b, preferred_element_type=F32)


def _dot_t(a, b):
    return lax.dot_general(a, b, _TRANS_B, preferred_element_type=F32)


def _rms(xf, g):
    ms = jnp.mean(xf * xf, axis=-1, keepdims=True)
    return xf * lax.rsqrt(ms + NORM_EPS) * g


def _params(sem):
    return pltpu.CompilerParams(dimension_semantics=sem, vmem_limit_bytes=VMEM_LIMIT_BYTES)


def _const_spec(shape):
    nd = len(shape)
    return pl.BlockSpec(shape, lambda *_: (0,) * nd, pipeline_mode=pl.Buffered(1))


def _store_transposed_tiles(dst_ref, src, heads, width, ones_rows):
    tk = ATT_TK
    per = width + ones_rows
    for b in range(src.shape[0] // tk):
        vt = src[b * tk:(b + 1) * tk, :].T.astype(BF16)
        for hh in range(heads):
            dst_ref[b, hh * per:hh * per + width, :] = vt[hh * width:(hh + 1) * width, :]
            if ones_rows:
                dst_ref[b, hh * per + width:(hh + 1) * per, :] = jnp.ones((ones_rows, tk), BF16)


def _proj_even_kernel(x_ref, g_ref, win_ref, cqg_ref, wuq_ref, ckvg_ref, wkv_ref,
                      cosq_ref, sinq_ref, cosk_ref, sink_ref, pos_ref,
                      qm_ref, km_ref, vmt_ref, dq_ref, dk_ref, dvt_ref):
    h = _rms(x_ref[...], g_ref[...]).astype(BF16)
    proj = _dot(h, win_ref[...])
    nq = MLA_HEADS * LANES
    cq = _rms(proj[:, 0:MLA_Q_RANK], cqg_ref[...]).astype(BF16)
    qq = _dot(cq, wuq_ref[...])
    cosq, sinq = cosq_ref[...], sinq_ref[...]
    for hh in range(MLA_HEADS):
        a = qq[:, hh * LANES:(hh + 1) * LANES]
        b = qq[:, nq + hh * LANES:nq + (hh + 1) * LANES]
        qm_ref[:, hh * LANES:(hh + 1) * LANES] = (a * cosq + b * sinq).astype(BF16)
    o = MLA_Q_RANK
    ckv = _rms(proj[:, o:o + MLA_KV_RANK], ckvg_ref[...]).astype(BF16)
    kv = _dot(ckv, wkv_ref[...])
    o += MLA_KV_RANK
    krc = proj[:, o:o + LANES] * cosk_ref[...] + proj[:, o + LANES:o + 2 * LANES] * sink_ref[...]
    for hh in range(MLA_HEADS):
        km_ref[:, hh * LANES:(hh + 1) * LANES] = (kv[:, hh * LANES:(hh + 1) * LANES] + krc).astype(BF16)
    _store_transposed_tiles(vmt_ref, kv[:, nq:nq + MLA_HEADS * MLA_V], MLA_HEADS, MLA_V, BF16_ROWS)
    o += 2 * LANES
    nd = DIFF_HEADS * 2 * DIFF_QK
    dq_ref[...] = (proj[:, o:o + nd] * (DIFF_QK ** -0.5 * LOG2E)).astype(BF16)
    o += nd
    pos = pos_ref[...]
    for hh in range(DIFF_HEADS):
        dk_ref[:, hh * 2 * LANES:hh * 2 * LANES + LANES] = proj[:, o + hh * LANES:o + (hh + 1) * LANES].astype(BF16)
        dk_ref[:, hh * 2 * LANES + LANES:(hh + 1) * 2 * LANES] = pos
    o += nd
    _store_transposed_tiles(dvt_ref, proj[:, o:o + DIFF_HEADS * DIFF_V], DIFF_HEADS, DIFF_V, BF16_ROWS)


def _proj_even(x, g, win, cqg, wuq, ckvg, wkv, cosq, sinq, cosk, sink, pos):
    s = x.shape[0]
    tm, tk = ROW_TILE, ATT_TK
    row = lambda c: pl.BlockSpec((tm, c), lambda i: (i, 0))
    tile_t = lambda r: pl.BlockSpec((tm // tk, r, tk), lambda i: (i, 0, 0))
    mla_rows = MLA_HEADS * (MLA_V + BF16_ROWS)
    diff_rows = DIFF_HEADS * (DIFF_V + BF16_ROWS)
    flat = lambda c: jax.ShapeDtypeStruct((s, c), BF16)
    tiled = lambda r: jax.ShapeDtypeStruct((s // tk, r, tk), BF16)
    return pl.pallas_call(
        _proj_even_kernel,
        grid=(s // tm,),
        in_specs=[row(D_MODEL), _const_spec(g.shape), _const_spec(win.shape), _const_spec(cqg.shape),
                  _const_spec(wuq.shape), _const_spec(ckvg.shape), _const_spec(wkv.shape),
                  row(LANES), row(LANES), row(LANES), row(LANES), _const_spec(pos.shape)],
        out_specs=[row(MLA_HEADS * LANES), row(MLA_HEADS * LANES), tile_t(mla_rows),
                   row(DIFF_HEADS * 2 * DIFF_QK), row(DIFF_HEADS * 2 * LANES), tile_t(diff_rows)],
        out_shape=[flat(MLA_HEADS * LANES), flat(MLA_HEADS * LANES), tiled(mla_rows),
                   flat(DIFF_HEADS * 2 * DIFF_QK), flat(DIFF_HEADS * 2 * LANES), tiled(diff_rows)],
        compiler_params=_params(("parallel",)),
        name="proj_even",
    )(x, g, win, cqg, wuq, ckvg, wkv, cosq, sinq, cosk, sink, pos)


def _flash_cols(chains, i, rows, tile_bias=None, scratch=None):
    tq, tk = ATT_TQ, ATT_TK
    ratio = tq // tk
    krow = lax.broadcasted_iota(jnp.int32, (tk, tq), 0)
    qcol = lax.broadcasted_iota(jnp.int32, (tk, tq), 1)
    n_full = i * ratio

    def safe_step(j, carry, diag):
        out = []
        for (qh, load_k, load_vt), (m, acc) in zip(chains, carry):
            s = _dot_t(load_k(j), qh)
            if diag is not None:
                s = jnp.where(krow + diag * tk <= qcol, s, NEG)
            mb = jnp.max(s, axis=0, keepdims=True)
            if tile_bias is not None:
                c = tile_bias(j)
                mb = mb + c
            m_new = jnp.maximum(m, mb)
            alpha = jnp.exp2(m - m_new)
            shift = m_new if tile_bias is None else m_new - c
            p = jnp.exp2(s - shift).astype(BF16)
            out.append((m_new, alpha * acc + _dot(load_vt(j), p)))
        return tuple(out)

    def fast_step(n, carry):
        j = n_full - 1 - n
        out = []
        for ci, ((qh, load_k, load_vt), (m, alpha, acc, jump)) in enumerate(zip(chains, carry)):
            acc = (acc + _dot(load_vt(j + 1), scratch[ci])) * alpha
            s = _dot_t(load_k(j), qh)
            c = None if tile_bias is None else tile_bias(j)
            shift = m if c is None else m - c
            scratch[ci] = jnp.exp2(s - shift).astype(BF16)
            mb = jnp.max(s, axis=0, keepdims=True)
            if c is not None:
                mb = mb + c
            m_new = jnp.maximum(m, mb)
            out.append((m_new, jnp.exp2(m - m_new), acc, jnp.maximum(jump, mb - m)))
        return tuple(out)

    def fast_group(g, c):
        for t in range(MAIN_UNROLL):
            c = fast_step(g * MAIN_UNROLL + t, c)
        return c

    init = tuple((jnp.full((1, tq), NEG, F32), jnp.zeros((rows, tq), F32)) for _ in chains)
    diag = lax.fori_loop(0, ratio, lambda d, c: safe_step(n_full + d, c, d), init)
    for ci in range(len(chains)):
        scratch[ci] = jnp.zeros((tk, tq), BF16)
    one, low = jnp.ones((1, tq), F32), jnp.full((1, tq), NEG, F32)
    fast = lax.fori_loop(0, n_full // MAIN_UNROLL, fast_group, tuple((m, one, acc, low) for m, acc in diag))
    accs = [(acc + _dot(load_vt(0), scratch[ci])) * alpha
            for ci, ((_, _, load_vt), (_, alpha, acc, _)) in enumerate(zip(chains, fast))]
    worst = functools.reduce(jnp.maximum, [jnp.max(jump) for _, _, _, jump in fast])

    def redo():
        carry = lax.fori_loop(0, n_full, lambda j, c: safe_step(j, c, None), diag)
        return [acc for _, acc in carry]

    return lax.cond(worst > MAX_JUMP, redo, lambda: accs)


def _k_tile(k_ref, j, lanes=slice(None)):
    return k_ref[pl.ds(pl.multiple_of(j * ATT_TK, ATT_TK), ATT_TK), lanes]


def _mla_kernel(q_ref, k_ref, vt_ref, o_ref, p_ref):
    i = pl.program_id(1)
    rows = MLA_V + BF16_ROWS
    def chain(hh):
        lanes = slice(hh * LANES, (hh + 1) * LANES)
        return (q_ref[:, lanes], lambda j: _k_tile(k_ref, j, lanes),
                lambda j: vt_ref[j, hh * rows:(hh + 1) * rows, :])

    accs = _flash_cols([chain(0), chain(1)], i, rows, scratch=p_ref)
    halves = [acc[0:MLA_V] / acc[MLA_V:MLA_V + 1] for acc in accs]
    o_ref[...] = jnp.concatenate(halves, axis=0).T.astype(BF16)


def _mla_attention(qm, km, vmt):
    s = qm.shape[0]
    tq = ATT_TQ
    rows = 2 * (MLA_V + BF16_ROWS)
    return pl.pallas_call(
        _mla_kernel,
        grid=(MLA_HEADS // 2, s // tq),
        in_specs=[pl.BlockSpec((tq, 2 * LANES), lambda p, i: (i, p)),
                  pl.BlockSpec((s, 2 * LANES), lambda p, i: (0, p)),
                  pl.BlockSpec((vmt.shape[0], rows, ATT_TK), lambda p, i: (0, p, 0))],
        out_specs=pl.BlockSpec((tq, LANES), lambda p, i: (i, p)),
        out_shape=jax.ShapeDtypeStruct((s, MLA_HEADS * MLA_V), BF16),
        scratch_shapes=[pltpu.VMEM((2, ATT_TK, tq), BF16)],
        compiler_params=_params(("parallel", "parallel")),
        name="mla_attn",
    )(qm, km, vmt)


def _diff_kernel(lam_ref, slope_ref, qf_ref, g_ref, q_ref, k_ref, vt_ref, o_ref, p_ref, *, lam_init):
    i = pl.program_id(1)
    tq = ATT_TQ
    lane = lax.broadcasted_iota(jnp.int32, (tq, LANES), 1)
    qf32 = q_ref[...].astype(F32)
    feat = qf_ref[...]
    zero = jnp.zeros_like(qf32)
    qa = jnp.concatenate([jnp.where(lane < DIFF_QK, qf32, zero).astype(BF16), feat], axis=1)
    qb = jnp.concatenate([jnp.where(lane >= DIFF_QK, qf32, zero).astype(BF16), feat], axis=1)
    slope = slope_ref[0:1, 0:1]
    rows = DIFF_V + BF16_ROWS
    tile_bias = lambda j: slope * (j * ATT_TK - i * tq).astype(F32)
    load_k = lambda j: _k_tile(k_ref, j)
    load_vt = lambda j: vt_ref[j]
    acc1, acc2 = _flash_cols([(qa, load_k, load_vt), (qb, load_k, load_vt)], i, rows, tile_bias, p_ref)
    lp = lam_ref[...]
    s1 = jnp.sum(lp[0:1, :] * lp[1:2, :], axis=1, keepdims=True)
    s2 = jnp.sum(lp[2:3, :] * lp[3:4, :], axis=1, keepdims=True)
    lam = jnp.exp(s1) - jnp.exp(s2) + lam_init
    ot = acc1[0:DIFF_V] / acc1[DIFF_V:DIFF_V + 1] - lam * (acc2[0:DIFF_V] / acc2[DIFF_V:DIFF_V + 1])
    o_ref[...] = (_rms(ot.T, g_ref[...]) * (1.0 - lam_init)).astype(BF16)


def _diff_attention(lam_p, slope_arr, qfeat, subln_g, dq, dk, dvt, lam_init):
    s = dq.shape[0]
    tq = ATT_TQ
    return pl.pallas_call(
        functools.partial(_diff_kernel, lam_init=lam_init),
        grid=(DIFF_HEADS, s // tq),
        in_specs=[pl.BlockSpec(lam_p.shape, lambda h, i: (0, 0)),
                  pl.BlockSpec((None, SUBLANES, LANES), lambda h, i: (h, 0, 0)),
                  pl.BlockSpec((None, tq, LANES), lambda h, i: (h, 0, 0)),
                  pl.BlockSpec(subln_g.shape, lambda h, i: (0, 0)),
                  pl.BlockSpec((tq, LANES), lambda h, i: (i, h)),
                  pl.BlockSpec((s, 2 * LANES), lambda h, i: (0, h)),
                  pl.BlockSpec((dvt.shape[0], DIFF_V + BF16_ROWS, ATT_TK), lambda h, i: (0, h, 0))],
        out_specs=pl.BlockSpec((tq, LANES), lambda h, i: (i, h)),
        out_shape=jax.ShapeDtypeStruct((s, DIFF_HEADS * DIFF_V), BF16),
        scratch_shapes=[pltpu.VMEM((2, ATT_TK, tq), BF16)],
        compiler_params=_params(("parallel", "parallel")),
        name="diff_attn",
    )(lam_p, slope_arr, qfeat, subln_g, dq, dk, dvt)


def _shift_rows(u, k, prev):
    top = jnp.where(lax.broadcasted_iota(jnp.int32, prev.shape, 0) < k,
                    pltpu.roll(prev, k, 0), pltpu.roll(u[0:SUBLANES, :], k, 0))
    return jnp.concatenate([top, pltpu.roll(u, k, 0)[SUBLANES:, :]], axis=0)


def _ffn_tile(x1, fpre, wup_ref, cw_ref, cb_ref, wdn_ref, fpost, carry_ref):
    tm = x1.shape[0]
    hf = _rms(x1, fpre).astype(BF16)
    acc = jnp.zeros((tm, D_MODEL), F32)
    lo = 0
    for width in FF_CHUNKS:
        ys = []
        for part in range(2):
            c0 = part * D_FF + lo
            u = _dot(hf, wup_ref[:, c0:c0 + width])
            prev = carry_ref[:, c0:c0 + width]
            carry_ref[:, c0:c0 + width] = u[tm - SUBLANES:, :]
            ys.append(cw_ref[2:3, c0:c0 + width] * u
                      + cw_ref[1:2, c0:c0 + width] * _shift_rows(u, 1, prev)
                      + cw_ref[0:1, c0:c0 + width] * _shift_rows(u, 2, prev)
                      + cb_ref[:, c0:c0 + width])
        gate, up = ys
        act = (gate * (1.0 / (1.0 + jnp.exp(-gate))) * up).astype(BF16)
        acc = acc + _dot(act, wdn_ref[lo:lo + width, :])
        lo += width
    return x1 + _rms(acc, fpost)


def _out_ffn_even_kernel(x_ref, a_ref, b_ref, wout_ref, postg_ref, fpre_ref, wup_ref, cw_ref, cb_ref,
                         wdn_ref, fpost_ref, o_ref, carry_ref):
    @pl.when(pl.program_id(0) == 0)
    def _():
        carry_ref[...] = jnp.zeros_like(carry_ref)

    na = a_ref.shape[1]
    mix = _dot(a_ref[...], wout_ref[0:na, :]) + _dot(b_ref[...], wout_ref[na:, :])
    x1 = x_ref[...] + _rms(mix, postg_ref[...])
    o_ref[...] = _ffn_tile(x1, fpre_ref[...], wup_ref, cw_ref, cb_ref, wdn_ref, fpost_ref[...], carry_ref)


def _out_ffn_odd_kernel(x_ref, o0_ref, l0_ref, o1_ref, l1_ref, o2_ref, l2_ref, bt_ref, wout_ref, postg_ref,
                        fpre_ref, wup_ref, cw_ref, cb_ref, wdn_ref, fpost_ref, o_ref, carry_ref):
    @pl.when(pl.program_id(0) == 0)
    def _():
        carry_ref[...] = jnp.zeros_like(carry_ref)

    l0, l1, l2 = l0_ref[...], l1_ref[...], l2_ref[...]
    m = jnp.maximum(jnp.maximum(l0, l1), l2)
    e0, e1, e2 = jnp.exp(l0 - m), jnp.exp(l1 - m), jnp.exp(l2 - m)
    dil = (e0 * o0_ref[...] + e1 * o1_ref[...] + e2 * o2_ref[...]) / (e0 + e1 + e2)
    na = dil.shape[1]
    sb = bt_ref[...].T
    mix = _dot(dil.astype(BF16), wout_ref[0:na, :]) + _dot(sb.astype(BF16), wout_ref[na:, :])
    x1 = x_ref[...] + _rms(mix, postg_ref[...])
    o_ref[...] = _ffn_tile(x1, fpre_ref[...], wup_ref, cw_ref, cb_ref, wdn_ref, fpost_ref[...], carry_ref)


def _out_ffn(kernel, name, x, acts, acts_t, consts):
    s = x.shape[0]
    tm = ROW_TILE
    row = lambda c: pl.BlockSpec((tm, c), lambda i: (i, 0))
    col = lambda r: pl.BlockSpec((r, tm), lambda i: (0, i))
    return pl.pallas_call(
        kernel,
        grid=(s // tm,),
        in_specs=([row(D_MODEL)] + [row(a.shape[1]) for a in acts] + [col(a.shape[0]) for a in acts_t]
                  + [_const_spec(c.shape) for c in consts]),
        out_specs=row(D_MODEL),
        out_shape=jax.ShapeDtypeStruct((s, D_MODEL), F32),
        scratch_shapes=[pltpu.VMEM((SUBLANES, 2 * D_FF), F32)],
        compiler_params=_params(("arbitrary",)),
        name=name,
    )(x, *acts, *acts_t, *consts)


def _proj_odd_kernel(x_ref, g_ref, win_ref, *out_refs):
    h = _rms(x_ref[...], g_ref[...]).astype(BF16)
    proj = _dot(h, win_ref[...])
    c = DIL_HEADS_PER_GROUP * HEAD_DIM
    scale = HEAD_DIM ** -0.5
    for n, ref in enumerate(out_refs[:-1]):
        blk = proj[:, n * c:(n + 1) * c]
        if n < 3 or n == 9:
            blk = blk * scale
        ref[...] = blk.astype(BF16)
    _store_transposed_tiles(out_refs[-1], proj[:, 11 * c:12 * c], SB_HEADS, HEAD_DIM, 0)


def _proj_odd(x, g, win):
    s = x.shape[0]
    tm, tk = ROW_TILE, ATT_TK
    c = DIL_HEADS_PER_GROUP * HEAD_DIM
    row = lambda w: pl.BlockSpec((tm, w), lambda i: (i, 0))
    return pl.pallas_call(
        _proj_odd_kernel,
        grid=(s // tm,),
        in_specs=[row(D_MODEL), _const_spec(g.shape), _const_spec(win.shape)],
        out_specs=[row(c)] * 11 + [pl.BlockSpec((tm // tk, c, tk), lambda i: (i, 0, 0))],
        out_shape=[jax.ShapeDtypeStruct((s, c), BF16)] * 11 + [jax.ShapeDtypeStruct((s // tk, c, tk), BF16)],
        compiler_params=_params(("parallel",)),
        name="proj_odd",
    )(x, g, win)


def _dilated_kernel(bias_ref, q_ref, kp_ref, kc_ref, vp_ref, vc_ref, o_ref, lse_ref, *, nub):
    t, tb = DIL_T, DIL_TB
    ub = pl.program_id(0) % nub
    c = DIL_HEADS_PER_GROUP * HEAD_DIM
    qf = q_ref[...].astype(F32)
    kcat = jnp.concatenate([kp_ref[...], kc_ref[...]], axis=0)
    vcat = jnp.concatenate([vp_ref[...], vc_ref[...]], axis=0)
    lane = lax.broadcasted_iota(jnp.int32, (tb, c), 1)
    col = lax.broadcasted_iota(jnp.int32, (tb, t + tb), 1)
    no_prev = col < jnp.where(ub == 0, t, 0)
    out = jnp.zeros((tb, c), F32)
    lse = jnp.zeros((tb, c), F32)
    for hh in range(DIL_HEADS_PER_GROUP):
        mine = jnp.logical_and(lane >= hh * HEAD_DIM, lane < (hh + 1) * HEAD_DIM)
        qh = jnp.where(mine, qf, 0.0).astype(BF16)
        s = _dot_t(qh, kcat) + bias_ref[hh]
        s = jnp.where(no_prev, NEG, s)
        m = jnp.max(s, axis=1, keepdims=True)
        e = jnp.exp(s - m)
        den = jnp.sum(e, axis=1, keepdims=True)
        oh = _dot(e.astype(BF16), vcat) / den
        out = jnp.where(mine, oh, out)
        lse = jnp.where(mine, m + jnp.log(den), lse)
    o_ref[...] = out
    lse_ref[...] = lse


def _dilated_group(q, k, v, dil, slopes):
    s, c = q.shape
    t, tb = DIL_T, DIL_TB
    nub = s // dil // tb
    a = np.arange(tb)[:, None]
    cc = np.arange(t + tb)[None, :]
    steps = t + a - cc
    ok = (steps >= 0) & (steps <= t)
    dist = (steps * dil).astype(np.float32)
    bias = np.where(ok[None], -np.asarray(slopes, np.float32)[:, None, None] * dist[None], np.float32(NEG))
    bias = jnp.asarray(bias, F32)
    view = lambda z: z.reshape(s // dil, dil * c)
    cur = pl.BlockSpec((tb, c), lambda b: (b % nub, b // nub))
    prev = pl.BlockSpec((t, c), lambda b: (jnp.maximum(b % nub * (tb // t) - 1, 0), b // nub))
    o, lse = pl.pallas_call(
        functools.partial(_dilated_kernel, nub=nub),
        grid=(dil * nub,),
        in_specs=[_const_spec(bias.shape), cur, prev, cur, prev, cur],
        out_specs=[cur, cur],
        out_shape=[jax.ShapeDtypeStruct((s // dil, dil * c), F32)] * 2,
        compiler_params=_params(("parallel",)),
        name=f"dilated_d{dil}",
    )(bias, view(q), view(k), view(k), view(v), view(v))
    return o.reshape(s, c), lse.reshape(s, c)


def _sb_kernel(tri_ref, q_ref, k_ref, vt_ref, o_ref):
    hh = pl.program_id(0)
    i = pl.program_id(1)
    tq, tk = SB_TQ, ATT_TK
    ratio = tq // tk
    c = SB_HEADS * HEAD_DIM
    tri = tri_ref[...]
    lane = lax.broadcasted_iota(jnp.int32, (tq, c), 1)
    mine = jnp.logical_and(lane >= hh * HEAD_DIM, lane < (hh + 1) * HEAD_DIM)
    qh = jnp.where(mine, q_ref[...].astype(F32), 0.0).astype(BF16)
    krow = lax.broadcasted_iota(jnp.int32, (tk, tq), 0)
    qcol = lax.broadcasted_iota(jnp.int32, (tk, tq), 1)

    def step(j, carry, diag):
        r, acc = carry
        z = _dot_t(_k_tile(k_ref, j), qh)
        sp = jnp.log1p(jnp.exp(-jnp.abs(z)))
        log_beta = jnp.minimum(z, 0.0) - sp
        log_keep = log_beta - z
        if diag is not None:
            strict = krow + diag * tk < qcol
            log_keep = jnp.where(strict, log_keep, 0.0)
        hi = log_keep.astype(BF16)
        lo = (log_keep - hi.astype(F32)).astype(BF16)
        aft = _dot(tri, jnp.concatenate([hi, lo], axis=0))
        a = jnp.exp(log_beta + aft[0:tk] + r)
        if diag is not None:
            a = jnp.where(strict, a, 0.0)
        acc = acc + _dot(vt_ref[j], a.astype(BF16))
        return r + aft[tk:tk + 1], acc

    carry = (jnp.zeros((1, tq), F32), jnp.zeros((HEAD_DIM, tq), F32))
    n_full = i * ratio
    carry = lax.fori_loop(0, ratio, lambda dd, cr: step(n_full + ratio - 1 - dd, cr, ratio - 1 - dd), carry)

    def more(state):
        n, r, _ = state
        return jnp.logical_and(n < n_full, jnp.max(r) > SB_STOP)

    def walk(state):
        n, r, acc = state
        r, acc = step(n_full - 1 - n, (r, acc), None)
        return n + 1, r, acc

    o_ref[...] = lax.while_loop(more, walk, (jnp.int32(0),) + carry)[2]


def _stickbreak(q, k, vt):
    s, c = q.shape
    tq, tk = SB_TQ, ATT_TK
    j = np.arange(tk)
    later = (j[None, :] > j[:, None]).astype(np.float32)
    tri = np.concatenate([np.concatenate([later, later], axis=1), np.ones((BF16_ROWS, 2 * tk), np.float32)], axis=0)
    tri = jnp.asarray(tri, BF16)
    return pl.pallas_call(
        _sb_kernel,
        grid=(SB_HEADS, s // tq),
        in_specs=[_const_spec(tri.shape),
                  pl.BlockSpec((tq, c), lambda h, i: (i, 0)),
                  _const_spec((s, c)),
                  pl.BlockSpec((s // tk, HEAD_DIM, tk), lambda h, i: (0, h, 0))],
        out_specs=pl.BlockSpec((HEAD_DIM, tq), lambda h, i: (h, i)),
        out_shape=jax.ShapeDtypeStruct((c, s), F32),
        compiler_params=_params(("parallel", "parallel")),
        name="stickbreak",
    )(tri, q, k, vt)


def _alibi_slopes(n):
    return 2.0 ** (-8.0 * np.arange(1, n + 1) / n)


def _pad_cols(w, lo, width):
    return jnp.pad(w, ((0, 0), (lo, width - lo - w.shape[1])))


def _rot_half_cols(w):
    half = w.shape[1] // 2
    return jnp.concatenate([-w[:, half:], w[:, :half]], axis=1)


def _even_weights(w_in, w_uq, w_ukv):
    o_kr = MLA_Q_RANK + MLA_KV_RANK
    w_kr = w_in[:, o_kr:o_kr + MLA_ROPE]
    win = jnp.concatenate([w_in[:, :o_kr], _pad_cols(w_kr, MLA_NOPE, LANES),
                           _pad_cols(_rot_half_cols(w_kr), MLA_NOPE, LANES),
                           w_in[:, o_kr + MLA_ROPE:]], axis=1)
    hd = MLA_NOPE + MLA_ROPE
    q_plain, q_rot = [], []
    for hh in range(MLA_HEADS):
        wh = w_uq[:, hh * hd:(hh + 1) * hd]
        q_plain.append(_pad_cols(wh, 0, LANES))
        q_rot.append(_pad_cols(_rot_half_cols(wh[:, MLA_NOPE:]), MLA_NOPE, LANES))
    wuq = jnp.concatenate(q_plain + q_rot, axis=1)
    hk = MLA_NOPE + MLA_V
    k_cols = [_pad_cols(w_ukv[:, hh * hk:hh * hk + MLA_NOPE], 0, LANES) for hh in range(MLA_HEADS)]
    v_cols = [w_ukv[:, hh * hk + MLA_NOPE:(hh + 1) * hk] for hh in range(MLA_HEADS)]
    wkv = jnp.concatenate(k_cols + v_cols, axis=1)
    return win.astype(BF16), wuq.astype(BF16), wkv.astype(BF16)


def _rope_tables(s):
    half = MLA_ROPE // 2
    inv = ROPE_BASE ** (-jnp.arange(half, dtype=F32) / half)
    ang = jnp.arange(s).astype(F32)[:, None] * inv
    cos2 = jnp.tile(jnp.cos(ang), (1, 2))
    sin2 = jnp.tile(jnp.sin(ang), (1, 2))
    scale = (MLA_NOPE + MLA_ROPE) ** -0.5 * LOG2E
    tail = LANES - MLA_NOPE - MLA_ROPE
    ones, zeros, ztail = jnp.ones((s, MLA_NOPE), F32), jnp.zeros((s, MLA_NOPE), F32), jnp.zeros((s, tail), F32)
    cosq = jnp.concatenate([ones, cos2, ztail], axis=1) * scale
    sinq = jnp.concatenate([zeros, sin2, ztail], axis=1) * scale
    cosk = jnp.concatenate([zeros, cos2, ztail], axis=1)
    sink = jnp.concatenate([zeros, sin2, ztail], axis=1)
    return cosq, sinq, cosk, sink


def _row2d(v):
    return v.reshape(1, -1).astype(F32)


def kernel(x, ev_pre_g, ev_w_in, ev_cq_g, ev_w_uq, ev_ckv_g, ev_w_ukv, ev_lam_q1, ev_lam_k1, ev_lam_q2,
           ev_lam_k2, ev_subln_g, ev_w_out, ev_post_g, od_pre_g, od_w_in, od_w_out, od_post_g, ffn_pre_g,
           ffn_w_up, ffn_conv_w, ffn_conv_b, ffn_w_down, ffn_post_g):
    b, s, _ = x.shape
    assert b == 1 and s % max(ROW_TILE, ATT_TQ, SB_TQ, DIL_TB * DIL_CONFIGS[-1][1]) == 0
    assert ROW_TILE % ATT_TK == 0 and ATT_TQ % ATT_TK == 0 and SB_TQ % ATT_TK == 0
    assert (ATT_TQ // ATT_TK) % MAIN_UNROLL == 0 and DIL_TB % DIL_T == 0
    assert sum(FF_CHUNKS) == D_FF
    assert ATT_TK <= 256
    xs = x[0]

    def ffn_consts(i):
        return [_row2d(ffn_pre_g[i]), ffn_w_up[i].astype(BF16), ffn_conv_w[i].astype(F32),
                _row2d(ffn_conv_b[i]), ffn_w_down[i].astype(BF16), _row2d(ffn_post_g[i])]

    win, wuq, wkv = _even_weights(ev_w_in[0], ev_w_uq[0], ev_w_ukv[0])
    cosq, sinq, cosk, sink = _rope_tables(s)
    pos_col = (jnp.arange(ROW_TILE) % ATT_TK).astype(BF16)[:, None]
    pos_tile = jnp.pad(jnp.tile(pos_col, (1, SLOPE_PARTS)), ((0, 0), (0, LANES - SLOPE_PARTS)))
    qm, km, vmt, dq, dk, dvt = _proj_even(xs, _row2d(ev_pre_g[0]), win, _row2d(ev_cq_g[0]), wuq,
                                          _row2d(ev_ckv_g[0]), wkv, cosq, sinq, cosk, sink, pos_tile)
    o_mla = _mla_attention(qm, km, vmt)
    lam_p = jnp.pad(jnp.stack([ev_lam_q1[0], ev_lam_k1[0], ev_lam_q2[0], ev_lam_k2[0]]).astype(F32),
                    ((0, SUBLANES - 4), (0, LANES - DIFF_QK)))
    d_slopes = _alibi_slopes(DIFF_HEADS) * LOG2E
    slope_arr = jnp.asarray(np.broadcast_to(d_slopes[:, None, None], (DIFF_HEADS, SUBLANES, LANES)), F32)
    qfeat = jnp.zeros((DIFF_HEADS, ATT_TQ, LANES), F32)
    rest = jnp.asarray(d_slopes, F32)
    for part in range(SLOPE_PARTS):
        piece = rest.astype(BF16).astype(F32)
        qfeat = qfeat.at[:, :, part].set(piece[:, None])
        rest = rest - piece
    lam_init = 0.8 - 0.6 * math.exp(-0.3 * 0)
    o_diff = _diff_attention(lam_p, slope_arr, qfeat.astype(BF16), _row2d(ev_subln_g[0]),
                             dq, dk, dvt, lam_init)
    x2 = _out_ffn(_out_ffn_even_kernel, "out_ffn_even", xs, [o_mla, o_diff], [],
                  [ev_w_out[0].astype(BF16), _row2d(ev_post_g[0])] + ffn_consts(0))

    p = _proj_odd(x2, _row2d(od_pre_g[0]), od_w_in[0].astype(BF16))
    slopes = _alibi_slopes(len(DIL_CONFIGS) * DIL_HEADS_PER_GROUP).reshape(len(DIL_CONFIGS), -1)
    dil_outs = []
    for gi, (_, dil) in enumerate(DIL_CONFIGS):
        dil_outs += list(_dilated_group(p[gi], p[3 + gi], p[6 + gi], dil, slopes[gi]))
    o_sbt = _stickbreak(p[9], p[10], p[11])
    x4 = _out_ffn(_out_ffn_odd_kernel, "out_ffn_odd", x2, dil_outs, [o_sbt],
                  [od_w_out[0].astype(BF16), _row2d(od_post_g[0])] + ffn_consts(1))
    return x4[None]
```

```python
import functools
import math

import numpy as np
import jax
import jax.numpy as jnp
from jax import lax
from jax.experimental import pallas as pl
from jax.experimental.pallas import tpu as pltpu

F32 = jnp.float32
BF16 = jnp.bfloat16

D_MODEL = 1024
NORM_EPS = 1e-6
MLA_HEADS = 8
MLA_Q_RANK = 256
MLA_KV_RANK = 128
MLA_NOPE = 64
MLA_ROPE = 32
MLA_V = 64
ROPE_BASE = 10000.0
DIFF_HEADS = 4
DIFF_QK = 64
DIFF_V = 128
DIL_CONFIGS = ((128, 1), (512, 4), (2048, 16))
DIL_HEADS_PER_GROUP = 4
HEAD_DIM = 64
SB_HEADS = 4
D_FF = 2816
CONV_WIDTH = 3

LANES = 128
SUBLANES = 8
BF16_ROWS = 16
VMEM_LIMIT_BYTES = 56 * 1024 * 1024
ROW_TILE = 512
ATT_TQ = 2048
MAIN_UNROLL = 4
SB_TQ = 512
ATT_TK = 256
DIL_T = 128
DIL_TB = 512
FF_CHUNKS = (512,) * 5 + (256,)
NEG = -1e30
LOG2E = 1.4426950408889634
MAX_JUMP = 32.0
SB_STOP = -110.0
SLOPE_PARTS = 3

_TRANS_B = (((1,), (1,)), ((), ()))


def _dot(a, b):
    return jnp.dot(a, b, preferred_element_type=F32)


def _dot_t(a, b):
    return lax.dot_general(a, b, _TRANS_B, preferred_element_type=F32)


def _rms(xf, g):
    ms = jnp.mean(xf * xf, axis=-1, keepdims=True)
    return xf * lax.rsqrt(ms + NORM_EPS) * g


def _params(sem):
    return pltpu.CompilerParams(dimension_semantics=sem, vmem_limit_bytes=VMEM_LIMIT_BYTES)


def _const_spec(shape):
    nd = len(shape)
    return pl.BlockSpec(shape, lambda *_: (0,) * nd, pipeline_mode=pl.Buffered(1))


def _store_transposed_tiles(dst_ref, src, heads, width, ones_rows):
    tk = ATT_TK
    per = width + ones_rows
    for b in range(src.shape[0] // tk):
        vt = src[b * tk:(b + 1) * tk, :].T.astype(BF16)
        for hh in range(heads):
            dst_ref[b, hh * per:hh * per + width, :] = vt[hh * width:(hh + 1) * width, :]
            if ones_rows:
                dst_ref[b, hh * per + width:(hh + 1) * per, :] = jnp.ones((ones_rows, tk), BF16)


def _proj_even_kernel(x_ref, g_ref, win_ref, cqg_ref, wuq_ref, ckvg_ref, wkv_ref,
                      cosq_ref, sinq_ref, cosk_ref, sink_ref, pos_ref,
                      qm_ref, km_ref, vmt_ref, dq_ref, dk_ref, dvt_ref):
    h = _rms(x_ref[...], g_ref[...]).astype(BF16)
    proj = _dot(h, win_ref[...])
    nq = MLA_HEADS * LANES
    cq = _rms(proj[:, 0:MLA_Q_RANK], cqg_ref[...]).astype(BF16)
    qq = _dot(cq, wuq_ref[...])
    cosq, sinq = cosq_ref[...], sinq_ref[...]
    for hh in range(MLA_HEADS):
        a = qq[:, hh * LANES:(hh + 1) * LANES]
        b = qq[:, nq + hh * LANES:nq + (hh + 1) * LANES]
        qm_ref[:, hh * LANES:(hh + 1) * LANES] = (a * cosq + b * sinq).astype(BF16)
    o = MLA_Q_RANK
    ckv = _rms(proj[:, o:o + MLA_KV_RANK], ckvg_ref[...]).astype(BF16)
    kv = _dot(ckv, wkv_ref[...])
    o += MLA_KV_RANK
    krc = proj[:, o:o + LANES] * cosk_ref[...] + proj[:, o + LANES:o + 2 * LANES] * sink_ref[...]
    for hh in range(MLA_HEADS):
        km_ref[:, hh * LANES:(hh + 1) * LANES] = (kv[:, hh * LANES:(hh + 1) * LANES] + krc).astype(BF16)
    _store_transposed_tiles(vmt_ref, kv[:, nq:nq + MLA_HEADS * MLA_V], MLA_HEADS, MLA_V, BF16_ROWS)
    o += 2 * LANES
    nd = DIFF_HEADS * 2 * DIFF_QK
    dq_ref[...] = (proj[:, o:o + nd] * (DIFF_QK ** -0.5 * LOG2E)).astype(BF16)
    o += nd
    pos = pos_ref[...]
    for hh in range(DIFF_HEADS):
        dk_ref[:, hh * 2 * LANES:hh * 2 * LANES + LANES] = proj[:, o + hh * LANES:o + (hh + 1) * LANES].astype(BF16)
        dk_ref[:, hh * 2 * LANES + LANES:(hh + 1) * 2 * LANES] = pos
    o += nd
    _store_transposed_tiles(dvt_ref, proj[:, o:o + DIFF_HEADS * DIFF_V], DIFF_HEADS, DIFF_V, BF16_ROWS)


def _proj_even(x, g, win, cqg, wuq, ckvg, wkv, cosq, sinq, cosk, sink, pos):
    s = x.shape[0]
    tm, tk = ROW_TILE, ATT_TK
    row = lambda c: pl.BlockSpec((tm, c), lambda i: (i, 0))
    tile_t = lambda r: pl.BlockSpec((tm // tk, r, tk), lambda i: (i, 0, 0))
    mla_rows = MLA_HEADS * (MLA_V + BF16_ROWS)
    diff_rows = DIFF_HEADS * (DIFF_V + BF16_ROWS)
    flat = lambda c: jax.ShapeDtypeStruct((s, c), BF16)
    tiled = lambda r: jax.ShapeDtypeStruct((s // tk, r, tk), BF16)
    return pl.pallas_call(
        _proj_even_kernel,
        grid=(s // tm,),
        in_specs=[row(D_MODEL), _const_spec(g.shape), _const_spec(win.shape), _const_spec(cqg.shape),
                  _const_spec(wuq.shape), _const_spec(ckvg.shape), _const_spec(wkv.shape),
                  row(LANES), row(LANES), row(LANES), row(LANES), _const_spec(pos.shape)],
        out_specs=[row(MLA_HEADS * LANES), row(MLA_HEADS * LANES), tile_t(mla_rows),
                   row(DIFF_HEADS * 2 * DIFF_QK), row(DIFF_HEADS * 2 * LANES), tile_t(diff_rows)],
        out_shape=[flat(MLA_HEADS * LANES), flat(MLA_HEADS * LANES), tiled(mla_rows),
                   flat(DIFF_HEADS * 2 * DIFF_QK), flat(DIFF_HEADS * 2 * LANES), tiled(diff_rows)],
        compiler_params=_params(("parallel",)),
        name="proj_even",
    )(x, g, win, cqg, wuq, ckvg, wkv, cosq, sinq, cosk, sink, pos)


def _flash_cols(chains, i, rows, tile_bias=None, scratch=None):
    tq, tk = ATT_TQ, ATT_TK
    ratio = tq // tk
    n_full = i * ratio

    def safe_step(j, carry):
        out = []
        for (qh, load_k, load_vt), (m, acc) in zip(chains, carry):
            s = _dot_t(load_k(j), qh)
            mb = jnp.max(s, axis=0, keepdims=True)
            if tile_bias is not None:
                c = tile_bias(j)
                mb = mb + c
            m_new = jnp.maximum(m, mb)
            alpha = jnp.exp2(m - m_new)
            shift = m_new if tile_bias is None else m_new - c
            p = jnp.exp2(s - shift).astype(BF16)
            out.append((m_new, alpha * acc + _dot(load_vt(j), p)))
        return tuple(out)

    def fast_step(n, carry):
        j = n_full - 1 - n
        out = []
        for ci, ((qh, load_k, load_vt), (m, alpha, acc, jump)) in enumerate(zip(chains, carry)):
            acc = (acc + _dot(load_vt(j + 1), scratch[ci])) * alpha
            s = _dot_t(load_k(j), qh)
            c = None if tile_bias is None else tile_bias(j)
            shift = m if c is None else m - c
            scratch[ci] = jnp.exp2(s - shift).astype(BF16)
            mb = jnp.max(s, axis=0, keepdims=True)
            if c is not None:
                mb = mb + c
            m_new = jnp.maximum(m, mb)
            out.append((m_new, jnp.exp2(m - m_new), acc, jnp.maximum(jump, mb - m)))
        return tuple(out)

    def fast_group(g, c):
        for t in range(MAIN_UNROLL):
            c = fast_step(g * MAIN_UNROLL + t, c)
        return c

    def diag_step(d, state):
        tri = lax.broadcasted_iota(jnp.int32, (tk, tk), 0) <= lax.broadcasted_iota(jnp.int32, (tk, tk), 1)
        out = []
        for (qh, load_k, load_vt), (ms, accs) in zip(chains, state):
            s = _dot_t(load_k(n_full + d), qh[d * tk:, :])
            vt = load_vt(n_full + d)
            c = None if tile_bias is None else tile_bias(n_full + d)
            ms, accs = list(ms), list(accs)
            for blk in range(d, ratio):
                sb = s[:, (blk - d) * tk:(blk - d + 1) * tk]
                if blk == d:
                    sb = jnp.where(tri, sb, NEG)
                mb = jnp.max(sb, axis=0, keepdims=True)
                if c is not None:
                    mb = mb + c
                m_new = jnp.maximum(ms[blk], mb)
                shift = m_new if c is None else m_new - c
                p = jnp.exp2(sb - shift).astype(BF16)
                accs[blk] = jnp.exp2(ms[blk] - m_new) * accs[blk] + _dot(vt, p)
                ms[blk] = m_new
            out.append((ms, accs))
        return out

    state = [([jnp.full((1, tk), NEG, F32)] * ratio, [jnp.zeros((rows, tk), F32)] * ratio) for _ in chains]
    for d in range(ratio):
        state = diag_step(d, state)
    diag = tuple((jnp.concatenate(ms, axis=1), jnp.concatenate(accs, axis=1)) for ms, accs in state)
    for ci in range(len(chains)):
        scratch[ci] = jnp.zeros((tk, tq), BF16)
    one, low = jnp.ones((1, tq), F32), jnp.full((1, tq), NEG, F32)
    fast = lax.fori_loop(0, n_full // MAIN_UNROLL, fast_group, tuple((m, one, acc, low) for m, acc in diag))
    accs = [(acc + _dot(load_vt(0), scratch[ci])) * alpha
            for ci, ((_, _, load_vt), (_, alpha, acc, _)) in enumerate(zip(chains, fast))]
    worst = functools.reduce(jnp.maximum, [jnp.max(jump) for _, _, _, jump in fast])

    def redo():
        carry = lax.fori_loop(0, n_full, safe_step, diag)
        return [acc for _, acc in carry]

    return lax.cond(worst > MAX_JUMP, redo, lambda: accs)


def _k_tile(k_ref, j, lanes=slice(None)):
    return k_ref[pl.ds(pl.multiple_of(j * ATT_TK, ATT_TK), ATT_TK), lanes]


def _mla_kernel(q_ref, k_ref, vt_ref, o_ref, p_ref):
    i = pl.program_id(1)
    rows = MLA_V + BF16_ROWS
    def chain(hh):
        lanes = slice(hh * LANES, (hh + 1) * LANES)
        return (q_ref[:, lanes], lambda j: _k_tile(k_ref, j, lanes),
                lambda j: vt_ref[j, hh * rows:(hh + 1) * rows, :])

    accs = _flash_cols([chain(0), chain(1)], i, rows, scratch=p_ref)
    halves = [acc[0:MLA_V] / acc[MLA_V:MLA_V + 1] for acc in accs]
    o_ref[...] = jnp.concatenate(halves, axis=0).T.astype(BF16)


def _mla_attention(qm, km, vmt):
    s = qm.shape[0]
    tq = ATT_TQ
    rows = 2 * (MLA_V + BF16_ROWS)
    return pl.pallas_call(
        _mla_kernel,
        grid=(MLA_HEADS // 2, s // tq),
        in_specs=[pl.BlockSpec((tq, 2 * LANES), lambda p, i: (i, p)),
                  pl.BlockSpec((s, 2 * LANES), lambda p, i: (0, p)),
                  pl.BlockSpec((vmt.shape[0], rows, ATT_TK), lambda p, i: (0, p, 0))],
        out_specs=pl.BlockSpec((tq, LANES), lambda p, i: (i, p)),
        out_shape=jax.ShapeDtypeStruct((s, MLA_HEADS * MLA_V), BF16),
        scratch_shapes=[pltpu.VMEM((2, ATT_TK, tq), BF16)],
        compiler_params=_params(("parallel", "parallel")),
        name="mla_attn",
    )(qm, km, vmt)


def _diff_kernel(lam_ref, slope_ref, qf_ref, g_ref, q_ref, k_ref, vt_ref, o_ref, p_ref, *, lam_init):
    i = pl.program_id(1)
    tq = ATT_TQ
    lane = lax.broadcasted_iota(jnp.int32, (tq, LANES), 1)
    qf32 = q_ref[...].astype(F32)
    feat = qf_ref[...]
    zero = jnp.zeros_like(qf32)
    qa = jnp.concatenate([jnp.where(lane < DIFF_QK, qf32, zero).astype(BF16), feat], axis=1)
    qb = jnp.concatenate([jnp.where(lane >= DIFF_QK, qf32, zero).astype(BF16), feat], axis=1)
    slope = slope_ref[0:1, 0:1]
    rows = DIFF_V + BF16_ROWS
    tile_bias = lambda j: slope * (j * ATT_TK - i * tq).astype(F32)
    load_k = lambda j: _k_tile(k_ref, j)
    load_vt = lambda j: vt_ref[j]
    acc1, acc2 = _flash_cols([(qa, load_k, load_vt), (qb, load_k, load_vt)], i, rows, tile_bias, p_ref)
    lp = lam_ref[...]
    s1 = jnp.sum(lp[0:1, :] * lp[1:2, :], axis=1, keepdims=True)
    s2 = jnp.sum(lp[2:3, :] * lp[3:4, :], axis=1, keepdims=True)
    lam = jnp.exp(s1) - jnp.exp(s2) + lam_init
    ot = acc1[0:DIFF_V] / acc1[DIFF_V:DIFF_V + 1] - lam * (acc2[0:DIFF_V] / acc2[DIFF_V:DIFF_V + 1])
    o_ref[...] = (_rms(ot.T, g_ref[...]) * (1.0 - lam_init)).astype(BF16)


def _diff_attention(lam_p, slope_arr, qfeat, subln_g, dq, dk, dvt, lam_init):
    s = dq.shape[0]
    tq = ATT_TQ
    return pl.pallas_call(
        functools.partial(_diff_kernel, lam_init=lam_init),
        grid=(DIFF_HEADS, s // tq),
        in_specs=[pl.BlockSpec(lam_p.shape, lambda h, i: (0, 0)),
                  pl.BlockSpec((None, SUBLANES, LANES), lambda h, i: (h, 0, 0)),
                  pl.BlockSpec((None, tq, LANES), lambda h, i: (h, 0, 0)),
                  pl.BlockSpec(subln_g.shape, lambda h, i: (0, 0)),
                  pl.BlockSpec((tq, LANES), lambda h, i: (i, h)),
                  pl.BlockSpec((s, 2 * LANES), lambda h, i: (0, h)),
                  pl.BlockSpec((dvt.shape[0], DIFF_V + BF16_ROWS, ATT_TK), lambda h, i: (0, h, 0))],
        out_specs=pl.BlockSpec((tq, LANES), lambda h, i: (i, h)),
        out_shape=jax.ShapeDtypeStruct((s, DIFF_HEADS * DIFF_V), BF16),
        scratch_shapes=[pltpu.VMEM((2, ATT_TK, tq), BF16)],
        compiler_params=_params(("parallel", "parallel")),
        name="diff_attn",
    )(lam_p, slope_arr, qfeat, subln_g, dq, dk, dvt)


def _shift_rows(u, k, prev):
    top = jnp.where(lax.broadcasted_iota(jnp.int32, prev.shape, 0) < k,
                    pltpu.roll(prev, k, 0), pltpu.roll(u[0:SUBLANES, :], k, 0))
    return jnp.concatenate([top, pltpu.roll(u, k, 0)[SUBLANES:, :]], axis=0)


def _ffn_tile(x1, fpre, wup_ref, cw_ref, cb_ref, wdn_ref, fpost, carry_ref):
    tm = x1.shape[0]
    hf = _rms(x1, fpre).astype(BF16)
    acc = jnp.zeros((tm, D_MODEL), F32)
    lo = 0
    for width in FF_CHUNKS:
        ys = []
        for part in range(2):
            c0 = part * D_FF + lo
            u = _dot(hf, wup_ref[:, c0:c0 + width])
            prev = carry_ref[:, c0:c0 + width]
            carry_ref[:, c0:c0 + width] = u[tm - SUBLANES:, :]
            ys.append(cw_ref[2:3, c0:c0 + width] * u
                      + cw_ref[1:2, c0:c0 + width] * _shift_rows(u, 1, prev)
                      + cw_ref[0:1, c0:c0 + width] * _shift_rows(u, 2, prev)
                      + cb_ref[:, c0:c0 + width])
        gate, up = ys
        act = (gate * (1.0 / (1.0 + jnp.exp(-gate))) * up).astype(BF16)
        acc = acc + _dot(act, wdn_ref[lo:lo + width, :])
        lo += width
    return x1 + _rms(acc, fpost)


def _out_ffn_even_kernel(x_ref, a_ref, b_ref, wout_ref, postg_ref, fpre_ref, wup_ref, cw_ref, cb_ref,
                         wdn_ref, fpost_ref, o_ref, carry_ref):
    @pl.when(pl.program_id(0) == 0)
    def _():
        carry_ref[...] = jnp.zeros_like(carry_ref)

    na = a_ref.shape[1]
    mix = _dot(a_ref[...], wout_ref[0:na, :]) + _dot(b_ref[...], wout_ref[na:, :])
    x1 = x_ref[...] + _rms(mix, postg_ref[...])
    o_ref[...] = _ffn_tile(x1, fpre_ref[...], wup_ref, cw_ref, cb_ref, wdn_ref, fpost_ref[...], carry_ref)


def _out_ffn_odd_kernel(x_ref, o0_ref, l0_ref, o1_ref, l1_ref, o2_ref, l2_ref, bt_ref, wout_ref, postg_ref,
                        fpre_ref, wup_ref, cw_ref, cb_ref, wdn_ref, fpost_ref, o_ref, carry_ref):
    @pl.when(pl.program_id(0) == 0)
    def _():
        carry_ref[...] = jnp.zeros_like(carry_ref)

    l0, l1, l2 = l0_ref[...], l1_ref[...], l2_ref[...]
    m = jnp.maximum(jnp.maximum(l0, l1), l2)
    e0, e1, e2 = jnp.exp(l0 - m), jnp.exp(l1 - m), jnp.exp(l2 - m)
    dil = (e0 * o0_ref[...] + e1 * o1_ref[...] + e2 * o2_ref[...]) / (e0 + e1 + e2)
    na = dil.shape[1]
    sb = bt_ref[...].T
    mix = _dot(dil.astype(BF16), wout_ref[0:na, :]) + _dot(sb.astype(BF16), wout_ref[na:, :])
    x1 = x_ref[...] + _rms(mix, postg_ref[...])
    o_ref[...] = _ffn_tile(x1, fpre_ref[...], wup_ref, cw_ref, cb_ref, wdn_ref, fpost_ref[...], carry_ref)


def _out_ffn(kernel, name, x, acts, acts_t, consts):
    s = x.shape[0]
    tm = ROW_TILE
    row = lambda c: pl.BlockSpec((tm, c), lambda i: (i, 0))
    col = lambda r: pl.BlockSpec((r, tm), lambda i: (0, i))
    return pl.pallas_call(
        kernel,
        grid=(s // tm,),
        in_specs=([row(D_MODEL)] + [row(a.shape[1]) for a in acts] + [col(a.shape[0]) for a in acts_t]
                  + [_const_spec(c.shape) for c in consts]),
        out_specs=row(D_MODEL),
        out_shape=jax.ShapeDtypeStruct((s, D_MODEL), F32),
        scratch_shapes=[pltpu.VMEM((SUBLANES, 2 * D_FF), F32)],
        compiler_params=_params(("arbitrary",)),
        name=name,
    )(x, *acts, *acts_t, *consts)


def _proj_odd_kernel(x_ref, g_ref, win_ref, *out_refs):
    h = _rms(x_ref[...], g_ref[...]).astype(BF16)
    proj = _dot(h, win_ref[...])
    c = DIL_HEADS_PER_GROUP * HEAD_DIM
    scale = HEAD_DIM ** -0.5
    for n, ref in enumerate(out_refs[:-1]):
        blk = proj[:, n * c:(n + 1) * c]
        if n < 3 or n == 9:
            blk = blk * scale
        ref[...] = blk.astype(BF16)
    _store_transposed_tiles(out_refs[-1], proj[:, 11 * c:12 * c], SB_HEADS, HEAD_DIM, 0)


def _proj_odd(x, g, win):
    s = x.shape[0]
    tm, tk = ROW_TILE, ATT_TK
    c = DIL_HEADS_PER_GROUP * HEAD_DIM
    row = lambda w: pl.BlockSpec((tm, w), lambda i: (i, 0))
    return pl.pallas_call(
        _proj_odd_kernel,
        grid=(s // tm,),
        in_specs=[row(D_MODEL), _const_spec(g.shape), _const_spec(win.shape)],
        out_specs=[row(c)] * 11 + [pl.BlockSpec((tm // tk, c, tk), lambda i: (i, 0, 0))],
        out_shape=[jax.ShapeDtypeStruct((s, c), BF16)] * 11 + [jax.ShapeDtypeStruct((s // tk, c, tk), BF16)],
        compiler_params=_params(("parallel",)),
        name="proj_odd",
    )(x, g, win)


def _dilated_kernel(bias_ref, q_ref, kp_ref, kc_ref, vp_ref, vc_ref, o_ref, lse_ref, *, nub):
    t, tb = DIL_T, DIL_TB
    ub = pl.program_id(0) % nub
    c = DIL_HEADS_PER_GROUP * HEAD_DIM
    qf = q_ref[...].astype(F32)
    kcat = jnp.concatenate([kp_ref[...], kc_ref[...]], axis=0)
    vcat = jnp.concatenate([vp_ref[...], vc_ref[...]], axis=0)
    lane = lax.broadcasted_iota(jnp.int32, (tb, c), 1)
    col = lax.broadcasted_iota(jnp.int32, (tb, t + tb), 1)
    no_prev = col < jnp.where(ub == 0, t, 0)
    out = jnp.zeros((tb, c), F32)
    lse = jnp.zeros((tb, c), F32)
    for hh in range(DIL_HEADS_PER_GROUP):
        mine = jnp.logical_and(lane >= hh * HEAD_DIM, lane < (hh + 1) * HEAD_DIM)
        qh = jnp.where(mine, qf, 0.0).astype(BF16)
        s = _dot_t(qh, kcat) + bias_ref[hh]
        s = jnp.where(no_prev, NEG, s)
        m = jnp.max(s, axis=1, keepdims=True)
        e = jnp.exp(s - m)
        den = jnp.sum(e, axis=1, keepdims=True)
        oh = _dot(e.astype(BF16), vcat) / den
        out = jnp.where(mine, oh, out)
        lse = jnp.where(mine, m + jnp.log(den), lse)
    o_ref[...] = out
    lse_ref[...] = lse


def _dilated_group(q, k, v, dil, slopes):
    s, c = q.shape
    t, tb = DIL_T, DIL_TB
    nub = s // dil // tb
    a = np.arange(tb)[:, None]
    cc = np.arange(t + tb)[None, :]
    steps = t + a - cc
    ok = (steps >= 0) & (steps <= t)
    dist = (steps * dil).astype(np.float32)
    bias = np.where(ok[None], -np.asarray(slopes, np.float32)[:, None, None] * dist[None], np.float32(NEG))
    bias = jnp.asarray(bias, F32)
    view = lambda z: z.reshape(s // dil, dil * c)
    cur = pl.BlockSpec((tb, c), lambda b: (b % nub, b // nub))
    prev = pl.BlockSpec((t, c), lambda b: (jnp.maximum(b % nub * (tb // t) - 1, 0), b // nub))
    o, lse = pl.pallas_call(
        functools.partial(_dilated_kernel, nub=nub),
        grid=(dil * nub,),
        in_specs=[_const_spec(bias.shape), cur, prev, cur, prev, cur],
        out_specs=[cur, cur],
        out_shape=[jax.ShapeDtypeStruct((s // dil, dil * c), F32)] * 2,
        compiler_params=_params(("parallel",)),
        name=f"dilated_d{dil}",
    )(bias, view(q), view(k), view(k), view(v), view(v))
    return o.reshape(s, c), lse.reshape(s, c)


def _sb_kernel(tri_ref, q_ref, k_ref, vt_ref, o_ref):
    hh = pl.program_id(0)
    i = pl.program_id(1)
    tq, tk = SB_TQ, ATT_TK
    ratio = tq // tk
    c = SB_HEADS * HEAD_DIM
    tri = tri_ref[...]
    lane = lax.broadcasted_iota(jnp.int32, (tq, c), 1)
    mine = jnp.logical_and(lane >= hh * HEAD_DIM, lane < (hh + 1) * HEAD_DIM)
    qh = jnp.where(mine, q_ref[...].astype(F32), 0.0).astype(BF16)
    krow = lax.broadcasted_iota(jnp.int32, (tk, tq), 0)
    qcol = lax.broadcasted_iota(jnp.int32, (tk, tq), 1)

    def step(j, carry, diag):
        r, acc = carry
        z = _dot_t(_k_tile(k_ref, j), qh)
        sp = jnp.log1p(jnp.exp(-jnp.abs(z)))
        log_beta = jnp.minimum(z, 0.0) - sp
        log_keep = log_beta - z
        if diag is not None:
            strict = krow + diag * tk < qcol
            log_keep = jnp.where(strict, log_keep, 0.0)
        hi = log_keep.astype(BF16)
        lo = (log_keep - hi.astype(F32)).astype(BF16)
        aft = _dot(tri, jnp.concatenate([hi, lo], axis=0))
        a = jnp.exp(log_beta + aft[0:tk] + r)
        if diag is not None:
            a = jnp.where(strict, a, 0.0)
        acc = acc + _dot(vt_ref[j], a.astype(BF16))
        return r + aft[tk:tk + 1], acc

    carry = (jnp.zeros((1, tq), F32), jnp.zeros((HEAD_DIM, tq), F32))
    n_full = i * ratio
    carry = lax.fori_loop(0, ratio, lambda dd, cr: step(n_full + ratio - 1 - dd, cr, ratio - 1 - dd), carry)

    def more(state):
        n, r, _ = state
        return jnp.logical_and(n < n_full, jnp.max(r) > SB_STOP)

    def walk(state):
        n, r, acc = state
        r, acc = step(n_full - 1 - n, (r, acc), None)
        return n + 1, r, acc

    o_ref[...] = lax.while_loop(more, walk, (jnp.int32(0),) + carry)[2]


def _stickbreak(q, k, vt):
    s, c = q.shape
    tq, tk = SB_TQ, ATT_TK
    j = np.arange(tk)
    later = (j[None, :] > j[:, None]).astype(np.float32)
    tri = np.concatenate([np.concatenate([later, later], axis=1), np.ones((BF16_ROWS, 2 * tk), np.float32)], axis=0)
    tri = jnp.asarray(tri, BF16)
    return pl.pallas_call(
        _sb_kernel,
        grid=(SB_HEADS, s // tq),
        in_specs=[_const_spec(tri.shape),
                  pl.BlockSpec((tq, c), lambda h, i: (i, 0)),
                  _const_spec((s, c)),
                  pl.BlockSpec((s // tk, HEAD_DIM, tk), lambda h, i: (0, h, 0))],
        out_specs=pl.BlockSpec((HEAD_DIM, tq), lambda h, i: (h, i)),
        out_shape=jax.ShapeDtypeStruct((c, s), F32),
        compiler_params=_params(("parallel", "parallel")),
        name="stickbreak",
    )(tri, q, k, vt)


def _alibi_slopes(n):
    return 2.0 ** (-8.0 * np.arange(1, n + 1) / n)


def _pad_cols(w, lo, width):
    return jnp.pad(w, ((0, 0), (lo, width - lo - w.shape[1])))


def _rot_half_cols(w):
    half = w.shape[1] // 2
    return jnp.concatenate([-w[:, half:], w[:, :half]], axis=1)


def _even_weights(w_in, w_uq, w_ukv):
    o_kr = MLA_Q_RANK + MLA_KV_RANK
    w_kr = w_in[:, o_kr:o_kr + MLA_ROPE]
    win = jnp.concatenate([w_in[:, :o_kr], _pad_cols(w_kr, MLA_NOPE, LANES),
                           _pad_cols(_rot_half_cols(w_kr), MLA_NOPE, LANES),
                           w_in[:, o_kr + MLA_ROPE:]], axis=1)
    hd = MLA_NOPE + MLA_ROPE
    q_plain, q_rot = [], []
    for hh in range(MLA_HEADS):
        wh = w_uq[:, hh * hd:(hh + 1) * hd]
        q_plain.append(_pad_cols(wh, 0, LANES))
        q_rot.append(_pad_cols(_rot_half_cols(wh[:, MLA_NOPE:]), MLA_NOPE, LANES))
    wuq = jnp.concatenate(q_plain + q_rot, axis=1)
    hk = MLA_NOPE + MLA_V
    k_cols = [_pad_cols(w_ukv[:, hh * hk:hh * hk + MLA_NOPE], 0, LANES) for hh in range(MLA_HEADS)]
    v_cols = [w_ukv[:, hh * hk + MLA_NOPE:(hh + 1) * hk] for hh in range(MLA_HEADS)]
    wkv = jnp.concatenate(k_cols + v_cols, axis=1)
    return win.astype(BF16), wuq.astype(BF16), wkv.astype(BF16)


def _rope_tables(s):
    half = MLA_ROPE // 2
    inv = ROPE_BASE ** (-jnp.arange(half, dtype=F32) / half)
    ang = jnp.arange(s).astype(F32)[:, None] * inv
    cos2 = jnp.tile(jnp.cos(ang), (1, 2))
    sin2 = jnp.tile(jnp.sin(ang), (1, 2))
    scale = (MLA_NOPE + MLA_ROPE) ** -0.5 * LOG2E
    tail = LANES - MLA_NOPE - MLA_ROPE
    ones, zeros, ztail = jnp.ones((s, MLA_NOPE), F32), jnp.zeros((s, MLA_NOPE), F32), jnp.zeros((s, tail), F32)
    cosq = jnp.concatenate([ones, cos2, ztail], axis=1) * scale
    sinq = jnp.concatenate([zeros, sin2, ztail], axis=1) * scale
    cosk = jnp.concatenate([zeros, cos2, ztail], axis=1)
    sink = jnp.concatenate([zeros, sin2, ztail], axis=1)
    return cosq, sinq, cosk, sink


def _row2d(v):
    return v.reshape(1, -1).astype(F32)


def kernel(x, ev_pre_g, ev_w_in, ev_cq_g, ev_w_uq, ev_ckv_g, ev_w_ukv, ev_lam_q1, ev_lam_k1, ev_lam_q2,
           ev_lam_k2, ev_subln_g, ev_w_out, ev_post_g, od_pre_g, od_w_in, od_w_out, od_post_g, ffn_pre_g,
           ffn_w_up, ffn_conv_w, ffn_conv_b, ffn_w_down, ffn_post_g):
    b, s, _ = x.shape
    assert b == 1 and s % max(ROW_TILE, ATT_TQ, SB_TQ, DIL_TB * DIL_CONFIGS[-1][1]) == 0
    assert ROW_TILE % ATT_TK == 0 and ATT_TQ % ATT_TK == 0 and SB_TQ % ATT_TK == 0
    assert (ATT_TQ // ATT_TK) % MAIN_UNROLL == 0 and DIL_TB % DIL_T == 0
    assert sum(FF_CHUNKS) == D_FF
    assert ATT_TK <= 256
    xs = x[0]

    def ffn_consts(i):
        return [_row2d(ffn_pre_g[i]), ffn_w_up[i].astype(BF16), ffn_conv_w[i].astype(F32),
                _row2d(ffn_conv_b[i]), ffn_w_down[i].astype(BF16), _row2d(ffn_post_g[i])]

    win, wuq, wkv = _even_weights(ev_w_in[0], ev_w_uq[0], ev_w_ukv[0])
    cosq, sinq, cosk, sink = _rope_tables(s)
    pos_col = (jnp.arange(ROW_TILE) % ATT_TK).astype(BF16)[:, None]
    pos_tile = jnp.pad(jnp.tile(pos_col, (1, SLOPE_PARTS)), ((0, 0), (0, LANES - SLOPE_PARTS)))
    qm, km, vmt, dq, dk, dvt = _proj_even(xs, _row2d(ev_pre_g[0]), win, _row2d(ev_cq_g[0]), wuq,
                                          _row2d(ev_ckv_g[0]), wkv, cosq, sinq, cosk, sink, pos_tile)
    o_mla = _mla_attention(qm, km, vmt)
    lam_p = jnp.pad(jnp.stack([ev_lam_q1[0], ev_lam_k1[0], ev_lam_q2[0], ev_lam_k2[0]]).astype(F32),
                    ((0, SUBLANES - 4), (0, LANES - DIFF_QK)))
    d_slopes = _alibi_slopes(DIFF_HEADS) * LOG2E
    slope_arr = jnp.asarray(np.broadcast_to(d_slopes[:, None, None], (DIFF_HEADS, SUBLANES, LANES)), F32)
    qfeat = jnp.zeros((DIFF_HEADS, ATT_TQ, LANES), F32)
    rest = jnp.asarray(d_slopes, F32)
    for part in range(SLOPE_PARTS):
        piece = rest.astype(BF16).astype(F32)
        qfeat = qfeat.at[:, :, part].set(piece[:, None])
        rest = rest - piece
    lam_init = 0.8 - 0.6 * math.exp(-0.3 * 0)
    o_diff = _diff_attention(lam_p, slope_arr, qfeat.astype(BF16), _row2d(ev_subln_g[0]),
                             dq, dk, dvt, lam_init)
    x2 = _out_ffn(_out_ffn_even_kernel, "out_ffn_even", xs, [o_mla, o_diff], [],
                  [ev_w_out[0].astype(BF16), _row2d(ev_post_g[0])] + ffn_consts(0))

    p = _proj_odd(x2, _row2d(od_pre_g[0]), od_w_in[0].astype(BF16))
    slopes = _alibi_slopes(len(DIL_CONFIGS) * DIL_HEADS_PER_GROUP).reshape(len(DIL_CONFIGS), -1)
    dil_outs = []
    for gi, (_, dil) in enumerate(DIL_CONFIGS):
        dil_outs += list(_dilated_group(p[gi], p[3 + gi], p[6 + gi], dil, slopes[gi]))
    o_sbt = _stickbreak(p[9], p[10], p[11])
    x4 = _out_ffn(_out_ffn_odd_kernel, "out_ffn_odd", x2, dil_outs, [o_sbt],
                  [od_w_out[0].astype(BF16), _row2d(od_post_g[0])] + ffn_consts(1))
    return x4[None]
```

```python
import functools
import math

import numpy as np
import jax
import jax.numpy as jnp
from jax import lax
from jax.experimental import pallas as pl
from jax.experimental.pallas import tpu as pltpu

F32 = jnp.float32
BF16 = jnp.bfloat16

D_MODEL = 1024
NORM_EPS = 1e-6
MLA_HEADS = 8
MLA_Q_RANK = 256
MLA_KV_RANK = 128
MLA_NOPE = 64
MLA_ROPE = 32
MLA_V = 64
ROPE_BASE = 10000.0
DIFF_HEADS = 4
DIFF_QK = 64
DIFF_V = 128
DIL_CONFIGS = ((128, 1), (512, 4), (2048, 16))
DIL_HEADS_PER_GROUP = 4
HEAD_DIM = 64
SB_HEADS = 4
D_FF = 2816
CONV_WIDTH = 3

LANES = 128
SUBLANES = 8
BF16_ROWS = 16
VMEM_LIMIT_BYTES = 56 * 1024 * 1024
ROW_TILE = 512
ATT_TQ = 2048
MAIN_UNROLL = 4
SB_TQ = 512
ATT_TK = 256
DIL_T = 128
DIL_TB = 512
FF_CHUNKS = (512,) * 5 + (256,)
NEG = -1e30
LOG2E = 1.4426950408889634
MAX_JUMP = 32.0
SB_STOP = -110.0
SLOPE_PARTS = 3

_TRANS_B = (((1,), (1,)), ((), ()))


def _dot(a, b):
    return jnp.dot(a, b, preferred_element_type=F32)


def _dot_t(a, b):
    return lax.dot_general(a, b, _TRANS_B, preferred_element_type=F32)


def _rms(xf, g):
    ms = jnp.mean(xf * xf, axis=-1, keepdims=True)
    return xf * lax.rsqrt(ms + NORM_EPS) * g


def _params(sem):
    return pltpu.CompilerParams(dimension_semantics=sem, vmem_limit_bytes=VMEM_LIMIT_BYTES)


def _const_spec(shape):
    nd = len(shape)
    return pl.BlockSpec(shape, lambda *_: (0,) * nd, pipeline_mode=pl.Buffered(1))


def _store_transposed_tiles(dst_ref, src, heads, width, ones_rows):
    tk = ATT_TK
    per = width + ones_rows
    for b in range(src.shape[0] // tk):
        vt = src[b * tk:(b + 1) * tk, :].T.astype(BF16)
        for hh in range(heads):
            dst_ref[b, hh * per:hh * per + width, :] = vt[hh * width:(hh + 1) * width, :]
            if ones_rows:
                dst_ref[b, hh * per + width:(hh + 1) * per, :] = jnp.ones((ones_rows, tk), BF16)


def _rows_to_classes(x, dil, scr):
    if dil == 1:
        return x
    tm, c = x.shape
    for h in range(c // LANES):
        scr[h] = x[:, h * LANES:(h + 1) * LANES]
    return jnp.concatenate([scr[h, pl.ds(r, tm // dil, stride=dil), :]
                            for r in range(dil) for h in range(c // LANES)], axis=1)


def _classes_to_rows(ref, dil, scr):
    if dil == 1:
        return ref[...]
    n, c = ref.shape[0], ref.shape[1] // dil
    for r in range(dil):
        for h in range(c // LANES):
            scr[h, pl.ds(r, n, stride=dil), :] = ref[:, r * c + h * LANES:r * c + (h + 1) * LANES]
    return jnp.concatenate([scr[h] for h in range(c // LANES)], axis=1)


def _proj_even_kernel(x_ref, g_ref, win_ref, cqg_ref, wuq_ref, ckvg_ref, wkv_ref,
                      cosq_ref, sinq_ref, cosk_ref, sink_ref, pos_ref,
                      qm_ref, km_ref, vmt_ref, dq_ref, dk_ref, dvt_ref):
    h = _rms(x_ref[...], g_ref[...]).astype(BF16)
    proj = _dot(h, win_ref[...])
    nq = MLA_HEADS * LANES
    cq = _rms(proj[:, 0:MLA_Q_RANK], cqg_ref[...]).astype(BF16)
    qq = _dot(cq, wuq_ref[...])
    cosq, sinq = cosq_ref[...], sinq_ref[...]
    for hh in range(MLA_HEADS):
        a = qq[:, hh * LANES:(hh + 1) * LANES]
        b = qq[:, nq + hh * LANES:nq + (hh + 1) * LANES]
        qm_ref[:, hh * LANES:(hh + 1) * LANES] = (a * cosq + b * sinq).astype(BF16)
    o = MLA_Q_RANK
    ckv = _rms(proj[:, o:o + MLA_KV_RANK], ckvg_ref[...]).astype(BF16)
    kv = _dot(ckv, wkv_ref[...])
    o += MLA_KV_RANK
    krc = proj[:, o:o + LANES] * cosk_ref[...] + proj[:, o + LANES:o + 2 * LANES] * sink_ref[...]
    for hh in range(MLA_HEADS):
        km_ref[:, hh * LANES:(hh + 1) * LANES] = (kv[:, hh * LANES:(hh + 1) * LANES] + krc).astype(BF16)
    _store_transposed_tiles(vmt_ref, kv[:, nq:nq + MLA_HEADS * MLA_V], MLA_HEADS, MLA_V, BF16_ROWS)
    o += 2 * LANES
    nd = DIFF_HEADS * 2 * DIFF_QK
    dq_ref[...] = (proj[:, o:o + nd] * (DIFF_QK ** -0.5 * LOG2E)).astype(BF16)
    o += nd
    pos = pos_ref[...]
    for hh in range(DIFF_HEADS):
        dk_ref[:, hh * 2 * LANES:hh * 2 * LANES + LANES] = proj[:, o + hh * LANES:o + (hh + 1) * LANES].astype(BF16)
        dk_ref[:, hh * 2 * LANES + LANES:(hh + 1) * 2 * LANES] = pos
    o += nd
    _store_transposed_tiles(dvt_ref, proj[:, o:o + DIFF_HEADS * DIFF_V], DIFF_HEADS, DIFF_V, BF16_ROWS)


def _proj_even(x, g, win, cqg, wuq, ckvg, wkv, cosq, sinq, cosk, sink, pos):
    s = x.shape[0]
    tm, tk = ROW_TILE, ATT_TK
    row = lambda c: pl.BlockSpec((tm, c), lambda i: (i, 0))
    tile_t = lambda r: pl.BlockSpec((tm // tk, r, tk), lambda i: (i, 0, 0))
    mla_rows = MLA_HEADS * (MLA_V + BF16_ROWS)
    diff_rows = DIFF_HEADS * (DIFF_V + BF16_ROWS)
    flat = lambda c: jax.ShapeDtypeStruct((s, c), BF16)
    tiled = lambda r: jax.ShapeDtypeStruct((s // tk, r, tk), BF16)
    return pl.pallas_call(
        _proj_even_kernel,
        grid=(s // tm,),
        in_specs=[row(D_MODEL), _const_spec(g.shape), _const_spec(win.shape), _const_spec(cqg.shape),
                  _const_spec(wuq.shape), _const_spec(ckvg.shape), _const_spec(wkv.shape),
                  row(LANES), row(LANES), row(LANES), row(LANES), _const_spec(pos.shape)],
        out_specs=[row(MLA_HEADS * LANES), row(MLA_HEADS * LANES), tile_t(mla_rows),
                   row(DIFF_HEADS * 2 * DIFF_QK), row(DIFF_HEADS * 2 * LANES), tile_t(diff_rows)],
        out_shape=[flat(MLA_HEADS * LANES), flat(MLA_HEADS * LANES), tiled(mla_rows),
                   flat(DIFF_HEADS * 2 * DIFF_QK), flat(DIFF_HEADS * 2 * LANES), tiled(diff_rows)],
        compiler_params=_params(("parallel",)),
        name="proj_even",
    )(x, g, win, cqg, wuq, ckvg, wkv, cosq, sinq, cosk, sink, pos)


def _flash_cols(chains, i, rows, tile_bias=None, scratch=None):
    tq, tk = ATT_TQ, ATT_TK
    ratio = tq // tk
    n_full = i * ratio

    def safe_step(j, carry):
        out = []
        for (qh, load_k, load_vt), (m, acc) in zip(chains, carry):
            s = _dot_t(load_k(j), qh)
            mb = jnp.max(s, axis=0, keepdims=True)
            if tile_bias is not None:
                c = tile_bias(j)
                mb = mb + c
            m_new = jnp.maximum(m, mb)
            alpha = jnp.exp2(m - m_new)
            shift = m_new if tile_bias is None else m_new - c
            p = jnp.exp2(s - shift).astype(BF16)
            out.append((m_new, alpha * acc + _dot(load_vt(j), p)))
        return tuple(out)

    def fast_step(n, carry):
        j = n_full - 1 - n
        out = []
        for ci, ((qh, load_k, load_vt), (m, alpha, acc, jump)) in enumerate(zip(chains, carry)):
            acc = (acc + _dot(load_vt(j + 1), scratch[ci])) * alpha
            s = _dot_t(load_k(j), qh)
            c = None if tile_bias is None else tile_bias(j)
            shift = m if c is None else m - c
            scratch[ci] = jnp.exp2(s - shift).astype(BF16)
            mb = jnp.max(s, axis=0, keepdims=True)
            if c is not None:
                mb = mb + c
            m_new = jnp.maximum(m, mb)
            out.append((m_new, jnp.exp2(m - m_new), acc, jnp.maximum(jump, mb - m)))
        return tuple(out)

    def fast_group(g, c):
        for t in range(MAIN_UNROLL):
            c = fast_step(g * MAIN_UNROLL + t, c)
        return c

    def diag_step(d, state):
        tri = lax.broadcasted_iota(jnp.int32, (tk, tk), 0) <= lax.broadcasted_iota(jnp.int32, (tk, tk), 1)
        out = []
        for (qh, load_k, load_vt), (ms, accs) in zip(chains, state):
            s = _dot_t(load_k(n_full + d), qh[d * tk:, :])
            vt = load_vt(n_full + d)
            c = None if tile_bias is None else tile_bias(n_full + d)
            ms, accs = list(ms), list(accs)
            for blk in range(d, ratio):
                sb = s[:, (blk - d) * tk:(blk - d + 1) * tk]
                if blk == d:
                    sb = jnp.where(tri, sb, NEG)
                mb = jnp.max(sb, axis=0, keepdims=True)
                if c is not None:
                    mb = mb + c
                m_new = jnp.maximum(ms[blk], mb)
                shift = m_new if c is None else m_new - c
                p = jnp.exp2(sb - shift).astype(BF16)
                accs[blk] = jnp.exp2(ms[blk] - m_new) * accs[blk] + _dot(vt, p)
                ms[blk] = m_new
            out.append((ms, accs))
        return out

    state = [([jnp.full((1, tk), NEG, F32)] * ratio, [jnp.zeros((rows, tk), F32)] * ratio) for _ in chains]
    for d in range(ratio):
        state = diag_step(d, state)
    diag = tuple((jnp.concatenate(ms, axis=1), jnp.concatenate(accs, axis=1)) for ms, accs in state)
    for ci in range(len(chains)):
        scratch[ci] = jnp.zeros((tk, tq), BF16)
    one, low = jnp.ones((1, tq), F32), jnp.full((1, tq), NEG, F32)
    fast = lax.fori_loop(0, n_full // MAIN_UNROLL, fast_group, tuple((m, one, acc, low) for m, acc in diag))
    accs = [(acc + _dot(load_vt(0), scratch[ci])) * alpha
            for ci, ((_, _, load_vt), (_, alpha, acc, _)) in enumerate(zip(chains, fast))]
    worst = functools.reduce(jnp.maximum, [jnp.max(jump) for _, _, _, jump in fast])

    def redo():
        carry = lax.fori_loop(0, n_full, safe_step, diag)
        return [acc for _, acc in carry]

    return lax.cond(worst > MAX_JUMP, redo, lambda: accs)


def _k_tile(k_ref, j, lanes=slice(None)):
    return k_ref[pl.ds(pl.multiple_of(j * ATT_TK, ATT_TK), ATT_TK), lanes]


def _mla_kernel(q_ref, k_ref, vt_ref, o_ref, p_ref):
    i = pl.program_id(1)
    rows = MLA_V + BF16_ROWS
    def chain(hh):
        lanes = slice(hh * LANES, (hh + 1) * LANES)
        return (q_ref[:, lanes], lambda j: _k_tile(k_ref, j, lanes),
                lambda j: vt_ref[j, hh * rows:(hh + 1) * rows, :])

    accs = _flash_cols([chain(0), chain(1)], i, rows, scratch=p_ref)
    halves = [acc[0:MLA_V] / acc[MLA_V:MLA_V + 1] for acc in accs]
    o_ref[...] = jnp.concatenate(halves, axis=0).T.astype(BF16)


def _mla_attention(qm, km, vmt):
    s = qm.shape[0]
    tq = ATT_TQ
    rows = 2 * (MLA_V + BF16_ROWS)
    return pl.pallas_call(
        _mla_kernel,
        grid=(MLA_HEADS // 2, s // tq),
        in_specs=[pl.BlockSpec((tq, 2 * LANES), lambda p, i: (i, p)),
                  pl.BlockSpec((s, 2 * LANES), lambda p, i: (0, p)),
                  pl.BlockSpec((vmt.shape[0], rows, ATT_TK), lambda p, i: (0, p, 0))],
        out_specs=pl.BlockSpec((tq, LANES), lambda p, i: (i, p)),
        out_shape=jax.ShapeDtypeStruct((s, MLA_HEADS * MLA_V), BF16),
        scratch_shapes=[pltpu.VMEM((2, ATT_TK, tq), BF16)],
        compiler_params=_params(("parallel", "parallel")),
        name="mla_attn",
    )(qm, km, vmt)


def _diff_kernel(lam_ref, slope_ref, qf_ref, g_ref, q_ref, k_ref, vt_ref, o_ref, p_ref, *, lam_init):
    i = pl.program_id(1)
    tq = ATT_TQ
    lane = lax.broadcasted_iota(jnp.int32, (tq, LANES), 1)
    qf32 = q_ref[...].astype(F32)
    feat = qf_ref[...]
    zero = jnp.zeros_like(qf32)
    qa = jnp.concatenate([jnp.where(lane < DIFF_QK, qf32, zero).astype(BF16), feat], axis=1)
    qb = jnp.concatenate([jnp.where(lane >= DIFF_QK, qf32, zero).astype(BF16), feat], axis=1)
    slope = slope_ref[0:1, 0:1]
    rows = DIFF_V + BF16_ROWS
    tile_bias = lambda j: slope * (j * ATT_TK - i * tq).astype(F32)
    load_k = lambda j: _k_tile(k_ref, j)
    load_vt = lambda j: vt_ref[j]
    acc1, acc2 = _flash_cols([(qa, load_k, load_vt), (qb, load_k, load_vt)], i, rows, tile_bias, p_ref)
    lp = lam_ref[...]
    s1 = jnp.sum(lp[0:1, :] * lp[1:2, :], axis=1, keepdims=True)
    s2 = jnp.sum(lp[2:3, :] * lp[3:4, :], axis=1, keepdims=True)
    lam = jnp.exp(s1) - jnp.exp(s2) + lam_init
    ot = acc1[0:DIFF_V] / acc1[DIFF_V:DIFF_V + 1] - lam * (acc2[0:DIFF_V] / acc2[DIFF_V:DIFF_V + 1])
    o_ref[...] = (_rms(ot.T, g_ref[...]) * (1.0 - lam_init)).astype(BF16)


def _diff_attention(lam_p, slope_arr, qfeat, subln_g, dq, dk, dvt, lam_init):
    s = dq.shape[0]
    tq = ATT_TQ
    return pl.pallas_call(
        functools.partial(_diff_kernel, lam_init=lam_init),
        grid=(DIFF_HEADS, s // tq),
        in_specs=[pl.BlockSpec(lam_p.shape, lambda h, i: (0, 0)),
                  pl.BlockSpec((None, SUBLANES, LANES), lambda h, i: (h, 0, 0)),
                  pl.BlockSpec((None, tq, LANES), lambda h, i: (h, 0, 0)),
                  pl.BlockSpec(subln_g.shape, lambda h, i: (0, 0)),
                  pl.BlockSpec((tq, LANES), lambda h, i: (i, h)),
                  pl.BlockSpec((s, 2 * LANES), lambda h, i: (0, h)),
                  pl.BlockSpec((dvt.shape[0], DIFF_V + BF16_ROWS, ATT_TK), lambda h, i: (0, h, 0))],
        out_specs=pl.BlockSpec((tq, LANES), lambda h, i: (i, h)),
        out_shape=jax.ShapeDtypeStruct((s, DIFF_HEADS * DIFF_V), BF16),
        scratch_shapes=[pltpu.VMEM((2, ATT_TK, tq), BF16)],
        compiler_params=_params(("parallel", "parallel")),
        name="diff_attn",
    )(lam_p, slope_arr, qfeat, subln_g, dq, dk, dvt)


def _shift_rows(u, k, prev):
    top = jnp.where(lax.broadcasted_iota(jnp.int32, prev.shape, 0) < k,
                    pltpu.roll(prev, k, 0), pltpu.roll(u[0:SUBLANES, :], k, 0))
    return jnp.concatenate([top, pltpu.roll(u, k, 0)[SUBLANES:, :]], axis=0)


def _ffn_tile(x1, fpre, wup_ref, cw_ref, cb_ref, wdn_ref, fpost, carry_ref):
    tm = x1.shape[0]
    hf = _rms(x1, fpre).astype(BF16)
    acc = jnp.zeros((tm, D_MODEL), F32)
    lo = 0
    for width in FF_CHUNKS:
        ys = []
        for part in range(2):
            c0 = part * D_FF + lo
            u = _dot(hf, wup_ref[:, c0:c0 + width])
            prev = carry_ref[:, c0:c0 + width]
            carry_ref[:, c0:c0 + width] = u[tm - SUBLANES:, :]
            ys.append(cw_ref[2:3, c0:c0 + width] * u
                      + cw_ref[1:2, c0:c0 + width] * _shift_rows(u, 1, prev)
                      + cw_ref[0:1, c0:c0 + width] * _shift_rows(u, 2, prev)
                      + cb_ref[:, c0:c0 + width])
        gate, up = ys
        act = (gate * (1.0 / (1.0 + jnp.exp(-gate))) * up).astype(BF16)
        acc = acc + _dot(act, wdn_ref[lo:lo + width, :])
        lo += width
    return x1 + _rms(acc, fpost)


def _out_ffn_even_kernel(x_ref, a_ref, b_ref, wout_ref, postg_ref, fpre_ref, wup_ref, cw_ref, cb_ref,
                         wdn_ref, fpost_ref, o_ref, carry_ref):
    @pl.when(pl.program_id(0) == 0)
    def _():
        carry_ref[...] = jnp.zeros_like(carry_ref)

    na = a_ref.shape[1]
    mix = _dot(a_ref[...], wout_ref[0:na, :]) + _dot(b_ref[...], wout_ref[na:, :])
    x1 = x_ref[...] + _rms(mix, postg_ref[...])
    o_ref[...] = _ffn_tile(x1, fpre_ref[...], wup_ref, cw_ref, cb_ref, wdn_ref, fpost_ref[...], carry_ref)


def _out_ffn_odd_kernel(x_ref, o0_ref, l0_ref, o1_ref, l1_ref, o2_ref, l2_ref, bt_ref, wout_ref, postg_ref,
                        fpre_ref, wup_ref, cw_ref, cb_ref, wdn_ref, fpost_ref, o_ref, carry_ref, scr):
    @pl.when(pl.program_id(0) == 0)
    def _():
        carry_ref[...] = jnp.zeros_like(carry_ref)

    (l0, o0), (l1, o1), (l2, o2) = [(_classes_to_rows(l_ref, d, scr), _classes_to_rows(og_ref, d, scr))
                                    for (l_ref, og_ref), (_, d) in zip(((l0_ref, o0_ref), (l1_ref, o1_ref),
                                                                        (l2_ref, o2_ref)), DIL_CONFIGS)]
    m = jnp.maximum(jnp.maximum(l0, l1), l2)
    e0, e1, e2 = jnp.exp(l0 - m), jnp.exp(l1 - m), jnp.exp(l2 - m)
    dil = (e0 * o0 + e1 * o1 + e2 * o2) / (e0 + e1 + e2)
    na = dil.shape[1]
    sb = bt_ref[...].T
    mix = _dot(dil.astype(BF16), wout_ref[0:na, :]) + _dot(sb.astype(BF16), wout_ref[na:, :])
    x1 = x_ref[...] + _rms(mix, postg_ref[...])
    o_ref[...] = _ffn_tile(x1, fpre_ref[...], wup_ref, cw_ref, cb_ref, wdn_ref, fpost_ref[...], carry_ref)


def _out_ffn(kernel, name, x, acts, acts_t, consts, scratch=()):
    s = x.shape[0]
    tm = ROW_TILE
    row = lambda c, n=s: pl.BlockSpec((tm * n // s, c), lambda i: (i, 0))
    col = lambda r: pl.BlockSpec((r, tm), lambda i: (0, i))
    return pl.pallas_call(
        kernel,
        grid=(s // tm,),
        in_specs=([row(D_MODEL)] + [row(a.shape[1], a.shape[0]) for a in acts] + [col(a.shape[0]) for a in acts_t]
                  + [_const_spec(c.shape) for c in consts]),
        out_specs=row(D_MODEL),
        out_shape=jax.ShapeDtypeStruct((s, D_MODEL), F32),
        scratch_shapes=[pltpu.VMEM((SUBLANES, 2 * D_FF), F32), *scratch],
        compiler_params=_params(("arbitrary",)),
        name=name,
    )(x, *acts, *acts_t, *consts)


def _proj_odd_kernel(x_ref, g_ref, win_ref, *refs):
    out_refs, scr = refs[:-1], refs[-1]
    h = _rms(x_ref[...], g_ref[...]).astype(BF16)
    proj = _dot(h, win_ref[...])
    c = DIL_HEADS_PER_GROUP * HEAD_DIM
    scale = HEAD_DIM ** -0.5
    for n, ref in enumerate(out_refs[:-1]):
        blk = proj[:, n * c:(n + 1) * c]
        if n < 3 or n == 9:
            blk = blk * scale
        if n < 9:
            blk = _rows_to_classes(blk, DIL_CONFIGS[n % 3][1], scr)
        ref[...] = blk.astype(BF16)
    _store_transposed_tiles(out_refs[-1], proj[:, 11 * c:12 * c], SB_HEADS, HEAD_DIM, 0)


def _proj_odd(x, g, win):
    s = x.shape[0]
    tm, tk = ROW_TILE, ATT_TK
    c = DIL_HEADS_PER_GROUP * HEAD_DIM
    row = lambda w, d=1: pl.BlockSpec((tm // d, d * w), lambda i: (i, 0))
    return pl.pallas_call(
        _proj_odd_kernel,
        grid=(s // tm,),
        in_specs=[row(D_MODEL), _const_spec(g.shape), _const_spec(win.shape)],
        out_specs=([row(c, d) for _ in range(3) for _, d in DIL_CONFIGS] + [row(c)] * 2
                   + [pl.BlockSpec((tm // tk, c, tk), lambda i: (i, 0, 0))]),
        out_shape=([jax.ShapeDtypeStruct((s // d, d * c), BF16) for _ in range(3) for _, d in DIL_CONFIGS]
                   + [jax.ShapeDtypeStruct((s, c), BF16)] * 2 + [jax.ShapeDtypeStruct((s // tk, c, tk), BF16)]),
        scratch_shapes=[pltpu.VMEM((c // LANES, tm, LANES), F32)],
        compiler_params=_params(("parallel",)),
        name="proj_odd",
    )(x, g, win)


def _dilated_kernel(bias_ref, q_ref, kp_ref, kc_ref, vp_ref, vc_ref, o_ref, lse_ref, *, nub):
    t, tb = DIL_T, DIL_TB
    ub = pl.program_id(0) % nub
    c = DIL_HEADS_PER_GROUP * HEAD_DIM
    qf = q_ref[...].astype(F32)
    kcat = jnp.concatenate([kp_ref[...], kc_ref[...]], axis=0)
    vcat = jnp.concatenate([vp_ref[...], vc_ref[...]], axis=0)
    lane = lax.broadcasted_iota(jnp.int32, (tb, c), 1)
    col = lax.broadcasted_iota(jnp.int32, (tb, t + tb), 1)
    no_prev = col < jnp.where(ub == 0, t, 0)
    out = jnp.zeros((tb, c), F32)
    lse = jnp.zeros((tb, c), F32)
    for hh in range(DIL_HEADS_PER_GROUP):
        mine = jnp.logical_and(lane >= hh * HEAD_DIM, lane < (hh + 1) * HEAD_DIM)
        qh = jnp.where(mine, qf, 0.0).astype(BF16)
        s = _dot_t(qh, kcat) + bias_ref[hh]
        s = jnp.where(no_prev, NEG, s)
        m = jnp.max(s, axis=1, keepdims=True)
        e = jnp.exp(s - m)
        den = jnp.sum(e, axis=1, keepdims=True)
        oh = _dot(e.astype(BF16), vcat) / den
        out = jnp.where(mine, oh, out)
        lse = jnp.where(mine, m + jnp.log(den), lse)
    o_ref[...] = out
    lse_ref[...] = lse


def _dilated_group(q, k, v, dil, slopes):
    c = q.shape[1] // dil
    s = q.shape[0] * dil
    t, tb = DIL_T, DIL_TB
    nub = s // dil // tb
    a = np.arange(tb)[:, None]
    cc = np.arange(t + tb)[None, :]
    steps = t + a - cc
    ok = (steps >= 0) & (steps <= t)
    dist = (steps * dil).astype(np.float32)
    bias = np.where(ok[None], -np.asarray(slopes, np.float32)[:, None, None] * dist[None], np.float32(NEG))
    bias = jnp.asarray(bias, F32)
    cur = pl.BlockSpec((tb, c), lambda b: (b % nub, b // nub))
    prev = pl.BlockSpec((t, c), lambda b: (jnp.maximum(b % nub * (tb // t) - 1, 0), b // nub))
    o, lse = pl.pallas_call(
        functools.partial(_dilated_kernel, nub=nub),
        grid=(dil * nub,),
        in_specs=[_const_spec(bias.shape), cur, prev, cur, prev, cur],
        out_specs=[cur, cur],
        out_shape=[jax.ShapeDtypeStruct((s // dil, dil * c), F32)] * 2,
        compiler_params=_params(("parallel",)),
        name=f"dilated_d{dil}",
    )(bias, q, k, k, v, v)
    return o, lse


def _sb_kernel(tri_ref, q_ref, k_ref, vt_ref, o_ref):
    hh = pl.program_id(0)
    i = pl.program_id(1)
    tq, tk = SB_TQ, ATT_TK
    ratio = tq // tk
    c = SB_HEADS * HEAD_DIM
    tri = tri_ref[...]
    lane = lax.broadcasted_iota(jnp.int32, (tq, c), 1)
    mine = jnp.logical_and(lane >= hh * HEAD_DIM, lane < (hh + 1) * HEAD_DIM)
    qh = jnp.where(mine, q_ref[...].astype(F32), 0.0).astype(BF16)
    krow = lax.broadcasted_iota(jnp.int32, (tk, tq), 0)
    qcol = lax.broadcasted_iota(jnp.int32, (tk, tq), 1)

    def step(j, carry, diag):
        r, acc = carry
        z = _dot_t(_k_tile(k_ref, j), qh)
        sp = jnp.log1p(jnp.exp(-jnp.abs(z)))
        log_beta = jnp.minimum(z, 0.0) - sp
        log_keep = log_beta - z
        if diag is not None:
            strict = krow + diag * tk < qcol
            log_keep = jnp.where(strict, log_keep, 0.0)
        hi = log_keep.astype(BF16)
        lo = (log_keep - hi.astype(F32)).astype(BF16)
        aft = _dot(tri, jnp.concatenate([hi, lo], axis=0))
        a = jnp.exp(log_beta + aft[0:tk] + r)
        if diag is not None:
            a = jnp.where(strict, a, 0.0)
        acc = acc + _dot(vt_ref[j], a.astype(BF16))
        return r + aft[tk:tk + 1], acc

    carry = (jnp.zeros((1, tq), F32), jnp.zeros((HEAD_DIM, tq), F32))
    n_full = i * ratio
    carry = lax.fori_loop(0, ratio, lambda dd, cr: step(n_full + ratio - 1 - dd, cr, ratio - 1 - dd), carry)

    def more(state):
        n, r, _ = state
        return jnp.logical_and(n < n_full, jnp.max(r) > SB_STOP)

    def walk(state):
        n, r, acc = state
        r, acc = step(n_full - 1 - n, (r, acc), None)
        return n + 1, r, acc

    o_ref[...] = lax.while_loop(more, walk, (jnp.int32(0),) + carry)[2]


def _stickbreak(q, k, vt):
    s, c = q.shape
    tq, tk = SB_TQ, ATT_TK
    j = np.arange(tk)
    later = (j[None, :] > j[:, None]).astype(np.float32)
    tri = np.concatenate([np.concatenate([later, later], axis=1), np.ones((BF16_ROWS, 2 * tk), np.float32)], axis=0)
    tri = jnp.asarray(tri, BF16)
    return pl.pallas_call(
        _sb_kernel,
        grid=(SB_HEADS, s // tq),
        in_specs=[_const_spec(tri.shape),
                  pl.BlockSpec((tq, c), lambda h, i: (i, 0)),
                  _const_spec((s, c)),
                  pl.BlockSpec((s // tk, HEAD_DIM, tk), lambda h, i: (0, h, 0))],
        out_specs=pl.BlockSpec((HEAD_DIM, tq), lambda h, i: (h, i)),
        out_shape=jax.ShapeDtypeStruct((c, s), F32),
        compiler_params=_params(("parallel", "parallel")),
        name="stickbreak",
    )(tri, q, k, vt)


def _alibi_slopes(n):
    return 2.0 ** (-8.0 * np.arange(1, n + 1) / n)


def _pad_cols(w, lo, width):
    return jnp.pad(w, ((0, 0), (lo, width - lo - w.shape[1])))


def _rot_half_cols(w):
    half = w.shape[1] // 2
    return jnp.concatenate([-w[:, half:], w[:, :half]], axis=1)


def _even_weights(w_in, w_uq, w_ukv):
    o_kr = MLA_Q_RANK + MLA_KV_RANK
    w_kr = w_in[:, o_kr:o_kr + MLA_ROPE]
    win = jnp.concatenate([w_in[:, :o_kr], _pad_cols(w_kr, MLA_NOPE, LANES),
                           _pad_cols(_rot_half_cols(w_kr), MLA_NOPE, LANES),
                           w_in[:, o_kr + MLA_ROPE:]], axis=1)
    hd = MLA_NOPE + MLA_ROPE
    q_plain, q_rot = [], []
    for hh in range(MLA_HEADS):
        wh = w_uq[:, hh * hd:(hh + 1) * hd]
        q_plain.append(_pad_cols(wh, 0, LANES))
        q_rot.append(_pad_cols(_rot_half_cols(wh[:, MLA_NOPE:]), MLA_NOPE, LANES))
    wuq = jnp.concatenate(q_plain + q_rot, axis=1)
    hk = MLA_NOPE + MLA_V
    k_cols = [_pad_cols(w_ukv[:, hh * hk:hh * hk + MLA_NOPE], 0, LANES) for hh in range(MLA_HEADS)]
    v_cols = [w_ukv[:, hh * hk + MLA_NOPE:(hh + 1) * hk] for hh in range(MLA_HEADS)]
    wkv = jnp.concatenate(k_cols + v_cols, axis=1)
    return win.astype(BF16), wuq.astype(BF16), wkv.astype(BF16)


def _rope_tables(s):
    half = MLA_ROPE // 2
    inv = ROPE_BASE ** (-jnp.arange(half, dtype=F32) / half)
    ang = jnp.arange(s).astype(F32)[:, None] * inv
    cos2 = jnp.tile(jnp.cos(ang), (1, 2))
    sin2 = jnp.tile(jnp.sin(ang), (1, 2))
    scale = (MLA_NOPE + MLA_ROPE) ** -0.5 * LOG2E
    tail = LANES - MLA_NOPE - MLA_ROPE
    ones, zeros, ztail = jnp.ones((s, MLA_NOPE), F32), jnp.zeros((s, MLA_NOPE), F32), jnp.zeros((s, tail), F32)
    cosq = jnp.concatenate([ones, cos2, ztail], axis=1) * scale
    sinq = jnp.concatenate([zeros, sin2, ztail], axis=1) * scale
    cosk = jnp.concatenate([zeros, cos2, ztail], axis=1)
    sink = jnp.concatenate([zeros, sin2, ztail], axis=1)
    return cosq, sinq, cosk, sink


def _row2d(v):
    return v.reshape(1, -1).astype(F32)


def kernel(x, ev_pre_g, ev_w_in, ev_cq_g, ev_w_uq, ev_ckv_g, ev_w_ukv, ev_lam_q1, ev_lam_k1, ev_lam_q2,
           ev_lam_k2, ev_subln_g, ev_w_out, ev_post_g, od_pre_g, od_w_in, od_w_out, od_post_g, ffn_pre_g,
           ffn_w_up, ffn_conv_w, ffn_conv_b, ffn_w_down, ffn_post_g):
    b, s, _ = x.shape
    assert b == 1 and s % max(ROW_TILE, ATT_TQ, SB_TQ, DIL_TB * DIL_CONFIGS[-1][1]) == 0
    assert ROW_TILE % ATT_TK == 0 and ATT_TQ % ATT_TK == 0 and SB_TQ % ATT_TK == 0
    assert (ATT_TQ // ATT_TK) % MAIN_UNROLL == 0 and DIL_TB % DIL_T == 0
    assert sum(FF_CHUNKS) == D_FF
    assert ATT_TK <= 256
    xs = x[0]

    def ffn_consts(i):
        return [_row2d(ffn_pre_g[i]), ffn_w_up[i].astype(BF16), ffn_conv_w[i].astype(F32),
                _row2d(ffn_conv_b[i]), ffn_w_down[i].astype(BF16), _row2d(ffn_post_g[i])]

    win, wuq, wkv = _even_weights(ev_w_in[0], ev_w_uq[0], ev_w_ukv[0])
    cosq, sinq, cosk, sink = _rope_tables(s)
    pos_col = (jnp.arange(ROW_TILE) % ATT_TK).astype(BF16)[:, None]
    pos_tile = jnp.pad(jnp.tile(pos_col, (1, SLOPE_PARTS)), ((0, 0), (0, LANES - SLOPE_PARTS)))
    qm, km, vmt, dq, dk, dvt = _proj_even(xs, _row2d(ev_pre_g[0]), win, _row2d(ev_cq_g[0]), wuq,
                                          _row2d(ev_ckv_g[0]), wkv, cosq, sinq, cosk, sink, pos_tile)
    o_mla = _mla_attention(qm, km, vmt)
    lam_p = jnp.pad(jnp.stack([ev_lam_q1[0], ev_lam_k1[0], ev_lam_q2[0], ev_lam_k2[0]]).astype(F32),
                    ((0, SUBLANES - 4), (0, LANES - DIFF_QK)))
    d_slopes = _alibi_slopes(DIFF_HEADS) * LOG2E
    slope_arr = jnp.asarray(np.broadcast_to(d_slopes[:, None, None], (DIFF_HEADS, SUBLANES, LANES)), F32)
    qfeat = jnp.zeros((DIFF_HEADS, ATT_TQ, LANES), F32)
    rest = jnp.asarray(d_slopes, F32)
    for part in range(SLOPE_PARTS):
        piece = rest.astype(BF16).astype(F32)
        qfeat = qfeat.at[:, :, part].set(piece[:, None])
        rest = rest - piece
    lam_init = 0.8 - 0.6 * math.exp(-0.3 * 0)
    o_diff = _diff_attention(lam_p, slope_arr, qfeat.astype(BF16), _row2d(ev_subln_g[0]),
                             dq, dk, dvt, lam_init)
    x2 = _out_ffn(_out_ffn_even_kernel, "out_ffn_even", xs, [o_mla, o_diff], [],
                  [ev_w_out[0].astype(BF16), _row2d(ev_post_g[0])] + ffn_consts(0))

    p = _proj_odd(x2, _row2d(od_pre_g[0]), od_w_in[0].astype(BF16))
    slopes = _alibi_slopes(len(DIL_CONFIGS) * DIL_HEADS_PER_GROUP).reshape(len(DIL_CONFIGS), -1)
    dil_outs = []
    for gi, (_, dil) in enumerate(DIL_CONFIGS):
        dil_outs += list(_dilated_group(p[gi], p[3 + gi], p[6 + gi], dil, slopes[gi]))
    o_sbt = _stickbreak(p[9], p[10], p[11])
    lane_tiles = DIL_HEADS_PER_GROUP * HEAD_DIM // LANES
    x4 = _out_ffn(_out_ffn_odd_kernel, "out_ffn_odd", x2, dil_outs, [o_sbt],
                  [od_w_out[0].astype(BF16), _row2d(od_post_g[0])] + ffn_consts(1),
                  scratch=[pltpu.VMEM((lane_tiles, ROW_TILE, LANES), F32)])
    return x4[None]
```

```python
import functools
import math

import numpy as np
import jax
import jax.numpy as jnp
from jax import lax
from jax.experimental import pallas as pl
from jax.experimental.pallas import tpu as pltpu

F32 = jnp.float32
BF16 = jnp.bfloat16

D_MODEL = 1024
NORM_EPS = 1e-6
MLA_HEADS = 8
MLA_Q_RANK = 256
MLA_KV_RANK = 128
MLA_NOPE = 64
MLA_ROPE = 32
MLA_V = 64
ROPE_BASE = 10000.0
DIFF_HEADS = 4
DIFF_QK = 64
DIFF_V = 128
DIL_CONFIGS = ((128, 1), (512, 4), (2048, 16))
DIL_HEADS_PER_GROUP = 4
HEAD_DIM = 64
SB_HEADS = 4
D_FF = 2816
CONV_WIDTH = 3

LANES = 128
SUBLANES = 8
BF16_ROWS = 16
VMEM_LIMIT_BYTES = 56 * 1024 * 1024
ROW_TILE = 512
ATT_TQ = 2048
MAIN_UNROLL = 8
SB_TQ = 512
SB_GROUP = 4
ATT_TK = 256
DIL_T = 128
DIL_TB = 512
FF_CHUNKS = (512,) * 5 + (256,)
NEG = -1e30
LOG2E = 1.4426950408889634
MAX_JUMP = 32.0
SB_STOP = -110.0
SLOPE_PARTS = 3

_TRANS_B = (((1,), (1,)), ((), ()))


def _dot(a, b):
    return jnp.dot(a, b, preferred_element_type=F32)


def _dot_t(a, b):
    return lax.dot_general(a, b, _TRANS_B, preferred_element_type=F32)


def _rms(xf, g):
    ms = jnp.mean(xf * xf, axis=-1, keepdims=True)
    return xf * lax.rsqrt(ms + NORM_EPS) * g


def _params(sem):
    return pltpu.CompilerParams(dimension_semantics=sem, vmem_limit_bytes=VMEM_LIMIT_BYTES)


def _const_spec(shape):
    nd = len(shape)
    return pl.BlockSpec(shape, lambda *_: (0,) * nd, pipeline_mode=pl.Buffered(1))


def _store_transposed_tiles(dst_ref, src, heads, width, ones_rows):
    tk = ATT_TK
    per = width + ones_rows
    for b in range(src.shape[0] // tk):
        vt = src[b * tk:(b + 1) * tk, :].T.astype(BF16)
        for hh in range(heads):
            dst_ref[b, hh * per:hh * per + width, :] = vt[hh * width:(hh + 1) * width, :]
            if ones_rows:
                dst_ref[b, hh * per + width:(hh + 1) * per, :] = jnp.ones((ones_rows, tk), BF16)


def _rows_to_classes(x, dil, scr):
    if dil == 1:
        return x
    tm, c = x.shape
    for h in range(c // LANES):
        scr[h] = x[:, h * LANES:(h + 1) * LANES]
    return jnp.concatenate([scr[h, pl.ds(r, tm // dil, stride=dil), :]
                            for r in range(dil) for h in range(c // LANES)], axis=1)


def _classes_to_rows(ref, dil, scr):
    if dil == 1:
        return ref[...]
    n, c = ref.shape[0], ref.shape[1] // dil
    for r in range(dil):
        for h in range(c // LANES):
            scr[h, pl.ds(r, n, stride=dil), :] = ref[:, r * c + h * LANES:r * c + (h + 1) * LANES]
    return jnp.concatenate([scr[h] for h in range(c // LANES)], axis=1)


def _proj_even_kernel(x_ref, g_ref, win_ref, cqg_ref, wuq_ref, ckvg_ref, wkv_ref,
                      cosq_ref, sinq_ref, cosk_ref, sink_ref, pos_ref,
                      qm_ref, km_ref, vmt_ref, dq_ref, dk_ref, dvt_ref):
    h = _rms(x_ref[...], g_ref[...]).astype(BF16)
    proj = _dot(h, win_ref[...])
    nq = MLA_HEADS * LANES
    cq = _rms(proj[:, 0:MLA_Q_RANK], cqg_ref[...]).astype(BF16)
    qq = _dot(cq, wuq_ref[...])
    cosq, sinq = cosq_ref[...], sinq_ref[...]
    for hh in range(MLA_HEADS):
        a = qq[:, hh * LANES:(hh + 1) * LANES]
        b = qq[:, nq + hh * LANES:nq + (hh + 1) * LANES]
        qm_ref[:, hh * LANES:(hh + 1) * LANES] = (a * cosq + b * sinq).astype(BF16)
    o = MLA_Q_RANK
    ckv = _rms(proj[:, o:o + MLA_KV_RANK], ckvg_ref[...]).astype(BF16)
    kv = _dot(ckv, wkv_ref[...])
    o += MLA_KV_RANK
    krc = proj[:, o:o + LANES] * cosk_ref[...] + proj[:, o + LANES:o + 2 * LANES] * sink_ref[...]
    for hh in range(MLA_HEADS):
        km_ref[:, hh * LANES:(hh + 1) * LANES] = (kv[:, hh * LANES:(hh + 1) * LANES] + krc).astype(BF16)
    _store_transposed_tiles(vmt_ref, kv[:, nq:nq + MLA_HEADS * MLA_V], MLA_HEADS, MLA_V, BF16_ROWS)
    o += 2 * LANES
    nd = DIFF_HEADS * 2 * DIFF_QK
    dq_ref[...] = (proj[:, o:o + nd] * (DIFF_QK ** -0.5 * LOG2E)).astype(BF16)
    o += nd
    pos = pos_ref[...]
    for hh in range(DIFF_HEADS):
        dk_ref[:, hh * 2 * LANES:hh * 2 * LANES + LANES] = proj[:, o + hh * LANES:o + (hh + 1) * LANES].astype(BF16)
        dk_ref[:, hh * 2 * LANES + LANES:(hh + 1) * 2 * LANES] = pos
    o += nd
    _store_transposed_tiles(dvt_ref, proj[:, o:o + DIFF_HEADS * DIFF_V], DIFF_HEADS, DIFF_V, BF16_ROWS)


def _proj_even(x, g, win, cqg, wuq, ckvg, wkv, cosq, sinq, cosk, sink, pos):
    s = x.shape[0]
    tm, tk = ROW_TILE, ATT_TK
    row = lambda c: pl.BlockSpec((tm, c), lambda i: (i, 0))
    tile_t = lambda r: pl.BlockSpec((tm // tk, r, tk), lambda i: (i, 0, 0))
    mla_rows = MLA_HEADS * (MLA_V + BF16_ROWS)
    diff_rows = DIFF_HEADS * (DIFF_V + BF16_ROWS)
    flat = lambda c: jax.ShapeDtypeStruct((s, c), BF16)
    tiled = lambda r: jax.ShapeDtypeStruct((s // tk, r, tk), BF16)
    return pl.pallas_call(
        _proj_even_kernel,
        grid=(s // tm,),
        in_specs=[row(D_MODEL), _const_spec(g.shape), _const_spec(win.shape), _const_spec(cqg.shape),
                  _const_spec(wuq.shape), _const_spec(ckvg.shape), _const_spec(wkv.shape),
                  row(LANES), row(LANES), row(LANES), row(LANES), _const_spec(pos.shape)],
        out_specs=[row(MLA_HEADS * LANES), row(MLA_HEADS * LANES), tile_t(mla_rows),
                   row(DIFF_HEADS * 2 * DIFF_QK), row(DIFF_HEADS * 2 * LANES), tile_t(diff_rows)],
        out_shape=[flat(MLA_HEADS * LANES), flat(MLA_HEADS * LANES), tiled(mla_rows),
                   flat(DIFF_HEADS * 2 * DIFF_QK), flat(DIFF_HEADS * 2 * LANES), tiled(diff_rows)],
        compiler_params=_params(("parallel",)),
        name="proj_even",
    )(x, g, win, cqg, wuq, ckvg, wkv, cosq, sinq, cosk, sink, pos)


def _flash_cols(chains, i, rows, tile_bias=None, scratch=None):
    tq, tk = ATT_TQ, ATT_TK
    ratio = tq // tk
    n_full = i * ratio

    def safe_step(j, carry):
        out = []
        for (qh, load_k, load_vt), (m, acc) in zip(chains, carry):
            s = _dot_t(load_k(j), qh)
            mb = jnp.max(s, axis=0, keepdims=True)
            if tile_bias is not None:
                c = tile_bias(j)
                mb = mb + c
            m_new = jnp.maximum(m, mb)
            alpha = jnp.exp2(m - m_new)
            shift = m_new if tile_bias is None else m_new - c
            p = jnp.exp2(s - shift).astype(BF16)
            out.append((m_new, alpha * acc + _dot(load_vt(j), p)))
        return tuple(out)

    def fast_step(n, carry):
        j = n_full - 1 - n
        out = []
        for ci, ((qh, load_k, load_vt), (m, alpha, acc, jump)) in enumerate(zip(chains, carry)):
            acc = (acc + _dot(load_vt(j + 1), scratch[ci])) * alpha
            s = _dot_t(load_k(j), qh)
            c = None if tile_bias is None else tile_bias(j)
            shift = m if c is None else m - c
            scratch[ci] = jnp.exp2(s - shift).astype(BF16)
            mb = jnp.max(s, axis=0, keepdims=True)
            if c is not None:
                mb = mb + c
            m_new = jnp.maximum(m, mb)
            out.append((m_new, jnp.exp2(m - m_new), acc, jnp.maximum(jump, mb - m)))
        return tuple(out)

    def fast_group(g, c):
        for t in range(MAIN_UNROLL):
            c = fast_step(g * MAIN_UNROLL + t, c)
        return c

    def diag_step(d, state):
        tri = lax.broadcasted_iota(jnp.int32, (tk, tk), 0) <= lax.broadcasted_iota(jnp.int32, (tk, tk), 1)
        out = []
        for (qh, load_k, load_vt), (ms, accs) in zip(chains, state):
            s = _dot_t(load_k(n_full + d), qh[d * tk:, :])
            vt = load_vt(n_full + d)
            c = None if tile_bias is None else tile_bias(n_full + d)
            ms, accs = list(ms), list(accs)
            for blk in range(d, ratio):
                sb = s[:, (blk - d) * tk:(blk - d + 1) * tk]
                if blk == d:
                    sb = jnp.where(tri, sb, NEG)
                mb = jnp.max(sb, axis=0, keepdims=True)
                if c is not None:
                    mb = mb + c
                m_new = jnp.maximum(ms[blk], mb)
                shift = m_new if c is None else m_new - c
                p = jnp.exp2(sb - shift).astype(BF16)
                accs[blk] = jnp.exp2(ms[blk] - m_new) * accs[blk] + _dot(vt, p)
                ms[blk] = m_new
            out.append((ms, accs))
        return out

    state = [([jnp.full((1, tk), NEG, F32)] * ratio, [jnp.zeros((rows, tk), F32)] * ratio) for _ in chains]
    for d in range(ratio):
        state = diag_step(d, state)
    diag = tuple((jnp.concatenate(ms, axis=1), jnp.concatenate(accs, axis=1)) for ms, accs in state)
    for ci in range(len(chains)):
        scratch[ci] = jnp.zeros((tk, tq), BF16)
    one, low = jnp.ones((1, tq), F32), jnp.full((1, tq), NEG, F32)
    fast = lax.fori_loop(0, n_full // MAIN_UNROLL, fast_group, tuple((m, one, acc, low) for m, acc in diag))
    accs = [(acc + _dot(load_vt(0), scratch[ci])) * alpha
            for ci, ((_, _, load_vt), (_, alpha, acc, _)) in enumerate(zip(chains, fast))]
    worst = functools.reduce(jnp.maximum, [jnp.max(jump) for _, _, _, jump in fast])

    def redo():
        carry = lax.fori_loop(0, n_full, safe_step, diag)
        return [acc for _, acc in carry]

    return lax.cond(worst > MAX_JUMP, redo, lambda: accs)


def _k_tile(k_ref, j, lanes=slice(None)):
    return k_ref[pl.ds(pl.multiple_of(j * ATT_TK, ATT_TK), ATT_TK), lanes]


def _mla_kernel(q_ref, k_ref, vt_ref, o_ref, p_ref):
    i = pl.program_id(1)
    rows = MLA_V + BF16_ROWS
    def chain(hh):
        lanes = slice(hh * LANES, (hh + 1) * LANES)
        return (q_ref[:, lanes], lambda j: _k_tile(k_ref, j, lanes),
                lambda j: vt_ref[j, hh * rows:(hh + 1) * rows, :])

    accs = _flash_cols([chain(0), chain(1)], i, rows, scratch=p_ref)
    halves = [acc[0:MLA_V] / acc[MLA_V:MLA_V + 1] for acc in accs]
    o_ref[...] = jnp.concatenate(halves, axis=0).T.astype(BF16)


def _mla_attention(qm, km, vmt):
    s = qm.shape[0]
    tq = ATT_TQ
    rows = 2 * (MLA_V + BF16_ROWS)
    return pl.pallas_call(
        _mla_kernel,
        grid=(MLA_HEADS // 2, s // tq),
        in_specs=[pl.BlockSpec((tq, 2 * LANES), lambda p, i: (i, p)),
                  pl.BlockSpec((s, 2 * LANES), lambda p, i: (0, p)),
                  pl.BlockSpec((vmt.shape[0], rows, ATT_TK), lambda p, i: (0, p, 0))],
        out_specs=pl.BlockSpec((tq, LANES), lambda p, i: (i, p)),
        out_shape=jax.ShapeDtypeStruct((s, MLA_HEADS * MLA_V), BF16),
        scratch_shapes=[pltpu.VMEM((2, ATT_TK, tq), BF16)],
        compiler_params=_params(("parallel", "parallel")),
        name="mla_attn",
    )(qm, km, vmt)


def _diff_kernel(lam_ref, slope_ref, qf_ref, g_ref, q_ref, k_ref, vt_ref, o_ref, p_ref, *, lam_init):
    i = pl.program_id(1)
    tq = ATT_TQ
    lane = lax.broadcasted_iota(jnp.int32, (tq, LANES), 1)
    qf32 = q_ref[...].astype(F32)
    feat = qf_ref[...]
    zero = jnp.zeros_like(qf32)
    qa = jnp.concatenate([jnp.where(lane < DIFF_QK, qf32, zero).astype(BF16), feat], axis=1)
    qb = jnp.concatenate([jnp.where(lane >= DIFF_QK, qf32, zero).astype(BF16), feat], axis=1)
    slope = slope_ref[0:1, 0:1]
    rows = DIFF_V + BF16_ROWS
    tile_bias = lambda j: slope * (j * ATT_TK - i * tq).astype(F32)
    load_k = lambda j: _k_tile(k_ref, j)
    load_vt = lambda j: vt_ref[j]
    acc1, acc2 = _flash_cols([(qa, load_k, load_vt), (qb, load_k, load_vt)], i, rows, tile_bias, p_ref)
    lp = lam_ref[...]
    s1 = jnp.sum(lp[0:1, :] * lp[1:2, :], axis=1, keepdims=True)
    s2 = jnp.sum(lp[2:3, :] * lp[3:4, :], axis=1, keepdims=True)
    lam = jnp.exp(s1) - jnp.exp(s2) + lam_init
    ot = acc1[0:DIFF_V] / acc1[DIFF_V:DIFF_V + 1] - lam * (acc2[0:DIFF_V] / acc2[DIFF_V:DIFF_V + 1])
    o_ref[...] = (_rms(ot.T, g_ref[...]) * (1.0 - lam_init)).astype(BF16)


def _diff_attention(lam_p, slope_arr, qfeat, subln_g, dq, dk, dvt, lam_init):
    s = dq.shape[0]
    tq = ATT_TQ
    return pl.pallas_call(
        functools.partial(_diff_kernel, lam_init=lam_init),
        grid=(DIFF_HEADS, s // tq),
        in_specs=[pl.BlockSpec(lam_p.shape, lambda h, i: (0, 0)),
                  pl.BlockSpec((None, SUBLANES, LANES), lambda h, i: (h, 0, 0)),
                  pl.BlockSpec((None, tq, LANES), lambda h, i: (h, 0, 0)),
                  pl.BlockSpec(subln_g.shape, lambda h, i: (0, 0)),
                  pl.BlockSpec((tq, LANES), lambda h, i: (i, h)),
                  pl.BlockSpec((s, 2 * LANES), lambda h, i: (0, h)),
                  pl.BlockSpec((dvt.shape[0], DIFF_V + BF16_ROWS, ATT_TK), lambda h, i: (0, h, 0))],
        out_specs=pl.BlockSpec((tq, LANES), lambda h, i: (i, h)),
        out_shape=jax.ShapeDtypeStruct((s, DIFF_HEADS * DIFF_V), BF16),
        scratch_shapes=[pltpu.VMEM((2, ATT_TK, tq), BF16)],
        compiler_params=_params(("parallel", "parallel")),
        name="diff_attn",
    )(lam_p, slope_arr, qfeat, subln_g, dq, dk, dvt)


def _shift_rows(u, k, prev):
    top = jnp.where(lax.broadcasted_iota(jnp.int32, prev.shape, 0) < k,
                    pltpu.roll(prev, k, 0), pltpu.roll(u[0:SUBLANES, :], k, 0))
    return jnp.concatenate([top, pltpu.roll(u, k, 0)[SUBLANES:, :]], axis=0)


def _ffn_tile(x1, fpre, wup_ref, cw_ref, cb_ref, wdn_ref, fpost, carry_ref):
    tm = x1.shape[0]
    hf = _rms(x1, fpre).astype(BF16)
    acts = []
    lo = 0
    for width in FF_CHUNKS:
        ys = []
        for part in range(2):
            c0 = part * D_FF + lo
            u = _dot(hf, wup_ref[:, c0:c0 + width])
            prev = carry_ref[:, c0:c0 + width]
            carry_ref[:, c0:c0 + width] = u[tm - SUBLANES:, :]
            ys.append(cw_ref[2:3, c0:c0 + width] * u
                      + cw_ref[1:2, c0:c0 + width] * _shift_rows(u, 1, prev)
                      + cw_ref[0:1, c0:c0 + width] * _shift_rows(u, 2, prev)
                      + cb_ref[:, c0:c0 + width])
        gate, up = ys
        acts.append((gate * (1.0 / (1.0 + jnp.exp(-gate))) * up).astype(BF16))
        lo += width
    acc = _dot(jnp.concatenate(acts, axis=1), wdn_ref[...])
    return x1 + _rms(acc, fpost)


def _out_ffn_even_kernel(x_ref, a_ref, b_ref, wout_ref, postg_ref, fpre_ref, wup_ref, cw_ref, cb_ref,
                         wdn_ref, fpost_ref, o_ref, carry_ref):
    @pl.when(pl.program_id(0) == 0)
    def _():
        carry_ref[...] = jnp.zeros_like(carry_ref)

    na = a_ref.shape[1]
    mix = _dot(a_ref[...], wout_ref[0:na, :]) + _dot(b_ref[...], wout_ref[na:, :])
    x1 = x_ref[...] + _rms(mix, postg_ref[...])
    o_ref[...] = _ffn_tile(x1, fpre_ref[...], wup_ref, cw_ref, cb_ref, wdn_ref, fpost_ref[...], carry_ref)


def _out_ffn_odd_kernel(x_ref, o0_ref, l0_ref, o1_ref, l1_ref, o2_ref, l2_ref, bt_ref, wout_ref, postg_ref,
                        fpre_ref, wup_ref, cw_ref, cb_ref, wdn_ref, fpost_ref, o_ref, carry_ref, scr):
    @pl.when(pl.program_id(0) == 0)
    def _():
        carry_ref[...] = jnp.zeros_like(carry_ref)

    (l0, o0), (l1, o1), (l2, o2) = [(_classes_to_rows(l_ref, d, scr), _classes_to_rows(og_ref, d, scr))
                                    for (l_ref, og_ref), (_, d) in zip(((l0_ref, o0_ref), (l1_ref, o1_ref),
                                                                        (l2_ref, o2_ref)), DIL_CONFIGS)]
    m = jnp.maximum(jnp.maximum(l0, l1), l2)
    e0, e1, e2 = jnp.exp(l0 - m), jnp.exp(l1 - m), jnp.exp(l2 - m)
    dil = (e0 * o0 + e1 * o1 + e2 * o2) / (e0 + e1 + e2)
    na = dil.shape[1]
    sb = bt_ref[...].T
    mix = _dot(dil.astype(BF16), wout_ref[0:na, :]) + _dot(sb.astype(BF16), wout_ref[na:, :])
    x1 = x_ref[...] + _rms(mix, postg_ref[...])
    o_ref[...] = _ffn_tile(x1, fpre_ref[...], wup_ref, cw_ref, cb_ref, wdn_ref, fpost_ref[...], carry_ref)


def _out_ffn(kernel, name, x, acts, acts_t, consts, scratch=()):
    s = x.shape[0]
    tm = ROW_TILE
    row = lambda c, n=s: pl.BlockSpec((tm * n // s, c), lambda i: (i, 0))
    col = lambda r: pl.BlockSpec((r, tm), lambda i: (0, i))
    return pl.pallas_call(
        kernel,
        grid=(s // tm,),
        in_specs=([row(D_MODEL)] + [row(a.shape[1], a.shape[0]) for a in acts] + [col(a.shape[0]) for a in acts_t]
                  + [_const_spec(c.shape) for c in consts]),
        out_specs=row(D_MODEL),
        out_shape=jax.ShapeDtypeStruct((s, D_MODEL), F32),
        scratch_shapes=[pltpu.VMEM((SUBLANES, 2 * D_FF), F32), *scratch],
        compiler_params=_params(("arbitrary",)),
        name=name,
    )(x, *acts, *acts_t, *consts)


def _proj_odd_kernel(x_ref, g_ref, win_ref, *refs):
    out_refs, scr = refs[:-1], refs[-1]
    h = _rms(x_ref[...], g_ref[...]).astype(BF16)
    proj = _dot(h, win_ref[...])
    c = DIL_HEADS_PER_GROUP * HEAD_DIM
    scale = HEAD_DIM ** -0.5
    for n, ref in enumerate(out_refs[:-1]):
        blk = proj[:, n * c:(n + 1) * c]
        if n < 3 or n == 9:
            blk = blk * scale
        if n < 9:
            blk = _rows_to_classes(blk, DIL_CONFIGS[n % 3][1], scr)
        ref[...] = blk.astype(BF16)
    _store_transposed_tiles(out_refs[-1], proj[:, 11 * c:12 * c], SB_HEADS, HEAD_DIM, 0)


def _proj_odd(x, g, win):
    s = x.shape[0]
    tm, tk = ROW_TILE, ATT_TK
    c = DIL_HEADS_PER_GROUP * HEAD_DIM
    row = lambda w, d=1: pl.BlockSpec((tm // d, d * w), lambda i: (i, 0))
    return pl.pallas_call(
        _proj_odd_kernel,
        grid=(s // tm,),
        in_specs=[row(D_MODEL), _const_spec(g.shape), _const_spec(win.shape)],
        out_specs=([row(c, d) for _ in range(3) for _, d in DIL_CONFIGS] + [row(c)] * 2
                   + [pl.BlockSpec((tm // tk, c, tk), lambda i: (i, 0, 0))]),
        out_shape=([jax.ShapeDtypeStruct((s // d, d * c), BF16) for _ in range(3) for _, d in DIL_CONFIGS]
                   + [jax.ShapeDtypeStruct((s, c), BF16)] * 2 + [jax.ShapeDtypeStruct((s // tk, c, tk), BF16)]),
        scratch_shapes=[pltpu.VMEM((c // LANES, tm, LANES), F32)],
        compiler_params=_params(("parallel",)),
        name="proj_odd",
    )(x, g, win)


def _dilated_kernel(bias_ref, q_ref, kp_ref, kc_ref, vp_ref, vc_ref, o_ref, lse_ref, *, nub):
    t, tb = DIL_T, DIL_TB
    ub = pl.program_id(0) % nub
    c = DIL_HEADS_PER_GROUP * HEAD_DIM
    qf = q_ref[...].astype(F32)
    kcat = jnp.concatenate([kp_ref[...], kc_ref[...]], axis=0)
    vcat = jnp.concatenate([vp_ref[...], vc_ref[...]], axis=0)
    lane = lax.broadcasted_iota(jnp.int32, (tb, c), 1)
    col = lax.broadcasted_iota(jnp.int32, (tb, t + tb), 1)
    no_prev = col < jnp.where(ub == 0, t, 0)
    out = jnp.zeros((tb, c), F32)
    lse = jnp.zeros((tb, c), F32)
    for hh in range(DIL_HEADS_PER_GROUP):
        mine = jnp.logical_and(lane >= hh * HEAD_DIM, lane < (hh + 1) * HEAD_DIM)
        qh = jnp.where(mine, qf, 0.0).astype(BF16)
        s = _dot_t(qh, kcat) + bias_ref[hh]
        s = jnp.where(no_prev, NEG, s)
        m = jnp.max(s, axis=1, keepdims=True)
        e = jnp.exp(s - m)
        den = jnp.sum(e, axis=1, keepdims=True)
        oh = _dot(e.astype(BF16), vcat) / den
        out = jnp.where(mine, oh, out)
        lse = jnp.where(mine, m + jnp.log(den), lse)
    o_ref[...] = out
    lse_ref[...] = lse


def _dilated_group(q, k, v, dil, slopes):
    c = q.shape[1] // dil
    s = q.shape[0] * dil
    t, tb = DIL_T, DIL_TB
    nub = s // dil // tb
    a = np.arange(tb)[:, None]
    cc = np.arange(t + tb)[None, :]
    steps = t + a - cc
    ok = (steps >= 0) & (steps <= t)
    dist = (steps * dil).astype(np.float32)
    bias = np.where(ok[None], -np.asarray(slopes, np.float32)[:, None, None] * dist[None], np.float32(NEG))
    bias = jnp.asarray(bias, F32)
    cur = pl.BlockSpec((tb, c), lambda b: (b % nub, b // nub))
    prev = pl.BlockSpec((t, c), lambda b: (jnp.maximum(b % nub * (tb // t) - 1, 0), b // nub))
    o, lse = pl.pallas_call(
        functools.partial(_dilated_kernel, nub=nub),
        grid=(dil * nub,),
        in_specs=[_const_spec(bias.shape), cur, prev, cur, prev, cur],
        out_specs=[cur, cur],
        out_shape=[jax.ShapeDtypeStruct((s // dil, dil * c), F32)] * 2,
        compiler_params=_params(("parallel",)),
        name=f"dilated_d{dil}",
    )(bias, q, k, k, v, v)
    return o, lse


def _sb_kernel(tri_ref, q_ref, k_ref, vt_ref, o_ref):
    hp = pl.program_id(0)
    i = pl.program_id(1)
    tq, tk = SB_TQ, ATT_TK
    ratio = tq // tk
    c = SB_HEADS * HEAD_DIM
    tri = tri_ref[...]
    lane = lax.broadcasted_iota(jnp.int32, (tq, c), 1)
    qf = q_ref[...].astype(F32)
    qhs = []
    for t in range(SB_GROUP):
        hh = hp * SB_GROUP + t
        mine = jnp.logical_and(lane >= hh * HEAD_DIM, lane < (hh + 1) * HEAD_DIM)
        qhs.append(jnp.where(mine, qf, 0.0).astype(BF16))
    strict = lax.broadcasted_iota(jnp.int32, (tk, tk), 0) < lax.broadcasted_iota(jnp.int32, (tk, tk), 1)
    n_full = i * ratio

    def weights(z, r, masked):
        sp = jnp.log1p(jnp.exp(-jnp.abs(z)))
        log_beta = jnp.minimum(z, 0.0) - sp
        log_keep = log_beta - z
        if masked:
            log_keep = jnp.where(strict, log_keep, 0.0)
        hi = log_keep.astype(BF16)
        lo = (log_keep - hi.astype(F32)).astype(BF16)
        aft = _dot(tri, jnp.concatenate([hi, lo], axis=0))
        a = jnp.exp(log_beta + aft[0:tk] + r)
        if masked:
            a = jnp.where(strict, a, 0.0)
        return a.astype(BF16), r + aft[tk:tk + 1]

    def diag_step(d, state):
        kt = _k_tile(k_ref, n_full + d)
        vt = vt_ref[n_full + d]
        out = []
        for t, (rs, accs) in enumerate(state):
            z = _dot_t(kt, qhs[t][d * tk:, :])
            rs, accs = list(rs), list(accs)
            for blk in range(d, ratio):
                a, rs[blk] = weights(z[:, (blk - d) * tk:(blk - d + 1) * tk], rs[blk], blk == d)
                accs[blk] = accs[blk] + _dot(vt[t * HEAD_DIM:(t + 1) * HEAD_DIM, :], a)
            out.append((rs, accs))
        return out

    state = [([jnp.zeros((1, tk), F32)] * ratio, [jnp.zeros((HEAD_DIM, tk), F32)] * ratio) for _ in range(SB_GROUP)]
    for d in reversed(range(ratio)):
        state = diag_step(d, state)
    carry = tuple((jnp.concatenate(rs, axis=1), jnp.concatenate(accs, axis=1)) for rs, accs in state)

    def step(j, cr):
        kt = _k_tile(k_ref, j)
        vt = vt_ref[j]
        out = []
        for t, (r, acc) in enumerate(cr):
            a, r = weights(_dot_t(kt, qhs[t]), r, False)
            out.append((r, acc + _dot(vt[t * HEAD_DIM:(t + 1) * HEAD_DIM, :], a)))
        return tuple(out)

    def more(state):
        n, cr = state
        top = functools.reduce(jnp.maximum, [jnp.max(r) for r, _ in cr])
        return jnp.logical_and(n < n_full, top > SB_STOP)

    def walk(state):
        n, cr = state
        return n + 1, step(n_full - 1 - n, cr)

    _, carry = lax.while_loop(more, walk, (jnp.int32(0), carry))
    o_ref[...] = jnp.concatenate([acc for _, acc in carry], axis=0)


def _stickbreak(q, k, vt):
    s, c = q.shape
    tq, tk = SB_TQ, ATT_TK
    j = np.arange(tk)
    later = (j[None, :] > j[:, None]).astype(np.float32)
    tri = np.concatenate([np.concatenate([later, later], axis=1), np.ones((BF16_ROWS, 2 * tk), np.float32)], axis=0)
    tri = jnp.asarray(tri, BF16)
    return pl.pallas_call(
        _sb_kernel,
        grid=(SB_HEADS // SB_GROUP, s // tq),
        in_specs=[_const_spec(tri.shape),
                  pl.BlockSpec((tq, c), lambda h, i: (i, 0)),
                  _const_spec((s, c)),
                  pl.BlockSpec((s // tk, SB_GROUP * HEAD_DIM, tk), lambda h, i: (0, h, 0))],
        out_specs=pl.BlockSpec((SB_GROUP * HEAD_DIM, tq), lambda h, i: (h, i)),
        out_shape=jax.ShapeDtypeStruct((c, s), F32),
        compiler_params=_params(("parallel", "parallel")),
        name="stickbreak",
    )(tri, q, k, vt)


def _alibi_slopes(n):
    return 2.0 ** (-8.0 * np.arange(1, n + 1) / n)


def _pad_cols(w, lo, width):
    return jnp.pad(w, ((0, 0), (lo, width - lo - w.shape[1])))


def _rot_half_cols(w):
    half = w.shape[1] // 2
    return jnp.concatenate([-w[:, half:], w[:, :half]], axis=1)


def _even_weights(w_in, w_uq, w_ukv):
    o_kr = MLA_Q_RANK + MLA_KV_RANK
    w_kr = w_in[:, o_kr:o_kr + MLA_ROPE]
    win = jnp.concatenate([w_in[:, :o_kr], _pad_cols(w_kr, MLA_NOPE, LANES),
                           _pad_cols(_rot_half_cols(w_kr), MLA_NOPE, LANES),
                           w_in[:, o_kr + MLA_ROPE:]], axis=1)
    hd = MLA_NOPE + MLA_ROPE
    q_plain, q_rot = [], []
    for hh in range(MLA_HEADS):
        wh = w_uq[:, hh * hd:(hh + 1) * hd]
        q_plain.append(_pad_cols(wh, 0, LANES))
        q_rot.append(_pad_cols(_rot_half_cols(wh[:, MLA_NOPE:]), MLA_NOPE, LANES))
    wuq = jnp.concatenate(q_plain + q_rot, axis=1)
    hk = MLA_NOPE + MLA_V
    k_cols = [_pad_cols(w_ukv[:, hh * hk:hh * hk + MLA_NOPE], 0, LANES) for hh in range(MLA_HEADS)]
    v_cols = [w_ukv[:, hh * hk + MLA_NOPE:(hh + 1) * hk] for hh in range(MLA_HEADS)]
    wkv = jnp.concatenate(k_cols + v_cols, axis=1)
    return win.astype(BF16), wuq.astype(BF16), wkv.astype(BF16)


def _rope_tables(s):
    half = MLA_ROPE // 2
    inv = ROPE_BASE ** (-jnp.arange(half, dtype=F32) / half)
    ang = jnp.arange(s).astype(F32)[:, None] * inv
    cos2 = jnp.tile(jnp.cos(ang), (1, 2))
    sin2 = jnp.tile(jnp.sin(ang), (1, 2))
    scale = (MLA_NOPE + MLA_ROPE) ** -0.5 * LOG2E
    tail = LANES - MLA_NOPE - MLA_ROPE
    ones, zeros, ztail = jnp.ones((s, MLA_NOPE), F32), jnp.zeros((s, MLA_NOPE), F32), jnp.zeros((s, tail), F32)
    cosq = jnp.concatenate([ones, cos2, ztail], axis=1) * scale
    sinq = jnp.concatenate([zeros, sin2, ztail], axis=1) * scale
    cosk = jnp.concatenate([zeros, cos2, ztail], axis=1)
    sink = jnp.concatenate([zeros, sin2, ztail], axis=1)
    return cosq, sinq, cosk, sink


def _row2d(v):
    return v.reshape(1, -1).astype(F32)


def kernel(x, ev_pre_g, ev_w_in, ev_cq_g, ev_w_uq, ev_ckv_g, ev_w_ukv, ev_lam_q1, ev_lam_k1, ev_lam_q2,
           ev_lam_k2, ev_subln_g, ev_w_out, ev_post_g, od_pre_g, od_w_in, od_w_out, od_post_g, ffn_pre_g,
           ffn_w_up, ffn_conv_w, ffn_conv_b, ffn_w_down, ffn_post_g):
    b, s, _ = x.shape
    assert b == 1 and s % max(ROW_TILE, ATT_TQ, SB_TQ, DIL_TB * DIL_CONFIGS[-1][1]) == 0
    assert ROW_TILE % ATT_TK == 0 and ATT_TQ % ATT_TK == 0 and SB_TQ % ATT_TK == 0
    assert (ATT_TQ // ATT_TK) % MAIN_UNROLL == 0 and DIL_TB % DIL_T == 0
    assert sum(FF_CHUNKS) == D_FF
    assert ATT_TK <= 256
    xs = x[0]

    def ffn_consts(i):
        return [_row2d(ffn_pre_g[i]), ffn_w_up[i].astype(BF16), ffn_conv_w[i].astype(F32),
                _row2d(ffn_conv_b[i]), ffn_w_down[i].astype(BF16), _row2d(ffn_post_g[i])]

    win, wuq, wkv = _even_weights(ev_w_in[0], ev_w_uq[0], ev_w_ukv[0])
    cosq, sinq, cosk, sink = _rope_tables(s)
    pos_col = (jnp.arange(ROW_TILE) % ATT_TK).astype(BF16)[:, None]
    pos_tile = jnp.pad(jnp.tile(pos_col, (1, SLOPE_PARTS)), ((0, 0), (0, LANES - SLOPE_PARTS)))
    qm, km, vmt, dq, dk, dvt = _proj_even(xs, _row2d(ev_pre_g[0]), win, _row2d(ev_cq_g[0]), wuq,
                                          _row2d(ev_ckv_g[0]), wkv, cosq, sinq, cosk, sink, pos_tile)
    o_mla = _mla_attention(qm, km, vmt)
    lam_p = jnp.pad(jnp.stack([ev_lam_q1[0], ev_lam_k1[0], ev_lam_q2[0], ev_lam_k2[0]]).astype(F32),
                    ((0, SUBLANES - 4), (0, LANES - DIFF_QK)))
    d_slopes = _alibi_slopes(DIFF_HEADS) * LOG2E
    slope_arr = jnp.asarray(np.broadcast_to(d_slopes[:, None, None], (DIFF_HEADS, SUBLANES, LANES)), F32)
    qfeat = jnp.zeros((DIFF_HEADS, ATT_TQ, LANES), F32)
    rest = jnp.asarray(d_slopes, F32)
    for part in range(SLOPE_PARTS):
        piece = rest.astype(BF16).astype(F32)
        qfeat = qfeat.at[:, :, part].set(piece[:, None])
        rest = rest - piece
    lam_init = 0.8 - 0.6 * math.exp(-0.3 * 0)
    o_diff = _diff_attention(lam_p, slope_arr, qfeat.astype(BF16), _row2d(ev_subln_g[0]),
                             dq, dk, dvt, lam_init)
    x2 = _out_ffn(_out_ffn_even_kernel, "out_ffn_even", xs, [o_mla, o_diff], [],
                  [ev_w_out[0].astype(BF16), _row2d(ev_post_g[0])] + ffn_consts(0))

    p = _proj_odd(x2, _row2d(od_pre_g[0]), od_w_in[0].astype(BF16))
    slopes = _alibi_slopes(len(DIL_CONFIGS) * DIL_HEADS_PER_GROUP).reshape(len(DIL_CONFIGS), -1)
    dil_outs = []
    for gi, (_, dil) in enumerate(DIL_CONFIGS):
        dil_outs += list(_dilated_group(p[gi], p[3 + gi], p[6 + gi], dil, slopes[gi]))
    o_sbt = _stickbreak(p[9], p[10], p[11])
    lane_tiles = DIL_HEADS_PER_GROUP * HEAD_DIM // LANES
    x4 = _out_ffn(_out_ffn_odd_kernel, "out_ffn_odd", x2, dil_outs, [o_sbt],
                  [od_w_out[0].astype(BF16), _row2d(od_post_g[0])] + ffn_consts(1),
                  scratch=[pltpu.VMEM((lane_tiles, ROW_TILE, LANES), F32)])
    return x4[None]
```

```python
import functools
import math

import numpy as np
import jax
import jax.numpy as jnp
from jax import lax
from jax.experimental import pallas as pl
from jax.experimental.pallas import tpu as pltpu

F32 = jnp.float32
BF16 = jnp.bfloat16

D_MODEL = 1024
NORM_EPS = 1e-6
MLA_HEADS = 8
MLA_Q_RANK = 256
MLA_KV_RANK = 128
MLA_NOPE = 64
MLA_ROPE = 32
MLA_V = 64
ROPE_BASE = 10000.0
DIFF_HEADS = 4
DIFF_QK = 64
DIFF_V = 128
DIL_CONFIGS = ((128, 1), (512, 4), (2048, 16))
DIL_HEADS_PER_GROUP = 4
HEAD_DIM = 64
SB_HEADS = 4
D_FF = 2816
CONV_WIDTH = 3

LANES = 128
SUBLANES = 8
BF16_ROWS = 16
VMEM_LIMIT_BYTES = 56 * 1024 * 1024
ROW_TILE = 512
ATT_TQ = 2048
MAIN_UNROLL = 8
SB_TQ = 512
SB_GROUP = 4
ATT_TK = 256
DIL_T = 128
DIL_TB = 512
FF_CHUNKS = (512,) * 5 + (256,)
NEG = -1e30
LOG2E = 1.4426950408889634
UNDERFLOW_LOG2 = -152.0
MAX_JUMP = 32.0
SB_STOP = -110.0
SLOPE_PARTS = 3

_TRANS_B = (((1,), (1,)), ((), ()))


def _dot(a, b):
    return jnp.dot(a, b, preferred_element_type=F32)


def _dot_t(a, b):
    return lax.dot_general(a, b, _TRANS_B, preferred_element_type=F32)


def _rms(xf, g):
    ms = jnp.mean(xf * xf, axis=-1, keepdims=True)
    return xf * lax.rsqrt(ms + NORM_EPS) * g


def _params(sem):
    return pltpu.CompilerParams(dimension_semantics=sem, vmem_limit_bytes=VMEM_LIMIT_BYTES)


def _const_spec(shape):
    nd = len(shape)
    return pl.BlockSpec(shape, lambda *_: (0,) * nd, pipeline_mode=pl.Buffered(1))


def _store_transposed_tiles(dst_ref, src, heads, width, ones_rows):
    tk = ATT_TK
    per = width + ones_rows
    for b in range(src.shape[0] // tk):
        vt = src[b * tk:(b + 1) * tk, :].T.astype(BF16)
        for hh in range(heads):
            dst_ref[b, hh * per:hh * per + width, :] = vt[hh * width:(hh + 1) * width, :]
            if ones_rows:
                dst_ref[b, hh * per + width:(hh + 1) * per, :] = jnp.ones((ones_rows, tk), BF16)


def _rows_to_classes(x, dil, scr):
    if dil == 1:
        return x
    tm, c = x.shape
    for h in range(c // LANES):
        scr[h] = x[:, h * LANES:(h + 1) * LANES]
    return jnp.concatenate([scr[h, pl.ds(r, tm // dil, stride=dil), :]
                            for r in range(dil) for h in range(c // LANES)], axis=1)


def _classes_to_rows(ref, dil, scr):
    if dil == 1:
        return ref[...]
    n, c = ref.shape[0], ref.shape[1] // dil
    for r in range(dil):
        for h in range(c // LANES):
            scr[h, pl.ds(r, n, stride=dil), :] = ref[:, r * c + h * LANES:r * c + (h + 1) * LANES]
    return jnp.concatenate([scr[h] for h in range(c // LANES)], axis=1)


def _proj_even_kernel(x_ref, g_ref, win_ref, cqg_ref, wuq_ref, ckvg_ref, wkv_ref,
                      cosq_ref, sinq_ref, cosk_ref, sink_ref, pos_ref,
                      qm_ref, km_ref, vmt_ref, dq_ref, dk_ref, dvt_ref, kn_ref):
    h = _rms(x_ref[...], g_ref[...]).astype(BF16)
    proj = _dot(h, win_ref[...])
    nq = MLA_HEADS * LANES
    cq = _rms(proj[:, 0:MLA_Q_RANK], cqg_ref[...]).astype(BF16)
    qq = _dot(cq, wuq_ref[...])
    cosq, sinq = cosq_ref[...], sinq_ref[...]
    for hh in range(MLA_HEADS):
        a = qq[:, hh * LANES:(hh + 1) * LANES]
        b = qq[:, nq + hh * LANES:nq + (hh + 1) * LANES]
        qm_ref[:, hh * LANES:(hh + 1) * LANES] = (a * cosq + b * sinq).astype(BF16)
    o = MLA_Q_RANK
    ckv = _rms(proj[:, o:o + MLA_KV_RANK], ckvg_ref[...]).astype(BF16)
    kv = _dot(ckv, wkv_ref[...])
    o += MLA_KV_RANK
    krc = proj[:, o:o + LANES] * cosk_ref[...] + proj[:, o + LANES:o + 2 * LANES] * sink_ref[...]
    for hh in range(MLA_HEADS):
        km_ref[:, hh * LANES:(hh + 1) * LANES] = (kv[:, hh * LANES:(hh + 1) * LANES] + krc).astype(BF16)
    _store_transposed_tiles(vmt_ref, kv[:, nq:nq + MLA_HEADS * MLA_V], MLA_HEADS, MLA_V, BF16_ROWS)
    o += 2 * LANES
    nd = DIFF_HEADS * 2 * DIFF_QK
    dq_ref[...] = (proj[:, o:o + nd] * (DIFF_QK ** -0.5 * LOG2E)).astype(BF16)
    o += nd
    pos = pos_ref[...]
    lane = lax.broadcasted_iota(jnp.int32, (x_ref.shape[0], LANES), 1)
    norms = []
    for hh in range(DIFF_HEADS):
        kb = proj[:, o + hh * LANES:o + (hh + 1) * LANES].astype(BF16)
        dk_ref[:, hh * 2 * LANES:hh * 2 * LANES + LANES] = kb
        dk_ref[:, hh * 2 * LANES + LANES:(hh + 1) * 2 * LANES] = pos
        kk = kb.astype(F32) * kb.astype(F32)
        for first in (True, False):
            part = jnp.where((lane < DIFF_QK) == first, kk, 0.0)
            top = jnp.max(jnp.sum(part, axis=1, keepdims=True), axis=0, keepdims=True)
            norms.append(jnp.broadcast_to(top, (1, LANES)))
    kn_ref[0] = jnp.concatenate(norms, axis=0)
    o += nd
    _store_transposed_tiles(dvt_ref, proj[:, o:o + DIFF_HEADS * DIFF_V], DIFF_HEADS, DIFF_V, BF16_ROWS)


def _proj_even(x, g, win, cqg, wuq, ckvg, wkv, cosq, sinq, cosk, sink, pos):
    s = x.shape[0]
    tm, tk = ROW_TILE, ATT_TK
    row = lambda c: pl.BlockSpec((tm, c), lambda i: (i, 0))
    tile_t = lambda r: pl.BlockSpec((tm // tk, r, tk), lambda i: (i, 0, 0))
    mla_rows = MLA_HEADS * (MLA_V + BF16_ROWS)
    diff_rows = DIFF_HEADS * (DIFF_V + BF16_ROWS)
    flat = lambda c: jax.ShapeDtypeStruct((s, c), BF16)
    tiled = lambda r: jax.ShapeDtypeStruct((s // tk, r, tk), BF16)
    return pl.pallas_call(
        _proj_even_kernel,
        grid=(s // tm,),
        in_specs=[row(D_MODEL), _const_spec(g.shape), _const_spec(win.shape), _const_spec(cqg.shape),
                  _const_spec(wuq.shape), _const_spec(ckvg.shape), _const_spec(wkv.shape),
                  row(LANES), row(LANES), row(LANES), row(LANES), _const_spec(pos.shape)],
        out_specs=[row(MLA_HEADS * LANES), row(MLA_HEADS * LANES), tile_t(mla_rows),
                   row(DIFF_HEADS * 2 * DIFF_QK), row(DIFF_HEADS * 2 * LANES), tile_t(diff_rows),
                   pl.BlockSpec((1, 2 * DIFF_HEADS, LANES), lambda i: (i, 0, 0))],
        out_shape=[flat(MLA_HEADS * LANES), flat(MLA_HEADS * LANES), tiled(mla_rows),
                   flat(DIFF_HEADS * 2 * DIFF_QK), flat(DIFF_HEADS * 2 * LANES), tiled(diff_rows),
                   jax.ShapeDtypeStruct((s // tm, 2 * DIFF_HEADS, LANES), F32)],
        compiler_params=_params(("parallel",)),
        name="proj_even",
    )(x, g, win, cqg, wuq, ckvg, wkv, cosq, sinq, cosk, sink, pos)


def _flash_cols(chains, i, rows, tile_bias=None, scratch=None, reach=None):
    tq, tk = ATT_TQ, ATT_TK
    ratio = tq // tk
    n_full = i * ratio

    def safe_step(j, carry):
        out = []
        for (qh, load_k, load_vt), (m, acc) in zip(chains, carry):
            s = _dot_t(load_k(j), qh)
            mb = jnp.max(s, axis=0, keepdims=True)
            if tile_bias is not None:
                c = tile_bias(j)
                mb = mb + c
            m_new = jnp.maximum(m, mb)
            alpha = jnp.exp2(m - m_new)
            shift = m_new if tile_bias is None else m_new - c
            p = jnp.exp2(s - shift).astype(BF16)
            out.append((m_new, alpha * acc + _dot(load_vt(j), p)))
        return tuple(out)

    def fast_step(n, carry):
        j = n_full - 1 - n
        out = []
        for ci, ((qh, load_k, load_vt), (m, alpha, acc, jump)) in enumerate(zip(chains, carry)):
            acc = (acc + _dot(load_vt(j + 1), scratch[ci])) * alpha
            s = _dot_t(load_k(j), qh)
            c = None if tile_bias is None else tile_bias(j)
            shift = m if c is None else m - c
            scratch[ci] = jnp.exp2(s - shift).astype(BF16)
            mb = jnp.max(s, axis=0, keepdims=True)
            if c is not None:
                mb = mb + c
            m_new = jnp.maximum(m, mb)
            out.append((m_new, jnp.exp2(m - m_new), acc, jnp.maximum(jump, mb - m)))
        return tuple(out)

    def fast_group(g, c):
        for t in range(MAIN_UNROLL):
            c = fast_step(g * MAIN_UNROLL + t, c)
        return c

    def diag_step(d, state):
        tri = lax.broadcasted_iota(jnp.int32, (tk, tk), 0) <= lax.broadcasted_iota(jnp.int32, (tk, tk), 1)
        out = []
        for (qh, load_k, load_vt), (ms, accs) in zip(chains, state):
            s = _dot_t(load_k(n_full + d), qh[d * tk:, :])
            vt = load_vt(n_full + d)
            c = None if tile_bias is None else tile_bias(n_full + d)
            ms, accs = list(ms), list(accs)
            for blk in range(d, ratio):
                sb = s[:, (blk - d) * tk:(blk - d + 1) * tk]
                if blk == d:
                    sb = jnp.where(tri, sb, NEG)
                mb = jnp.max(sb, axis=0, keepdims=True)
                if c is not None:
                    mb = mb + c
                m_new = jnp.maximum(ms[blk], mb)
                shift = m_new if c is None else m_new - c
                p = jnp.exp2(sb - shift).astype(BF16)
                accs[blk] = jnp.exp2(ms[blk] - m_new) * accs[blk] + _dot(vt, p)
                ms[blk] = m_new
            out.append((ms, accs))
        return out

    state = [([jnp.full((1, tk), NEG, F32)] * ratio, [jnp.zeros((rows, tk), F32)] * ratio) for _ in chains]
    for d in range(ratio):
        state = diag_step(d, state)
    diag = tuple((jnp.concatenate(ms, axis=1), jnp.concatenate(accs, axis=1)) for ms, accs in state)
    for ci in range(len(chains)):
        scratch[ci] = jnp.zeros((tk, tq), BF16)
    one, low = jnp.ones((1, tq), F32), jnp.full((1, tq), NEG, F32)
    groups = n_full // MAIN_UNROLL
    start = tuple((m, one, acc, low) for m, acc in diag)
    if reach is None:
        done, fast = groups, lax.fori_loop(0, groups, fast_group, start)
    else:
        def live(state):
            g, c = state
            nxt = tile_bias(n_full - 1 - g * MAIN_UNROLL)
            gap = functools.reduce(jnp.maximum, [jnp.max(b + nxt) - jnp.min(m)
                                                 for b, (m, _, _, _) in zip(reach, c)])
            return jnp.logical_and(g < groups, gap > UNDERFLOW_LOG2)

        done, fast = lax.while_loop(live, lambda st: (st[0] + 1, fast_group(st[0], st[1])), (jnp.int32(0), start))
    last = n_full - done * MAIN_UNROLL
    accs = [(acc + _dot(load_vt(last), scratch[ci])) * alpha
            for ci, ((_, _, load_vt), (_, alpha, acc, _)) in enumerate(zip(chains, fast))]
    worst = functools.reduce(jnp.maximum, [jnp.max(jump) for _, _, _, jump in fast])

    def redo():
        carry = lax.fori_loop(0, n_full, safe_step, diag)
        return [acc for _, acc in carry]

    return lax.cond(worst > MAX_JUMP, redo, lambda: accs)


def _k_tile(k_ref, j, lanes=slice(None)):
    return k_ref[pl.ds(pl.multiple_of(j * ATT_TK, ATT_TK), ATT_TK), lanes]


def _mla_kernel(q_ref, k_ref, vt_ref, o_ref, p_ref):
    i = pl.program_id(1)
    rows = MLA_V + BF16_ROWS
    def chain(hh):
        lanes = slice(hh * LANES, (hh + 1) * LANES)
        return (q_ref[:, lanes], lambda j: _k_tile(k_ref, j, lanes),
                lambda j: vt_ref[j, hh * rows:(hh + 1) * rows, :])

    accs = _flash_cols([chain(0), chain(1)], i, rows, scratch=p_ref)
    halves = [acc[0:MLA_V] / acc[MLA_V:MLA_V + 1] for acc in accs]
    o_ref[...] = jnp.concatenate(halves, axis=0).T.astype(BF16)


def _mla_attention(qm, km, vmt):
    s = qm.shape[0]
    tq = ATT_TQ
    rows = 2 * (MLA_V + BF16_ROWS)
    return pl.pallas_call(
        _mla_kernel,
        grid=(MLA_HEADS // 2, s // tq),
        in_specs=[pl.BlockSpec((tq, 2 * LANES), lambda p, i: (i, p)),
                  pl.BlockSpec((s, 2 * LANES), lambda p, i: (0, p)),
                  pl.BlockSpec((vmt.shape[0], rows, ATT_TK), lambda p, i: (0, p, 0))],
        out_specs=pl.BlockSpec((tq, LANES), lambda p, i: (i, p)),
        out_shape=jax.ShapeDtypeStruct((s, MLA_HEADS * MLA_V), BF16),
        scratch_shapes=[pltpu.VMEM((2, ATT_TK, tq), BF16)],
        compiler_params=_params(("parallel", "parallel")),
        name="mla_attn",
    )(qm, km, vmt)


def _diff_kernel(lam_ref, slope_ref, kn_ref, qf_ref, g_ref, q_ref, k_ref, vt_ref, o_ref, p_ref, *, lam_init):
    i = pl.program_id(1)
    tq = ATT_TQ
    lane = lax.broadcasted_iota(jnp.int32, (tq, LANES), 1)
    qf32 = q_ref[...].astype(F32)
    feat = qf_ref[...]
    zero = jnp.zeros_like(qf32)
    qa = jnp.concatenate([jnp.where(lane < DIFF_QK, qf32, zero).astype(BF16), feat], axis=1)
    qb = jnp.concatenate([jnp.where(lane >= DIFF_QK, qf32, zero).astype(BF16), feat], axis=1)
    slope = slope_ref[0:1, 0:1]
    rows = DIFF_V + BF16_ROWS
    tile_bias = lambda j: slope * (j * ATT_TK - i * tq).astype(F32)
    load_k = lambda j: _k_tile(k_ref, j)
    load_vt = lambda j: vt_ref[j]
    q2 = qf32 * qf32
    top = lambda z: jnp.max(jnp.sum(z, axis=1, keepdims=True), axis=0, keepdims=True)
    qn = (top(jnp.where(lane < DIFF_QK, q2, zero)), top(jnp.where(lane >= DIFF_QK, q2, zero)))
    reach = [jnp.sqrt(qn[x] * kn_ref[x:x + 1, 0:1]) * 1.001 + slope * ATT_TK for x in range(2)]
    acc1, acc2 = _flash_cols([(qa, load_k, load_vt), (qb, load_k, load_vt)], i, rows, tile_bias, p_ref, reach)
    lp = lam_ref[...]
    s1 = jnp.sum(lp[0:1, :] * lp[1:2, :], axis=1, keepdims=True)
    s2 = jnp.sum(lp[2:3, :] * lp[3:4, :], axis=1, keepdims=True)
    lam = jnp.exp(s1) - jnp.exp(s2) + lam_init
    ot = acc1[0:DIFF_V] / acc1[DIFF_V:DIFF_V + 1] - lam * (acc2[0:DIFF_V] / acc2[DIFF_V:DIFF_V + 1])
    o_ref[...] = (_rms(ot.T, g_ref[...]) * (1.0 - lam_init)).astype(BF16)


def _diff_attention(lam_p, slope_arr, knorm, qfeat, subln_g, dq, dk, dvt, lam_init):
    s = dq.shape[0]
    tq = ATT_TQ
    return pl.pallas_call(
        functools.partial(_diff_kernel, lam_init=lam_init),
        grid=(DIFF_HEADS, s // tq),
        in_specs=[pl.BlockSpec(lam_p.shape, lambda h, i: (0, 0)),
                  pl.BlockSpec((None, SUBLANES, LANES), lambda h, i: (h, 0, 0)),
                  pl.BlockSpec((None, SUBLANES, LANES), lambda h, i: (h, 0, 0)),
                  pl.BlockSpec((None, tq, LANES), lambda h, i: (h, 0, 0)),
                  pl.BlockSpec(subln_g.shape, lambda h, i: (0, 0)),
                  pl.BlockSpec((tq, LANES), lambda h, i: (i, h)),
                  pl.BlockSpec((s, 2 * LANES), lambda h, i: (0, h)),
                  pl.BlockSpec((dvt.shape[0], DIFF_V + BF16_ROWS, ATT_TK), lambda h, i: (0, h, 0))],
        out_specs=pl.BlockSpec((tq, LANES), lambda h, i: (i, h)),
        out_shape=jax.ShapeDtypeStruct((s, DIFF_HEADS * DIFF_V), BF16),
        scratch_shapes=[pltpu.VMEM((2, ATT_TK, tq), BF16)],
        compiler_params=_params(("parallel", "parallel")),
        name="diff_attn",
    )(lam_p, slope_arr, knorm, qfeat, subln_g, dq, dk, dvt)


def _shift_rows(u, k, prev):
    top = jnp.where(lax.broadcasted_iota(jnp.int32, prev.shape, 0) < k,
                    pltpu.roll(prev, k, 0), pltpu.roll(u[0:SUBLANES, :], k, 0))
    return jnp.concatenate([top, pltpu.roll(u, k, 0)[SUBLANES:, :]], axis=0)


def _ffn_tile(x1, fpre, wup_ref, cw_ref, cb_ref, wdn_ref, fpost, carry_ref):
    tm = x1.shape[0]
    hf = _rms(x1, fpre).astype(BF16)
    acts = []
    lo = 0
    for width in FF_CHUNKS:
        ys = []
        for part in range(2):
            c0 = part * D_FF + lo
            u = _dot(hf, wup_ref[:, c0:c0 + width])
            prev = carry_ref[:, c0:c0 + width]
            carry_ref[:, c0:c0 + width] = u[tm - SUBLANES:, :]
            ys.append(cw_ref[2:3, c0:c0 + width] * u
                      + cw_ref[1:2, c0:c0 + width] * _shift_rows(u, 1, prev)
                      + cw_ref[0:1, c0:c0 + width] * _shift_rows(u, 2, prev)
                      + cb_ref[:, c0:c0 + width])
        gate, up = ys
        acts.append((gate * (1.0 / (1.0 + jnp.exp(-gate))) * up).astype(BF16))
        lo += width
    acc = _dot(jnp.concatenate(acts, axis=1), wdn_ref[...])
    return x1 + _rms(acc, fpost)


def _out_ffn_even_kernel(x_ref, a_ref, b_ref, wout_ref, postg_ref, fpre_ref, wup_ref, cw_ref, cb_ref,
                         wdn_ref, fpost_ref, o_ref, carry_ref):
    @pl.when(pl.program_id(0) == 0)
    def _():
        carry_ref[...] = jnp.zeros_like(carry_ref)

    na = a_ref.shape[1]
    mix = _dot(a_ref[...], wout_ref[0:na, :]) + _dot(b_ref[...], wout_ref[na:, :])
    x1 = x_ref[...] + _rms(mix, postg_ref[...])
    o_ref[...] = _ffn_tile(x1, fpre_ref[...], wup_ref, cw_ref, cb_ref, wdn_ref, fpost_ref[...], carry_ref)


def _out_ffn_odd_kernel(x_ref, o0_ref, l0_ref, o1_ref, l1_ref, o2_ref, l2_ref, bt_ref, wout_ref, postg_ref,
                        fpre_ref, wup_ref, cw_ref, cb_ref, wdn_ref, fpost_ref, o_ref, carry_ref, scr):
    @pl.when(pl.program_id(0) == 0)
    def _():
        carry_ref[...] = jnp.zeros_like(carry_ref)

    (l0, o0), (l1, o1), (l2, o2) = [(_classes_to_rows(l_ref, d, scr), _classes_to_rows(og_ref, d, scr))
                                    for (l_ref, og_ref), (_, d) in zip(((l0_ref, o0_ref), (l1_ref, o1_ref),
                                                                        (l2_ref, o2_ref)), DIL_CONFIGS)]
    m = jnp.maximum(jnp.maximum(l0, l1), l2)
    e0, e1, e2 = jnp.exp(l0 - m), jnp.exp(l1 - m), jnp.exp(l2 - m)
    dil = (e0 * o0 + e1 * o1 + e2 * o2) / (e0 + e1 + e2)
    na = dil.shape[1]
    sb = bt_ref[...].T
    mix = _dot(dil.astype(BF16), wout_ref[0:na, :]) + _dot(sb.astype(BF16), wout_ref[na:, :])
    x1 = x_ref[...] + _rms(mix, postg_ref[...])
    o_ref[...] = _ffn_tile(x1, fpre_ref[...], wup_ref, cw_ref, cb_ref, wdn_ref, fpost_ref[...], carry_ref)


def _out_ffn(kernel, name, x, acts, acts_t, consts, scratch=()):
    s = x.shape[0]
    tm = ROW_TILE
    row = lambda c, n=s: pl.BlockSpec((tm * n // s, c), lambda i: (i, 0))
    col = lambda r: pl.BlockSpec((r, tm), lambda i: (0, i))
    return pl.pallas_call(
        kernel,
        grid=(s // tm,),
        in_specs=([row(D_MODEL)] + [row(a.shape[1], a.shape[0]) for a in acts] + [col(a.shape[0]) for a in acts_t]
                  + [_const_spec(c.shape) for c in consts]),
        out_specs=row(D_MODEL),
        out_shape=jax.ShapeDtypeStruct((s, D_MODEL), F32),
        scratch_shapes=[pltpu.VMEM((SUBLANES, 2 * D_FF), F32), *scratch],
        compiler_params=_params(("arbitrary",)),
        name=name,
    )(x, *acts, *acts_t, *consts)


def _proj_odd_kernel(x_ref, g_ref, win_ref, *refs):
    out_refs, scr = refs[:-1], refs[-1]
    h = _rms(x_ref[...], g_ref[...]).astype(BF16)
    proj = _dot(h, win_ref[...])
    c = DIL_HEADS_PER_GROUP * HEAD_DIM
    scale = HEAD_DIM ** -0.5
    for n, ref in enumerate(out_refs[:-1]):
        blk = proj[:, n * c:(n + 1) * c]
        if n < 3 or n == 9:
            blk = blk * scale
        if n < 9:
            blk = _rows_to_classes(blk, DIL_CONFIGS[n % 3][1], scr)
        ref[...] = blk.astype(BF16)
    _store_transposed_tiles(out_refs[-1], proj[:, 11 * c:12 * c], SB_HEADS, HEAD_DIM, 0)


def _proj_odd(x, g, win):
    s = x.shape[0]
    tm, tk = ROW_TILE, ATT_TK
    c = DIL_HEADS_PER_GROUP * HEAD_DIM
    row = lambda w, d=1: pl.BlockSpec((tm // d, d * w), lambda i: (i, 0))
    return pl.pallas_call(
        _proj_odd_kernel,
        grid=(s // tm,),
        in_specs=[row(D_MODEL), _const_spec(g.shape), _const_spec(win.shape)],
        out_specs=([row(c, d) for _ in range(3) for _, d in DIL_CONFIGS] + [row(c)] * 2
                   + [pl.BlockSpec((tm // tk, c, tk), lambda i: (i, 0, 0))]),
        out_shape=([jax.ShapeDtypeStruct((s // d, d * c), BF16) for _ in range(3) for _, d in DIL_CONFIGS]
                   + [jax.ShapeDtypeStruct((s, c), BF16)] * 2 + [jax.ShapeDtypeStruct((s // tk, c, tk), BF16)]),
        scratch_shapes=[pltpu.VMEM((c // LANES, tm, LANES), F32)],
        compiler_params=_params(("parallel",)),
        name="proj_odd",
    )(x, g, win)


def _dilated_kernel(bias_ref, q_ref, kp_ref, kc_ref, vp_ref, vc_ref, o_ref, lse_ref, *, nub):
    t, tb = DIL_T, DIL_TB
    ub = pl.program_id(0) % nub
    c = DIL_HEADS_PER_GROUP * HEAD_DIM
    qf = q_ref[...].astype(F32)
    kcat = jnp.concatenate([kp_ref[...], kc_ref[...]], axis=0)
    vcat = jnp.concatenate([vp_ref[...], vc_ref[...]], axis=0)
    lane = lax.broadcasted_iota(jnp.int32, (tb, c), 1)
    col = lax.broadcasted_iota(jnp.int32, (tb, t + tb), 1)
    no_prev = col < jnp.where(ub == 0, t, 0)
    out = jnp.zeros((tb, c), F32)
    lse = jnp.zeros((tb, c), F32)
    for hh in range(DIL_HEADS_PER_GROUP):
        mine = jnp.logical_and(lane >= hh * HEAD_DIM, lane < (hh + 1) * HEAD_DIM)
        qh = jnp.where(mine, qf, 0.0).astype(BF16)
        s = _dot_t(qh, kcat) + bias_ref[hh]
        s = jnp.where(no_prev, NEG, s)
        m = jnp.max(s, axis=1, keepdims=True)
        e = jnp.exp(s - m)
        den = jnp.sum(e, axis=1, keepdims=True)
        oh = _dot(e.astype(BF16), vcat) / den
        out = jnp.where(mine, oh, out)
        lse = jnp.where(mine, m + jnp.log(den), lse)
    o_ref[...] = out
    lse_ref[...] = lse


def _dilated_group(q, k, v, dil, slopes):
    c = q.shape[1] // dil
    s = q.shape[0] * dil
    t, tb = DIL_T, DIL_TB
    nub = s // dil // tb
    a = np.arange(tb)[:, None]
    cc = np.arange(t + tb)[None, :]
    steps = t + a - cc
    ok = (steps >= 0) & (steps <= t)
    dist = (steps * dil).astype(np.float32)
    bias = np.where(ok[None], -np.asarray(slopes, np.float32)[:, None, None] * dist[None], np.float32(NEG))
    bias = jnp.asarray(bias, F32)
    cur = pl.BlockSpec((tb, c), lambda b: (b % nub, b // nub))
    prev = pl.BlockSpec((t, c), lambda b: (jnp.maximum(b % nub * (tb // t) - 1, 0), b // nub))
    o, lse = pl.pallas_call(
        functools.partial(_dilated_kernel, nub=nub),
        grid=(dil * nub,),
        in_specs=[_const_spec(bias.shape), cur, prev, cur, prev, cur],
        out_specs=[cur, cur],
        out_shape=[jax.ShapeDtypeStruct((s // dil, dil * c), F32)] * 2,
        compiler_params=_params(("parallel",)),
        name=f"dilated_d{dil}",
    )(bias, q, k, k, v, v)
    return o, lse


def _sb_kernel(tri_ref, q_ref, k_ref, vt_ref, o_ref):
    hp = pl.program_id(0)
    i = pl.program_id(1)
    tq, tk = SB_TQ, ATT_TK
    ratio = tq // tk
    c = SB_HEADS * HEAD_DIM
    tri = tri_ref[...]
    lane = lax.broadcasted_iota(jnp.int32, (tq, c), 1)
    qf = q_ref[...].astype(F32)
    qhs = []
    for t in range(SB_GROUP):
        hh = hp * SB_GROUP + t
        mine = jnp.logical_and(lane >= hh * HEAD_DIM, lane < (hh + 1) * HEAD_DIM)
        qhs.append(jnp.where(mine, qf, 0.0).astype(BF16))
    strict = lax.broadcasted_iota(jnp.int32, (tk, tk), 0) < lax.broadcasted_iota(jnp.int32, (tk, tk), 1)
    n_full = i * ratio

    def weights(z, r, masked):
        sp = jnp.log1p(jnp.exp(-jnp.abs(z)))
        log_beta = jnp.minimum(z, 0.0) - sp
        log_keep = log_beta - z
        if masked:
            log_keep = jnp.where(strict, log_keep, 0.0)
        hi = log_keep.astype(BF16)
        lo = (log_keep - hi.astype(F32)).astype(BF16)
        aft = _dot(tri, jnp.concatenate([hi, lo], axis=0))
        a = jnp.exp(log_beta + aft[0:tk] + r)
        if masked:
            a = jnp.where(strict, a, 0.0)
        return a.astype(BF16), r + aft[tk:tk + 1]

    def diag_step(d, state):
        kt = _k_tile(k_ref, n_full + d)
        vt = vt_ref[n_full + d]
        out = []
        for t, (rs, accs) in enumerate(state):
            z = _dot_t(kt, qhs[t][d * tk:, :])
            rs, accs = list(rs), list(accs)
            for blk in range(d, ratio):
                a, rs[blk] = weights(z[:, (blk - d) * tk:(blk - d + 1) * tk], rs[blk], blk == d)
                accs[blk] = accs[blk] + _dot(vt[t * HEAD_DIM:(t + 1) * HEAD_DIM, :], a)
            out.append((rs, accs))
        return out

    state = [([jnp.zeros((1, tk), F32)] * ratio, [jnp.zeros((HEAD_DIM, tk), F32)] * ratio) for _ in range(SB_GROUP)]
    for d in reversed(range(ratio)):
        state = diag_step(d, state)
    carry = tuple((jnp.concatenate(rs, axis=1), jnp.concatenate(accs, axis=1)) for rs, accs in state)

    def step(j, cr):
        kt = _k_tile(k_ref, j)
        vt = vt_ref[j]
        out = []
        for t, (r, acc) in enumerate(cr):
            a, r = weights(_dot_t(kt, qhs[t]), r, False)
            out.append((r, acc + _dot(vt[t * HEAD_DIM:(t + 1) * HEAD_DIM, :], a)))
        return tuple(out)

    def more(state):
        n, cr = state
        top = functools.reduce(jnp.maximum, [jnp.max(r) for r, _ in cr])
        return jnp.logical_and(n < n_full, top > SB_STOP)

    def walk(state):
        n, cr = state
        return n + 1, step(n_full - 1 - n, cr)

    _, carry = lax.while_loop(more, walk, (jnp.int32(0), carry))
    o_ref[...] = jnp.concatenate([acc for _, acc in carry], axis=0)


def _stickbreak(q, k, vt):
    s, c = q.shape
    tq, tk = SB_TQ, ATT_TK
    j = np.arange(tk)
    later = (j[None, :] > j[:, None]).astype(np.float32)
    tri = np.concatenate([np.concatenate([later, later], axis=1), np.ones((BF16_ROWS, 2 * tk), np.float32)], axis=0)
    tri = jnp.asarray(tri, BF16)
    return pl.pallas_call(
        _sb_kernel,
        grid=(SB_HEADS // SB_GROUP, s // tq),
        in_specs=[_const_spec(tri.shape),
                  pl.BlockSpec((tq, c), lambda h, i: (i, 0)),
                  _const_spec((s, c)),
                  pl.BlockSpec((s // tk, SB_GROUP * HEAD_DIM, tk), lambda h, i: (0, h, 0))],
        out_specs=pl.BlockSpec((SB_GROUP * HEAD_DIM, tq), lambda h, i: (h, i)),
        out_shape=jax.ShapeDtypeStruct((c, s), F32),
        compiler_params=_params(("parallel", "parallel")),
        name="stickbreak",
    )(tri, q, k, vt)


def _alibi_slopes(n):
    return 2.0 ** (-8.0 * np.arange(1, n + 1) / n)


def _pad_cols(w, lo, width):
    return jnp.pad(w, ((0, 0), (lo, width - lo - w.shape[1])))


def _rot_half_cols(w):
    half = w.shape[1] // 2
    return jnp.concatenate([-w[:, half:], w[:, :half]], axis=1)


def _even_weights(w_in, w_uq, w_ukv):
    o_kr = MLA_Q_RANK + MLA_KV_RANK
    w_kr = w_in[:, o_kr:o_kr + MLA_ROPE]
    win = jnp.concatenate([w_in[:, :o_kr], _pad_cols(w_kr, MLA_NOPE, LANES),
                           _pad_cols(_rot_half_cols(w_kr), MLA_NOPE, LANES),
                           w_in[:, o_kr + MLA_ROPE:]], axis=1)
    hd = MLA_NOPE + MLA_ROPE
    q_plain, q_rot = [], []
    for hh in range(MLA_HEADS):
        wh = w_uq[:, hh * hd:(hh + 1) * hd]
        q_plain.append(_pad_cols(wh, 0, LANES))
        q_rot.append(_pad_cols(_rot_half_cols(wh[:, MLA_NOPE:]), MLA_NOPE, LANES))
    wuq = jnp.concatenate(q_plain + q_rot, axis=1)
    hk = MLA_NOPE + MLA_V
    k_cols = [_pad_cols(w_ukv[:, hh * hk:hh * hk + MLA_NOPE], 0, LANES) for hh in range(MLA_HEADS)]
    v_cols = [w_ukv[:, hh * hk + MLA_NOPE:(hh + 1) * hk] for hh in range(MLA_HEADS)]
    wkv = jnp.concatenate(k_cols + v_cols, axis=1)
    return win.astype(BF16), wuq.astype(BF16), wkv.astype(BF16)


def _rope_tables(s):
    half = MLA_ROPE // 2
    inv = ROPE_BASE ** (-jnp.arange(half, dtype=F32) / half)
    ang = jnp.arange(s).astype(F32)[:, None] * inv
    cos2 = jnp.tile(jnp.cos(ang), (1, 2))
    sin2 = jnp.tile(jnp.sin(ang), (1, 2))
    scale = (MLA_NOPE + MLA_ROPE) ** -0.5 * LOG2E
    tail = LANES - MLA_NOPE - MLA_ROPE
    ones, zeros, ztail = jnp.ones((s, MLA_NOPE), F32), jnp.zeros((s, MLA_NOPE), F32), jnp.zeros((s, tail), F32)
    cosq = jnp.concatenate([ones, cos2, ztail], axis=1) * scale
    sinq = jnp.concatenate([zeros, sin2, ztail], axis=1) * scale
    cosk = jnp.concatenate([zeros, cos2, ztail], axis=1)
    sink = jnp.concatenate([zeros, sin2, ztail], axis=1)
    return cosq, sinq, cosk, sink


def _row2d(v):
    return v.reshape(1, -1).astype(F32)


def kernel(x, ev_pre_g, ev_w_in, ev_cq_g, ev_w_uq, ev_ckv_g, ev_w_ukv, ev_lam_q1, ev_lam_k1, ev_lam_q2,
           ev_lam_k2, ev_subln_g, ev_w_out, ev_post_g, od_pre_g, od_w_in, od_w_out, od_post_g, ffn_pre_g,
           ffn_w_up, ffn_conv_w, ffn_conv_b, ffn_w_down, ffn_post_g):
    b, s, _ = x.shape
    assert b == 1 and s % max(ROW_TILE, ATT_TQ, SB_TQ, DIL_TB * DIL_CONFIGS[-1][1]) == 0
    assert ROW_TILE % ATT_TK == 0 and ATT_TQ % ATT_TK == 0 and SB_TQ % ATT_TK == 0
    assert (ATT_TQ // ATT_TK) % MAIN_UNROLL == 0 and DIL_TB % DIL_T == 0
    assert sum(FF_CHUNKS) == D_FF
    assert ATT_TK <= 256
    xs = x[0]

    def ffn_consts(i):
        return [_row2d(ffn_pre_g[i]), ffn_w_up[i].astype(BF16), ffn_conv_w[i].astype(F32),
                _row2d(ffn_conv_b[i]), ffn_w_down[i].astype(BF16), _row2d(ffn_post_g[i])]

    win, wuq, wkv = _even_weights(ev_w_in[0], ev_w_uq[0], ev_w_ukv[0])
    cosq, sinq, cosk, sink = _rope_tables(s)
    pos_col = (jnp.arange(ROW_TILE) % ATT_TK).astype(BF16)[:, None]
    pos_tile = jnp.pad(jnp.tile(pos_col, (1, SLOPE_PARTS)), ((0, 0), (0, LANES - SLOPE_PARTS)))
    qm, km, vmt, dq, dk, dvt, kn_tiles = _proj_even(xs, _row2d(ev_pre_g[0]), win, _row2d(ev_cq_g[0]), wuq,
                                          _row2d(ev_ckv_g[0]), wkv, cosq, sinq, cosk, sink, pos_tile)
    o_mla = _mla_attention(qm, km, vmt)
    lam_p = jnp.pad(jnp.stack([ev_lam_q1[0], ev_lam_k1[0], ev_lam_q2[0], ev_lam_k2[0]]).astype(F32),
                    ((0, SUBLANES - 4), (0, LANES - DIFF_QK)))
    d_slopes = _alibi_slopes(DIFF_HEADS) * LOG2E
    slope_arr = jnp.asarray(np.broadcast_to(d_slopes[:, None, None], (DIFF_HEADS, SUBLANES, LANES)), F32)
    qfeat = jnp.zeros((DIFF_HEADS, ATT_TQ, LANES), F32)
    rest = jnp.asarray(d_slopes, F32)
    for part in range(SLOPE_PARTS):
        piece = rest.astype(BF16).astype(F32)
        qfeat = qfeat.at[:, :, part].set(piece[:, None])
        rest = rest - piece
    lam_init = 0.8 - 0.6 * math.exp(-0.3 * 0)
    knorm = jnp.pad(jnp.max(kn_tiles, axis=0).reshape(DIFF_HEADS, 2, LANES), ((0, 0), (0, SUBLANES - 2), (0, 0)))
    o_diff = _diff_attention(lam_p, slope_arr, knorm, qfeat.astype(BF16), _row2d(ev_subln_g[0]),
                             dq, dk, dvt, lam_init)
    x2 = _out_ffn(_out_ffn_even_kernel, "out_ffn_even", xs, [o_mla, o_diff], [],
                  [ev_w_out[0].astype(BF16), _row2d(ev_post_g[0])] + ffn_consts(0))

    p = _proj_odd(x2, _row2d(od_pre_g[0]), od_w_in[0].astype(BF16))
    slopes = _alibi_slopes(len(DIL_CONFIGS) * DIL_HEADS_PER_GROUP).reshape(len(DIL_CONFIGS), -1)
    dil_outs = []
    for gi, (_, dil) in enumerate(DIL_CONFIGS):
        dil_outs += list(_dilated_group(p[gi], p[3 + gi], p[6 + gi], dil, slopes[gi]))
    o_sbt = _stickbreak(p[9], p[10], p[11])
    lane_tiles = DIL_HEADS_PER_GROUP * HEAD_DIM // LANES
    x4 = _out_ffn(_out_ffn_odd_kernel, "out_ffn_odd", x2, dil_outs, [o_sbt],
                  [od_w_out[0].astype(BF16), _row2d(od_post_g[0])] + ffn_consts(1),
                  scratch=[pltpu.VMEM((lane_tiles, ROW_TILE, LANES), F32)])
    return x4[None]
```

```python
import functools
import math

import numpy as np
import jax
import jax.numpy as jnp
from jax import lax
from jax.experimental import pallas as pl
from jax.experimental.pallas import tpu as pltpu

F32 = jnp.float32
BF16 = jnp.bfloat16

D_MODEL = 1024
NORM_EPS = 1e-6
MLA_HEADS = 8
MLA_Q_RANK = 256
MLA_KV_RANK = 128
MLA_NOPE = 64
MLA_ROPE = 32
MLA_V = 64
ROPE_BASE = 10000.0
DIFF_HEADS = 4
DIFF_QK = 64
DIFF_V = 128
DIL_CONFIGS = ((128, 1), (512, 4), (2048, 16))
DIL_HEADS_PER_GROUP = 4
HEAD_DIM = 64
SB_HEADS = 4
D_FF = 2816
CONV_WIDTH = 3

LANES = 128
SUBLANES = 8
BF16_ROWS = 16
VMEM_LIMIT_BYTES = 56 * 1024 * 1024
ROW_TILE = 512
ATT_TQ = 2048
MAIN_UNROLL = 8
SB_TQ = 512
SB_GROUP = 4
ATT_TK = 256
DIL_T = 128
DIL_TB = 512
FF_CHUNKS = (512,) * 5 + (256,)
NEG = -1e30
LOG2E = 1.4426950408889634
MLA_SCALE = (MLA_NOPE + MLA_ROPE) ** -0.5 * LOG2E
REACH_MARGIN = 1.001
UNDERFLOW_LOG2 = -152.0
MAX_JUMP = 32.0
SB_STOP = -110.0
SLOPE_PARTS = 3

_TRANS_B = (((1,), (1,)), ((), ()))


def _dot(a, b):
    return jnp.dot(a, b, preferred_element_type=F32)


def _dot_t(a, b):
    return lax.dot_general(a, b, _TRANS_B, preferred_element_type=F32)


def _rms(xf, g):
    ms = jnp.mean(xf * xf, axis=-1, keepdims=True)
    return xf * lax.rsqrt(ms + NORM_EPS) * g


def _params(sem):
    return pltpu.CompilerParams(dimension_semantics=sem, vmem_limit_bytes=VMEM_LIMIT_BYTES)


def _const_spec(shape):
    nd = len(shape)
    return pl.BlockSpec(shape, lambda *_: (0,) * nd, pipeline_mode=pl.Buffered(1))


def _store_transposed_tiles(dst_ref, src, heads, width, ones_rows):
    tk = ATT_TK
    per = width + ones_rows
    for b in range(src.shape[0] // tk):
        vt = src[b * tk:(b + 1) * tk, :].T.astype(BF16)
        for hh in range(heads):
            dst_ref[b, hh * per:hh * per + width, :] = vt[hh * width:(hh + 1) * width, :]
            if ones_rows:
                dst_ref[b, hh * per + width:(hh + 1) * per, :] = jnp.ones((ones_rows, tk), BF16)


def _rows_to_classes(x, dil, scr):
    if dil == 1:
        return x
    tm, c = x.shape
    for h in range(c // LANES):
        scr[h] = x[:, h * LANES:(h + 1) * LANES]
    return jnp.concatenate([scr[h, pl.ds(r, tm // dil, stride=dil), :]
                            for r in range(dil) for h in range(c // LANES)], axis=1)


def _classes_to_rows(ref, dil, scr):
    if dil == 1:
        return ref[...]
    n, c = ref.shape[0], ref.shape[1] // dil
    for r in range(dil):
        for h in range(c // LANES):
            scr[h, pl.ds(r, n, stride=dil), :] = ref[:, r * c + h * LANES:r * c + (h + 1) * LANES]
    return jnp.concatenate([scr[h] for h in range(c // LANES)], axis=1)


def _proj_even_kernel(x_ref, g_ref, win_ref, cqg_ref, wuq_ref, ckvg_ref, wkv_ref,
                      cos_ref, sin_ref,
                      qm_ref, km_ref, vmt_ref, dq_ref, dk_ref, dvt_ref, kn_ref):
    h = _rms(x_ref[...], g_ref[...]).astype(BF16)
    proj = _dot(h, win_ref[...])
    nq = MLA_HEADS * LANES
    cq = _rms(proj[:, 0:MLA_Q_RANK], cqg_ref[...]).astype(BF16)
    qq = _dot(cq, wuq_ref[...])
    cosk, sink = cos_ref[...], sin_ref[...]
    rope_lane = lax.broadcasted_iota(jnp.int32, cosk.shape, 1) >= MLA_NOPE
    cosq, sinq = jnp.where(rope_lane, cosk * MLA_SCALE, MLA_SCALE), sink * MLA_SCALE
    for hh in range(MLA_HEADS):
        a = qq[:, hh * LANES:(hh + 1) * LANES]
        b = qq[:, nq + hh * LANES:nq + (hh + 1) * LANES]
        qm_ref[:, hh * LANES:(hh + 1) * LANES] = (a * cosq + b * sinq).astype(BF16)
    o = MLA_Q_RANK
    ckv = _rms(proj[:, o:o + MLA_KV_RANK], ckvg_ref[...]).astype(BF16)
    kv = _dot(ckv, wkv_ref[...])
    o += MLA_KV_RANK
    krc = proj[:, o:o + LANES] * cosk + proj[:, o + LANES:o + 2 * LANES] * sink
    for hh in range(MLA_HEADS):
        km_ref[:, hh * LANES:(hh + 1) * LANES] = (kv[:, hh * LANES:(hh + 1) * LANES] + krc).astype(BF16)
    _store_transposed_tiles(vmt_ref, kv[:, nq:nq + MLA_HEADS * MLA_V], MLA_HEADS, MLA_V, BF16_ROWS)
    o += 2 * LANES
    nd = DIFF_HEADS * 2 * DIFF_QK
    dq_ref[...] = (proj[:, o:o + nd] * (DIFF_QK ** -0.5 * LOG2E)).astype(BF16)
    o += nd
    lane = lax.broadcasted_iota(jnp.int32, (x_ref.shape[0], LANES), 1)
    norms = []
    for hh in range(DIFF_HEADS):
        kb = proj[:, o + hh * LANES:o + (hh + 1) * LANES].astype(BF16)
        dk_ref[:, hh * LANES:(hh + 1) * LANES] = kb
        kk = kb.astype(F32) * kb.astype(F32)
        for first in (True, False):
            part = jnp.where((lane < DIFF_QK) == first, kk, 0.0)
            top = jnp.max(jnp.sum(part, axis=1, keepdims=True), axis=0, keepdims=True)
            norms.append(jnp.broadcast_to(top, (1, LANES)))
    kn_ref[0] = jnp.concatenate(norms, axis=0)
    o += nd
    _store_transposed_tiles(dvt_ref, proj[:, o:o + DIFF_HEADS * DIFF_V], DIFF_HEADS, DIFF_V, BF16_ROWS)


def _proj_even(x, g, win, cqg, wuq, ckvg, wkv, cos_t, sin_t):
    s = x.shape[0]
    tm, tk = ROW_TILE, ATT_TK
    row = lambda c: pl.BlockSpec((tm, c), lambda i: (i, 0))
    tile_t = lambda r: pl.BlockSpec((tm // tk, r, tk), lambda i: (i, 0, 0))
    mla_rows = MLA_HEADS * (MLA_V + BF16_ROWS)
    diff_rows = DIFF_HEADS * (DIFF_V + BF16_ROWS)
    flat = lambda c: jax.ShapeDtypeStruct((s, c), BF16)
    tiled = lambda r: jax.ShapeDtypeStruct((s // tk, r, tk), BF16)
    return pl.pallas_call(
        _proj_even_kernel,
        grid=(s // tm,),
        in_specs=[row(D_MODEL), _const_spec(g.shape), _const_spec(win.shape), _const_spec(cqg.shape),
                  _const_spec(wuq.shape), _const_spec(ckvg.shape), _const_spec(wkv.shape),
                  row(LANES), row(LANES)],
        out_specs=[row(MLA_HEADS * LANES), row(MLA_HEADS * LANES), tile_t(mla_rows),
                   row(DIFF_HEADS * 2 * DIFF_QK), row(DIFF_HEADS * 2 * DIFF_QK), tile_t(diff_rows),
                   pl.BlockSpec((1, 2 * DIFF_HEADS, LANES), lambda i: (i, 0, 0))],
        out_shape=[flat(MLA_HEADS * LANES), flat(MLA_HEADS * LANES), tiled(mla_rows),
                   flat(DIFF_HEADS * 2 * DIFF_QK), flat(DIFF_HEADS * 2 * DIFF_QK), tiled(diff_rows),
                   jax.ShapeDtypeStruct((s // tm, 2 * DIFF_HEADS, LANES), F32)],
        compiler_params=_params(("parallel",)),
        name="proj_even",
    )(x, g, win, cqg, wuq, ckvg, wkv, cos_t, sin_t)


def _flash_cols(chains, i, rows, tile_bias=None, scratch=None, reach=None):
    tq, tk = ATT_TQ, ATT_TK
    ratio = tq // tk
    n_full = i * ratio

    def safe_step(j, carry):
        out = []
        for (qh, load_k, load_vt), (m, acc) in zip(chains, carry):
            s = _dot_t(load_k(j), qh)
            mb = jnp.max(s, axis=0, keepdims=True)
            if tile_bias is not None:
                c = tile_bias(j)
                mb = mb + c
            m_new = jnp.maximum(m, mb)
            alpha = jnp.exp2(m - m_new)
            shift = m_new if tile_bias is None else m_new - c
            p = jnp.exp2(s - shift).astype(BF16)
            out.append((m_new, alpha * acc + _dot(load_vt(j), p)))
        return tuple(out)

    def fast_step(n, carry):
        j = n_full - 1 - n
        out = []
        for ci, ((qh, load_k, load_vt), (m, alpha, acc, jump)) in enumerate(zip(chains, carry)):
            acc = (acc + _dot(load_vt(j + 1), scratch[ci])) * alpha
            s = _dot_t(load_k(j), qh)
            c = None if tile_bias is None else tile_bias(j)
            shift = m if c is None else m - c
            scratch[ci] = jnp.exp2(s - shift).astype(BF16)
            mb = jnp.max(s, axis=0, keepdims=True)
            if c is not None:
                mb = mb + c
            m_new = jnp.maximum(m, mb)
            out.append((m_new, jnp.exp2(m - m_new), acc, jnp.maximum(jump, mb - m)))
        return tuple(out)

    def fast_group(g, c):
        for t in range(MAIN_UNROLL):
            c = fast_step(g * MAIN_UNROLL + t, c)
        return c

    def diag_step(d, state):
        tri = lax.broadcasted_iota(jnp.int32, (tk, tk), 0) <= lax.broadcasted_iota(jnp.int32, (tk, tk), 1)
        out = []
        for (qh, load_k, load_vt), (ms, accs) in zip(chains, state):
            s = _dot_t(load_k(n_full + d), qh[d * tk:, :])
            vt = load_vt(n_full + d)
            c = None if tile_bias is None else tile_bias(n_full + d)
            ms, accs = list(ms), list(accs)
            for blk in range(d, ratio):
                sb = s[:, (blk - d) * tk:(blk - d + 1) * tk]
                if blk == d:
                    sb = jnp.where(tri, sb, NEG)
                mb = jnp.max(sb, axis=0, keepdims=True)
                if c is not None:
                    mb = mb + c
                m_new = jnp.maximum(ms[blk], mb)
                shift = m_new if c is None else m_new - c
                p = jnp.exp2(sb - shift).astype(BF16)
                accs[blk] = jnp.exp2(ms[blk] - m_new) * accs[blk] + _dot(vt, p)
                ms[blk] = m_new
            out.append((ms, accs))
        return out

    state = [([jnp.full((1, tk), NEG, F32)] * ratio, [jnp.zeros((rows, tk), F32)] * ratio) for _ in chains]
    for d in range(ratio):
        state = diag_step(d, state)
    diag = tuple((jnp.concatenate(ms, axis=1), jnp.concatenate(accs, axis=1)) for ms, accs in state)
    for ci in range(len(chains)):
        scratch[ci] = jnp.zeros((tk, tq), BF16)
    one, low = jnp.ones((1, tq), F32), jnp.full((1, tq), NEG, F32)
    groups = n_full // MAIN_UNROLL
    start = tuple((m, one, acc, low) for m, acc in diag)
    if reach is None:
        done, fast = groups, lax.fori_loop(0, groups, fast_group, start)
    else:
        def live(state):
            g, c = state
            nxt = tile_bias(n_full - 1 - g * MAIN_UNROLL)
            gap = functools.reduce(jnp.maximum, [jnp.max(b + nxt) - jnp.min(m)
                                                 for b, (m, _, _, _) in zip(reach, c)])
            return jnp.logical_and(g < groups, gap > UNDERFLOW_LOG2)

        done, fast = lax.while_loop(live, lambda st: (st[0] + 1, fast_group(st[0], st[1])), (jnp.int32(0), start))
    last = n_full - done * MAIN_UNROLL
    accs = [(acc + _dot(load_vt(last), scratch[ci])) * alpha
            for ci, ((_, _, load_vt), (_, alpha, acc, _)) in enumerate(zip(chains, fast))]
    worst = functools.reduce(jnp.maximum, [jnp.max(jump) for _, _, _, jump in fast])

    def redo():
        carry = lax.fori_loop(0, n_full, safe_step, diag)
        return [acc for _, acc in carry]

    return lax.cond(worst > MAX_JUMP, redo, lambda: accs)


def _k_tile(k_ref, j, lanes=slice(None)):
    return k_ref[pl.ds(pl.multiple_of(j * ATT_TK, ATT_TK), ATT_TK), lanes]


def _mla_kernel(q_ref, k_ref, vt_ref, o_ref, p_ref):
    i = pl.program_id(1)
    rows = MLA_V + BF16_ROWS
    def chain(hh):
        lanes = slice(hh * LANES, (hh + 1) * LANES)
        return (q_ref[:, lanes], lambda j: _k_tile(k_ref, j, lanes),
                lambda j: vt_ref[j, hh * rows:(hh + 1) * rows, :])

    accs = _flash_cols([chain(0), chain(1)], i, rows, scratch=p_ref)
    halves = [acc[0:MLA_V] / acc[MLA_V:MLA_V + 1] for acc in accs]
    o_ref[...] = jnp.concatenate(halves, axis=0).T.astype(BF16)


def _mla_attention(qm, km, vmt):
    s = qm.shape[0]
    tq = ATT_TQ
    rows = 2 * (MLA_V + BF16_ROWS)
    return pl.pallas_call(
        _mla_kernel,
        grid=(MLA_HEADS // 2, s // tq),
        in_specs=[pl.BlockSpec((tq, 2 * LANES), lambda p, i: (i, p)),
                  pl.BlockSpec((s, 2 * LANES), lambda p, i: (0, p)),
                  pl.BlockSpec((vmt.shape[0], rows, ATT_TK), lambda p, i: (0, p, 0))],
        out_specs=pl.BlockSpec((tq, LANES), lambda p, i: (i, p)),
        out_shape=jax.ShapeDtypeStruct((s, MLA_HEADS * MLA_V), BF16),
        scratch_shapes=[pltpu.VMEM((2, ATT_TK, tq), BF16)],
        compiler_params=_params(("parallel", "parallel")),
        name="mla_attn",
    )(qm, km, vmt)


def _diff_kernel(lam_ref, slope_ref, kn_ref, qf_ref, pos_ref, g_ref, q_ref, k_ref, vt_ref, o_ref, p_ref, *, lam_init):
    i = pl.program_id(1)
    tq = ATT_TQ
    lane = lax.broadcasted_iota(jnp.int32, (tq, LANES), 1)
    qf32 = q_ref[...].astype(F32)
    feat = qf_ref[...]
    zero = jnp.zeros_like(qf32)
    qa = jnp.concatenate([jnp.where(lane < DIFF_QK, qf32, zero).astype(BF16), feat], axis=1)
    qb = jnp.concatenate([jnp.where(lane >= DIFF_QK, qf32, zero).astype(BF16), feat], axis=1)
    slope = slope_ref[0:1, 0:1]
    rows = DIFF_V + BF16_ROWS
    tile_bias = lambda j: slope * (j * ATT_TK - i * tq).astype(F32)
    pos = pos_ref[...]
    load_k = lambda j: jnp.concatenate([_k_tile(k_ref, j), pos], axis=1)
    load_vt = lambda j: vt_ref[j]
    q2 = qf32 * qf32
    top = lambda z: jnp.max(jnp.sum(z, axis=1, keepdims=True), axis=0, keepdims=True)
    qn = (top(jnp.where(lane < DIFF_QK, q2, zero)), top(jnp.where(lane >= DIFF_QK, q2, zero)))
    reach = [jnp.sqrt(qn[x] * kn_ref[x:x + 1, 0:1]) * REACH_MARGIN + slope * ATT_TK for x in range(2)]
    acc1, acc2 = _flash_cols([(qa, load_k, load_vt), (qb, load_k, load_vt)], i, rows, tile_bias, p_ref, reach)
    lp = lam_ref[...]
    s1 = jnp.sum(lp[0:1, :] * lp[1:2, :], axis=1, keepdims=True)
    s2 = jnp.sum(lp[2:3, :] * lp[3:4, :], axis=1, keepdims=True)
    lam = jnp.exp(s1) - jnp.exp(s2) + lam_init
    ot = acc1[0:DIFF_V] / acc1[DIFF_V:DIFF_V + 1] - lam * (acc2[0:DIFF_V] / acc2[DIFF_V:DIFF_V + 1])
    o_ref[...] = (_rms(ot.T, g_ref[...]) * (1.0 - lam_init)).astype(BF16)


def _diff_attention(lam_p, slope_arr, knorm, qfeat, pos, subln_g, dq, dk, dvt, lam_init):
    s = dq.shape[0]
    tq = ATT_TQ
    return pl.pallas_call(
        functools.partial(_diff_kernel, lam_init=lam_init),
        grid=(DIFF_HEADS, s // tq),
        in_specs=[pl.BlockSpec(lam_p.shape, lambda h, i: (0, 0)),
                  pl.BlockSpec((None, SUBLANES, LANES), lambda h, i: (h, 0, 0)),
                  pl.BlockSpec((None, SUBLANES, LANES), lambda h, i: (h, 0, 0)),
                  pl.BlockSpec((None, tq, LANES), lambda h, i: (h, 0, 0)),
                  pl.BlockSpec(pos.shape, lambda h, i: (0, 0)),
                  pl.BlockSpec(subln_g.shape, lambda h, i: (0, 0)),
                  pl.BlockSpec((tq, LANES), lambda h, i: (i, h)),
                  pl.BlockSpec((s, LANES), lambda h, i: (0, h)),
                  pl.BlockSpec((dvt.shape[0], DIFF_V + BF16_ROWS, ATT_TK), lambda h, i: (0, h, 0))],
        out_specs=pl.BlockSpec((tq, LANES), lambda h, i: (i, h)),
        out_shape=jax.ShapeDtypeStruct((s, DIFF_HEADS * DIFF_V), BF16),
        scratch_shapes=[pltpu.VMEM((2, ATT_TK, tq), BF16)],
        compiler_params=_params(("parallel", "parallel")),
        name="diff_attn",
    )(lam_p, slope_arr, knorm, qfeat, pos, subln_g, dq, dk, dvt)


def _shift_rows(u, k, prev):
    top = jnp.where(lax.broadcasted_iota(jnp.int32, prev.shape, 0) < k,
                    pltpu.roll(prev, k, 0), pltpu.roll(u[0:SUBLANES, :], k, 0))
    return jnp.concatenate([top, pltpu.roll(u, k, 0)[SUBLANES:, :]], axis=0)


def _ffn_tile(x1, fpre, wup_ref, cw_ref, cb_ref, wdn_ref, fpost, carry_ref):
    tm = x1.shape[0]
    hf = _rms(x1, fpre).astype(BF16)
    acts = []
    lo = 0
    for width in FF_CHUNKS:
        ys = []
        for part in range(2):
            c0 = part * D_FF + lo
            u = _dot(hf, wup_ref[:, c0:c0 + width])
            prev = carry_ref[:, c0:c0 + width]
            carry_ref[:, c0:c0 + width] = u[tm - SUBLANES:, :]
            ys.append(cw_ref[2:3, c0:c0 + width] * u
                      + cw_ref[1:2, c0:c0 + width] * _shift_rows(u, 1, prev)
                      + cw_ref[0:1, c0:c0 + width] * _shift_rows(u, 2, prev)
                      + cb_ref[:, c0:c0 + width])
        gate, up = ys
        acts.append((gate * (1.0 / (1.0 + jnp.exp(-gate))) * up).astype(BF16))
        lo += width
    acc = _dot(jnp.concatenate(acts, axis=1), wdn_ref[...])
    return x1 + _rms(acc, fpost)


def _out_ffn_even_kernel(x_ref, a_ref, b_ref, wout_ref, postg_ref, fpre_ref, wup_ref, cw_ref, cb_ref,
                         wdn_ref, fpost_ref, o_ref, carry_ref):
    @pl.when(pl.program_id(0) == 0)
    def _():
        carry_ref[...] = jnp.zeros_like(carry_ref)

    na = a_ref.shape[1]
    mix = _dot(a_ref[...], wout_ref[0:na, :]) + _dot(b_ref[...], wout_ref[na:, :])
    x1 = x_ref[...] + _rms(mix, postg_ref[...])
    o_ref[...] = _ffn_tile(x1, fpre_ref[...], wup_ref, cw_ref, cb_ref, wdn_ref, fpost_ref[...], carry_ref)


def _out_ffn_odd_kernel(x_ref, o0_ref, l0_ref, o1_ref, l1_ref, o2_ref, l2_ref, bt_ref, wout_ref, postg_ref,
                        fpre_ref, wup_ref, cw_ref, cb_ref, wdn_ref, fpost_ref, o_ref, carry_ref, scr):
    @pl.when(pl.program_id(0) == 0)
    def _():
        carry_ref[...] = jnp.zeros_like(carry_ref)

    (l0, o0), (l1, o1), (l2, o2) = [(_classes_to_rows(l_ref, d, scr), _classes_to_rows(og_ref, d, scr))
                                    for (l_ref, og_ref), (_, d) in zip(((l0_ref, o0_ref), (l1_ref, o1_ref),
                                                                        (l2_ref, o2_ref)), DIL_CONFIGS)]
    m = jnp.maximum(jnp.maximum(l0, l1), l2)
    e0, e1, e2 = jnp.exp(l0 - m), jnp.exp(l1 - m), jnp.exp(l2 - m)
    dil = (e0 * o0 + e1 * o1 + e2 * o2) / (e0 + e1 + e2)
    na = dil.shape[1]
    sb = bt_ref[...].T
    mix = _dot(dil.astype(BF16), wout_ref[0:na, :]) + _dot(sb.astype(BF16), wout_ref[na:, :])
    x1 = x_ref[...] + _rms(mix, postg_ref[...])
    o_ref[...] = _ffn_tile(x1, fpre_ref[...], wup_ref, cw_ref, cb_ref, wdn_ref, fpost_ref[...], carry_ref)


def _out_ffn(kernel, name, x, acts, acts_t, consts, scratch=()):
    s = x.shape[0]
    tm = ROW_TILE
    row = lambda c, n=s: pl.BlockSpec((tm * n // s, c), lambda i: (i, 0))
    col = lambda r: pl.BlockSpec((r, tm), lambda i: (0, i))
    return pl.pallas_call(
        kernel,
        grid=(s // tm,),
        in_specs=([row(D_MODEL)] + [row(a.shape[1], a.shape[0]) for a in acts] + [col(a.shape[0]) for a in acts_t]
                  + [_const_spec(c.shape) for c in consts]),
        out_specs=row(D_MODEL),
        out_shape=jax.ShapeDtypeStruct((s, D_MODEL), F32),
        scratch_shapes=[pltpu.VMEM((SUBLANES, 2 * D_FF), F32), *scratch],
        compiler_params=_params(("arbitrary",)),
        name=name,
    )(x, *acts, *acts_t, *consts)


def _proj_odd_kernel(x_ref, g_ref, win_ref, *refs):
    out_refs, scr = refs[:-1], refs[-1]
    h = _rms(x_ref[...], g_ref[...]).astype(BF16)
    proj = _dot(h, win_ref[...])
    c = DIL_HEADS_PER_GROUP * HEAD_DIM
    scale = HEAD_DIM ** -0.5
    for n, ref in enumerate(out_refs[:-1]):
        blk = proj[:, n * c:(n + 1) * c]
        if n < 3 or n == 9:
            blk = blk * scale
        if n < 9:
            blk = _rows_to_classes(blk, DIL_CONFIGS[n % 3][1], scr)
        ref[...] = blk.astype(BF16)
    _store_transposed_tiles(out_refs[-1], proj[:, 11 * c:12 * c], SB_HEADS, HEAD_DIM, 0)


def _proj_odd(x, g, win):
    s = x.shape[0]
    tm, tk = ROW_TILE, ATT_TK
    c = DIL_HEADS_PER_GROUP * HEAD_DIM
    row = lambda w, d=1: pl.BlockSpec((tm // d, d * w), lambda i: (i, 0))
    return pl.pallas_call(
        _proj_odd_kernel,
        grid=(s // tm,),
        in_specs=[row(D_MODEL), _const_spec(g.shape), _const_spec(win.shape)],
        out_specs=([row(c, d) for _ in range(3) for _, d in DIL_CONFIGS] + [row(c)] * 2
                   + [pl.BlockSpec((tm // tk, c, tk), lambda i: (i, 0, 0))]),
        out_shape=([jax.ShapeDtypeStruct((s // d, d * c), BF16) for _ in range(3) for _, d in DIL_CONFIGS]
                   + [jax.ShapeDtypeStruct((s, c), BF16)] * 2 + [jax.ShapeDtypeStruct((s // tk, c, tk), BF16)]),
        scratch_shapes=[pltpu.VMEM((c // LANES, tm, LANES), F32)],
        compiler_params=_params(("parallel",)),
        name="proj_odd",
    )(x, g, win)


def _dilated_kernel(bias_ref, q_ref, kp_ref, kc_ref, vp_ref, vc_ref, o_ref, lse_ref, *, nub):
    t, tb = DIL_T, DIL_TB
    ub = pl.program_id(0) % nub
    c = DIL_HEADS_PER_GROUP * HEAD_DIM
    qf = q_ref[...].astype(F32)
    kcat = jnp.concatenate([kp_ref[...], kc_ref[...]], axis=0)
    vcat = jnp.concatenate([vp_ref[...], vc_ref[...]], axis=0)
    lane = lax.broadcasted_iota(jnp.int32, (tb, c), 1)
    col = lax.broadcasted_iota(jnp.int32, (tb, t + tb), 1)
    no_prev = col < jnp.where(ub == 0, t, 0)
    out = jnp.zeros((tb, c), F32)
    lse = jnp.zeros((tb, c), F32)
    for hh in range(DIL_HEADS_PER_GROUP):
        mine = jnp.logical_and(lane >= hh * HEAD_DIM, lane < (hh + 1) * HEAD_DIM)
        qh = jnp.where(mine, qf, 0.0).astype(BF16)
        s = _dot_t(qh, kcat) + bias_ref[hh]
        s = jnp.where(no_prev, NEG, s)
        m = jnp.max(s, axis=1, keepdims=True)
        e = jnp.exp(s - m)
        den = jnp.sum(e, axis=1, keepdims=True)
        oh = _dot(e.astype(BF16), vcat) / den
        out = jnp.where(mine, oh, out)
        lse = jnp.where(mine, m + jnp.log(den), lse)
    o_ref[...] = out
    lse_ref[...] = lse


def _dilated_group(q, k, v, dil, slopes):
    c = q.shape[1] // dil
    s = q.shape[0] * dil
    t, tb = DIL_T, DIL_TB
    nub = s // dil // tb
    a = np.arange(tb)[:, None]
    cc = np.arange(t + tb)[None, :]
    steps = t + a - cc
    ok = (steps >= 0) & (steps <= t)
    dist = (steps * dil).astype(np.float32)
    bias = np.where(ok[None], -np.asarray(slopes, np.float32)[:, None, None] * dist[None], np.float32(NEG))
    bias = jnp.asarray(bias, F32)
    cur = pl.BlockSpec((tb, c), lambda b: (b % nub, b // nub))
    prev = pl.BlockSpec((t, c), lambda b: (jnp.maximum(b % nub * (tb // t) - 1, 0), b // nub))
    o, lse = pl.pallas_call(
        functools.partial(_dilated_kernel, nub=nub),
        grid=(dil * nub,),
        in_specs=[_const_spec(bias.shape), cur, prev, cur, prev, cur],
        out_specs=[cur, cur],
        out_shape=[jax.ShapeDtypeStruct((s // dil, dil * c), F32)] * 2,
        compiler_params=_params(("parallel",)),
        name=f"dilated_d{dil}",
    )(bias, q, k, k, v, v)
    return o, lse


def _sb_kernel(tri_ref, q_ref, k_ref, vt_ref, o_ref):
    hp = pl.program_id(0)
    i = pl.program_id(1)
    tq, tk = SB_TQ, ATT_TK
    ratio = tq // tk
    c = SB_HEADS * HEAD_DIM
    tri = tri_ref[...]
    lane = lax.broadcasted_iota(jnp.int32, (tq, c), 1)
    qf = q_ref[...].astype(F32)
    qhs = []
    for t in range(SB_GROUP):
        hh = hp * SB_GROUP + t
        mine = jnp.logical_and(lane >= hh * HEAD_DIM, lane < (hh + 1) * HEAD_DIM)
        qhs.append(jnp.where(mine, qf, 0.0).astype(BF16))
    strict = lax.broadcasted_iota(jnp.int32, (tk, tk), 0) < lax.broadcasted_iota(jnp.int32, (tk, tk), 1)
    n_full = i * ratio

    def weights(z, r, masked):
        sp = jnp.log1p(jnp.exp(-jnp.abs(z)))
        log_beta = jnp.minimum(z, 0.0) - sp
        log_keep = log_beta - z
        if masked:
            log_keep = jnp.where(strict, log_keep, 0.0)
        hi = log_keep.astype(BF16)
        lo = (log_keep - hi.astype(F32)).astype(BF16)
        aft = _dot(tri, jnp.concatenate([hi, lo], axis=0))
        a = jnp.exp(log_beta + aft[0:tk] + r)
        if masked:
            a = jnp.where(strict, a, 0.0)
        return a.astype(BF16), r + aft[tk:tk + 1]

    def diag_step(d, state):
        kt = _k_tile(k_ref, n_full + d)
        vt = vt_ref[n_full + d]
        out = []
        for t, (rs, accs) in enumerate(state):
            z = _dot_t(kt, qhs[t][d * tk:, :])
            rs, accs = list(rs), list(accs)
            for blk in range(d, ratio):
                a, rs[blk] = weights(z[:, (blk - d) * tk:(blk - d + 1) * tk], rs[blk], blk == d)
                accs[blk] = accs[blk] + _dot(vt[t * HEAD_DIM:(t + 1) * HEAD_DIM, :], a)
            out.append((rs, accs))
        return out

    state = [([jnp.zeros((1, tk), F32)] * ratio, [jnp.zeros((HEAD_DIM, tk), F32)] * ratio) for _ in range(SB_GROUP)]
    for d in reversed(range(ratio)):
        state = diag_step(d, state)
    carry = tuple((jnp.concatenate(rs, axis=1), jnp.concatenate(accs, axis=1)) for rs, accs in state)

    def step(j, cr):
        kt = _k_tile(k_ref, j)
        vt = vt_ref[j]
        out = []
        for t, (r, acc) in enumerate(cr):
            a, r = weights(_dot_t(kt, qhs[t]), r, False)
            out.append((r, acc + _dot(vt[t * HEAD_DIM:(t + 1) * HEAD_DIM, :], a)))
        return tuple(out)

    def more(state):
        n, cr = state
        top = functools.reduce(jnp.maximum, [jnp.max(r) for r, _ in cr])
        return jnp.logical_and(n < n_full, top > SB_STOP)

    def walk(state):
        n, cr = state
        return n + 1, step(n_full - 1 - n, cr)

    _, carry = lax.while_loop(more, walk, (jnp.int32(0), carry))
    o_ref[...] = jnp.concatenate([acc for _, acc in carry], axis=0)


def _stickbreak(q, k, vt):
    s, c = q.shape
    tq, tk = SB_TQ, ATT_TK
    j = np.arange(tk)
    later = (j[None, :] > j[:, None]).astype(np.float32)
    tri = np.concatenate([np.concatenate([later, later], axis=1), np.ones((BF16_ROWS, 2 * tk), np.float32)], axis=0)
    tri = jnp.asarray(tri, BF16)
    return pl.pallas_call(
        _sb_kernel,
        grid=(SB_HEADS // SB_GROUP, s // tq),
        in_specs=[_const_spec(tri.shape),
                  pl.BlockSpec((tq, c), lambda h, i: (i, 0)),
                  _const_spec((s, c)),
                  pl.BlockSpec((s // tk, SB_GROUP * HEAD_DIM, tk), lambda h, i: (0, h, 0))],
        out_specs=pl.BlockSpec((SB_GROUP * HEAD_DIM, tq), lambda h, i: (h, i)),
        out_shape=jax.ShapeDtypeStruct((c, s), F32),
        compiler_params=_params(("parallel", "parallel")),
        name="stickbreak",
    )(tri, q, k, vt)


def _alibi_slopes(n):
    return 2.0 ** (-8.0 * np.arange(1, n + 1) / n)


def _pad_cols(w, lo, width):
    return jnp.pad(w, ((0, 0), (lo, width - lo - w.shape[1])))


def _rot_half_cols(w):
    half = w.shape[1] // 2
    return jnp.concatenate([-w[:, half:], w[:, :half]], axis=1)


def _even_weights(w_in, w_uq, w_ukv):
    o_kr = MLA_Q_RANK + MLA_KV_RANK
    w_kr = w_in[:, o_kr:o_kr + MLA_ROPE]
    win = jnp.concatenate([w_in[:, :o_kr], _pad_cols(w_kr, MLA_NOPE, LANES),
                           _pad_cols(_rot_half_cols(w_kr), MLA_NOPE, LANES),
                           w_in[:, o_kr + MLA_ROPE:]], axis=1)
    hd = MLA_NOPE + MLA_ROPE
    q_plain, q_rot = [], []
    for hh in range(MLA_HEADS):
        wh = w_uq[:, hh * hd:(hh + 1) * hd]
        q_plain.append(_pad_cols(wh, 0, LANES))
        q_rot.append(_pad_cols(_rot_half_cols(wh[:, MLA_NOPE:]), MLA_NOPE, LANES))
    wuq = jnp.concatenate(q_plain + q_rot, axis=1)
    hk = MLA_NOPE + MLA_V
    k_cols = [_pad_cols(w_ukv[:, hh * hk:hh * hk + MLA_NOPE], 0, LANES) for hh in range(MLA_HEADS)]
    v_cols = [w_ukv[:, hh * hk + MLA_NOPE:(hh + 1) * hk] for hh in range(MLA_HEADS)]
    wkv = jnp.concatenate(k_cols + v_cols, axis=1)
    return win.astype(BF16), wuq.astype(BF16), wkv.astype(BF16)


def _rope_tables(s):
    half = MLA_ROPE // 2
    inv = ROPE_BASE ** (-jnp.arange(half, dtype=F32) / half)
    ang = jnp.arange(s).astype(F32)[:, None] * inv
    pads = ((0, 0), (MLA_NOPE, LANES - MLA_NOPE - MLA_ROPE))
    return jnp.pad(jnp.tile(jnp.cos(ang), (1, 2)), pads), jnp.pad(jnp.tile(jnp.sin(ang), (1, 2)), pads)


def _row2d(v):
    return v.reshape(1, -1).astype(F32)


def kernel(x, ev_pre_g, ev_w_in, ev_cq_g, ev_w_uq, ev_ckv_g, ev_w_ukv, ev_lam_q1, ev_lam_k1, ev_lam_q2,
           ev_lam_k2, ev_subln_g, ev_w_out, ev_post_g, od_pre_g, od_w_in, od_w_out, od_post_g, ffn_pre_g,
           ffn_w_up, ffn_conv_w, ffn_conv_b, ffn_w_down, ffn_post_g):
    b, s, _ = x.shape
    assert b == 1 and s % max(ROW_TILE, ATT_TQ, SB_TQ, DIL_TB * DIL_CONFIGS[-1][1]) == 0
    assert ROW_TILE % ATT_TK == 0 and ATT_TQ % ATT_TK == 0 and SB_TQ % ATT_TK == 0
    assert (ATT_TQ // ATT_TK) % MAIN_UNROLL == 0 and DIL_TB % DIL_T == 0
    assert sum(FF_CHUNKS) == D_FF
    assert ATT_TK <= 256
    xs = x[0]

    def ffn_consts(i):
        return [_row2d(ffn_pre_g[i]), ffn_w_up[i].astype(BF16), ffn_conv_w[i].astype(F32),
                _row2d(ffn_conv_b[i]), ffn_w_down[i].astype(BF16), _row2d(ffn_post_g[i])]

    win, wuq, wkv = _even_weights(ev_w_in[0], ev_w_uq[0], ev_w_ukv[0])
    cos_t, sin_t = _rope_tables(s)
    pos_col = np.arange(ATT_TK, dtype=np.float32)[:, None]
    pos_tile = jnp.asarray(np.pad(np.tile(pos_col, (1, SLOPE_PARTS)), ((0, 0), (0, LANES - SLOPE_PARTS))), BF16)
    qm, km, vmt, dq, dk, dvt, kn_tiles = _proj_even(xs, _row2d(ev_pre_g[0]), win, _row2d(ev_cq_g[0]), wuq,
                                          _row2d(ev_ckv_g[0]), wkv, cos_t, sin_t)
    o_mla = _mla_attention(qm, km, vmt)
    lam_p = jnp.pad(jnp.stack([ev_lam_q1[0], ev_lam_k1[0], ev_lam_q2[0], ev_lam_k2[0]]).astype(F32),
                    ((0, SUBLANES - 4), (0, LANES - DIFF_QK)))
    d_slopes = _alibi_slopes(DIFF_HEADS) * LOG2E
    slope_arr = jnp.asarray(np.broadcast_to(d_slopes[:, None, None], (DIFF_HEADS, SUBLANES, LANES)), F32)
    qfeat = jnp.zeros((DIFF_HEADS, ATT_TQ, LANES), F32)
    rest = jnp.asarray(d_slopes, F32)
    for part in range(SLOPE_PARTS):
        piece = rest.astype(BF16).astype(F32)
        qfeat = qfeat.at[:, :, part].set(piece[:, None])
        rest = rest - piece
    lam_init = 0.8 - 0.6 * math.exp(-0.3 * 0)
    knorm = jnp.pad(jnp.max(kn_tiles, axis=0).reshape(DIFF_HEADS, 2, LANES), ((0, 0), (0, SUBLANES - 2), (0, 0)))
    o_diff = _diff_attention(lam_p, slope_arr, knorm, qfeat.astype(BF16), pos_tile, _row2d(ev_subln_g[0]),
                             dq, dk, dvt, lam_init)
    x2 = _out_ffn(_out_ffn_even_kernel, "out_ffn_even", xs, [o_mla, o_diff], [],
                  [ev_w_out[0].astype(BF16), _row2d(ev_post_g[0])] + ffn_consts(0))

    p = _proj_odd(x2, _row2d(od_pre_g[0]), od_w_in[0].astype(BF16))
    slopes = _alibi_slopes(len(DIL_CONFIGS) * DIL_HEADS_PER_GROUP).reshape(len(DIL_CONFIGS), -1)
    dil_outs = []
    for gi, (_, dil) in enumerate(DIL_CONFIGS):
        dil_outs += list(_dilated_group(p[gi], p[3 + gi], p[6 + gi], dil, slopes[gi]))
    o_sbt = _stickbreak(p[9], p[10], p[11])
    lane_tiles = DIL_HEADS_PER_GROUP * HEAD_DIM // LANES
    x4 = _out_ffn(_out_ffn_odd_kernel, "out_ffn_odd", x2, dil_outs, [o_sbt],
                  [od_w_out[0].astype(BF16), _row2d(od_post_g[0])] + ffn_consts(1),
                  scratch=[pltpu.VMEM((lane_tiles, ROW_TILE, LANES), F32)])
    return x4[None]
```

```python
import functools
import math

import numpy as np
import jax
import jax.numpy as jnp
from jax import lax
from jax.experimental import pallas as pl
from jax.experimental.pallas import tpu as pltpu

F32 = jnp.float32
BF16 = jnp.bfloat16

D_MODEL = 1024
NORM_EPS = 1e-6
MLA_HEADS = 8
MLA_Q_RANK = 256
MLA_KV_RANK = 128
MLA_NOPE = 64
MLA_ROPE = 32
MLA_V = 64
ROPE_BASE = 10000.0
DIFF_HEADS = 4
DIFF_QK = 64
DIFF_V = 128
DIL_CONFIGS = ((128, 1), (512, 4), (2048, 16))
DIL_HEADS_PER_GROUP = 4
HEAD_DIM = 64
SB_HEADS = 4
D_FF = 2816
CONV_WIDTH = 3

LANES = 128
SUBLANES = 8
BF16_ROWS = 16
VMEM_LIMIT_BYTES = 56 * 1024 * 1024
ROW_TILE = 512
ATT_TQ = 2048
MAIN_UNROLL = 8
SB_TQ = 512
SB_GROUP = 4
ATT_TK = 256
DIL_T = 128
DIL_TB = 512
FF_CHUNKS = (512,) * 5 + (256,)
NEG = -1e30
LOG2E = 1.4426950408889634
UNDERFLOW_LOG2 = -152.0
MAX_JUMP = 32.0
SB_STOP = -110.0
SLOPE_PARTS = 3

_TRANS_B = (((1,), (1,)), ((), ()))


def _dot(a, b):
    return jnp.dot(a, b, preferred_element_type=F32)


def _dot_t(a, b):
    return lax.dot_general(a, b, _TRANS_B, preferred_element_type=F32)


def _rms(xf, g):
    ms = jnp.mean(xf * xf, axis=-1, keepdims=True)
    return xf * lax.rsqrt(ms + NORM_EPS) * g


def _params(sem):
    return pltpu.CompilerParams(dimension_semantics=sem, vmem_limit_bytes=VMEM_LIMIT_BYTES)


def _const_spec(shape):
    nd = len(shape)
    return pl.BlockSpec(shape, lambda *_: (0,) * nd, pipeline_mode=pl.Buffered(1))


def _store_transposed_tiles(dst_ref, src, heads, width, ones_rows):
    tk = ATT_TK
    per = width + ones_rows
    for b in range(src.shape[0] // tk):
        vt = src[b * tk:(b + 1) * tk, :].T.astype(BF16)
        for hh in range(heads):
            dst_ref[b, hh * per:hh * per + width, :] = vt[hh * width:(hh + 1) * width, :]
            if ones_rows:
                dst_ref[b, hh * per + width:(hh + 1) * per, :] = jnp.ones((ones_rows, tk), BF16)


def _rows_to_classes(x, dil, scr):
    if dil == 1:
        return x
    tm, c = x.shape
    for h in range(c // LANES):
        scr[h] = x[:, h * LANES:(h + 1) * LANES]
    return jnp.concatenate([scr[h, pl.ds(r, tm // dil, stride=dil), :]
                            for r in range(dil) for h in range(c // LANES)], axis=1)


def _classes_to_rows(ref, dil, scr):
    if dil == 1:
        return ref[...]
    n, c = ref.shape[0], ref.shape[1] // dil
    for r in range(dil):
        for h in range(c // LANES):
            scr[h, pl.ds(r, n, stride=dil), :] = ref[:, r * c + h * LANES:r * c + (h + 1) * LANES]
    return jnp.concatenate([scr[h] for h in range(c // LANES)], axis=1)


def _proj_even_kernel(x_ref, g_ref, win_ref, cqg_ref, wuq_ref, ckvg_ref, wkv_ref,
                      cosq_ref, sinq_ref, cosk_ref, sink_ref, pos_ref,
                      qm_ref, km_ref, vmt_ref, dq_ref, dk_ref, dvt_ref, kn_ref):
    h = _rms(x_ref[...], g_ref[...]).astype(BF16)
    proj = _dot(h, win_ref[...])
    nq = MLA_HEADS * LANES
    cq = _rms(proj[:, 0:MLA_Q_RANK], cqg_ref[...]).astype(BF16)
    qq = _dot(cq, wuq_ref[...])
    cosq, sinq = cosq_ref[...], sinq_ref[...]
    for hh in range(MLA_HEADS):
        a = qq[:, hh * LANES:(hh + 1) * LANES]
        b = qq[:, nq + hh * LANES:nq + (hh + 1) * LANES]
        qm_ref[:, hh * LANES:(hh + 1) * LANES] = (a * cosq + b * sinq).astype(BF16)
    o = MLA_Q_RANK
    ckv = _rms(proj[:, o:o + MLA_KV_RANK], ckvg_ref[...]).astype(BF16)
    kv = _dot(ckv, wkv_ref[...])
    o += MLA_KV_RANK
    krc = proj[:, o:o + LANES] * cosk_ref[...] + proj[:, o + LANES:o + 2 * LANES] * sink_ref[...]
    for hh in range(MLA_HEADS):
        km_ref[:, hh * LANES:(hh + 1) * LANES] = (kv[:, hh * LANES:(hh + 1) * LANES] + krc).astype(BF16)
    _store_transposed_tiles(vmt_ref, kv[:, nq:nq + MLA_HEADS * MLA_V], MLA_HEADS, MLA_V, BF16_ROWS)
    o += 2 * LANES
    nd = DIFF_HEADS * 2 * DIFF_QK
    dq_ref[...] = (proj[:, o:o + nd] * (DIFF_QK ** -0.5 * LOG2E)).astype(BF16)
    o += nd
    pos = pos_ref[...]
    lane = lax.broadcasted_iota(jnp.int32, (x_ref.shape[0], LANES), 1)
    norms = []
    for hh in range(DIFF_HEADS):
        kb = proj[:, o + hh * LANES:o + (hh + 1) * LANES].astype(BF16)
        dk_ref[:, hh * 2 * LANES:hh * 2 * LANES + LANES] = kb
        dk_ref[:, hh * 2 * LANES + LANES:(hh + 1) * 2 * LANES] = pos
        kk = kb.astype(F32) * kb.astype(F32)
        for first in (True, False):
            part = jnp.where((lane < DIFF_QK) == first, kk, 0.0)
            top = jnp.max(jnp.sum(part, axis=1, keepdims=True), axis=0, keepdims=True)
            norms.append(jnp.broadcast_to(top, (1, LANES)))
    kn_ref[0] = jnp.concatenate(norms, axis=0)
    o += nd
    _store_transposed_tiles(dvt_ref, proj[:, o:o + DIFF_HEADS * DIFF_V], DIFF_HEADS, DIFF_V, BF16_ROWS)


def _proj_even(x, g, win, cqg, wuq, ckvg, wkv, cosq, sinq, cosk, sink, pos):
    s = x.shape[0]
    tm, tk = ROW_TILE, ATT_TK
    row = lambda c: pl.BlockSpec((tm, c), lambda i: (i, 0))
    tile_t = lambda r: pl.BlockSpec((tm // tk, r, tk), lambda i: (i, 0, 0))
    mla_rows = MLA_HEADS * (MLA_V + BF16_ROWS)
    diff_rows = DIFF_HEADS * (DIFF_V + BF16_ROWS)
    flat = lambda c: jax.ShapeDtypeStruct((s, c), BF16)
    tiled = lambda r: jax.ShapeDtypeStruct((s // tk, r, tk), BF16)
    return pl.pallas_call(
        _proj_even_kernel,
        grid=(s // tm,),
        in_specs=[row(D_MODEL), _const_spec(g.shape), _const_spec(win.shape), _const_spec(cqg.shape),
                  _const_spec(wuq.shape), _const_spec(ckvg.shape), _const_spec(wkv.shape),
                  row(LANES), row(LANES), row(LANES), row(LANES), _const_spec(pos.shape)],
        out_specs=[row(MLA_HEADS * LANES), row(MLA_HEADS * LANES), tile_t(mla_rows),
                   row(DIFF_HEADS * 2 * DIFF_QK), row(DIFF_HEADS * 2 * LANES), tile_t(diff_rows),
                   pl.BlockSpec((1, 2 * DIFF_HEADS, LANES), lambda i: (i, 0, 0))],
        out_shape=[flat(MLA_HEADS * LANES), flat(MLA_HEADS * LANES), tiled(mla_rows),
                   flat(DIFF_HEADS * 2 * DIFF_QK), flat(DIFF_HEADS * 2 * LANES), tiled(diff_rows),
                   jax.ShapeDtypeStruct((s // tm, 2 * DIFF_HEADS, LANES), F32)],
        compiler_params=_params(("parallel",)),
        name="proj_even",
    )(x, g, win, cqg, wuq, ckvg, wkv, cosq, sinq, cosk, sink, pos)


def _flash_cols(chains, i, rows, tile_bias=None, scratch=None, reach=None):
    tq, tk = ATT_TQ, ATT_TK
    ratio = tq // tk
    n_full = i * ratio

    def safe_step(j, carry):
        out = []
        for (qh, load_k, load_vt), (m, acc) in zip(chains, carry):
            s = _dot(load_k(j), qh)
            mb = jnp.max(s, axis=0, keepdims=True)
            if tile_bias is not None:
                c = tile_bias(j)
                mb = mb + c
            m_new = jnp.maximum(m, mb)
            alpha = jnp.exp2(m - m_new)
            shift = m_new if tile_bias is None else m_new - c
            p = jnp.exp2(s - shift).astype(BF16)
            out.append((m_new, alpha * acc + _dot(load_vt(j), p)))
        return tuple(out)

    def fast_step(n, carry):
        j = n_full - 1 - n
        out = []
        for ci, ((qh, load_k, load_vt), (m, alpha, acc, jump)) in enumerate(zip(chains, carry)):
            acc = (acc + _dot(load_vt(j + 1), scratch[ci])) * alpha
            s = _dot(load_k(j), qh)
            c = None if tile_bias is None else tile_bias(j)
            shift = m if c is None else m - c
            scratch[ci] = jnp.exp2(s - shift).astype(BF16)
            mb = jnp.max(s, axis=0, keepdims=True)
            if c is not None:
                mb = mb + c
            m_new = jnp.maximum(m, mb)
            out.append((m_new, jnp.exp2(m - m_new), acc, jnp.maximum(jump, mb - m)))
        return tuple(out)

    def fast_group(g, c):
        for t in range(MAIN_UNROLL):
            c = fast_step(g * MAIN_UNROLL + t, c)
        return c

    def diag_step(d, state):
        tri = lax.broadcasted_iota(jnp.int32, (tk, tk), 0) <= lax.broadcasted_iota(jnp.int32, (tk, tk), 1)
        out = []
        for (qh, load_k, load_vt), (ms, accs) in zip(chains, state):
            s = _dot(load_k(n_full + d), qh[:, d * tk:])
            vt = load_vt(n_full + d)
            c = None if tile_bias is None else tile_bias(n_full + d)
            ms, accs = list(ms), list(accs)
            for blk in range(d, ratio):
                sb = s[:, (blk - d) * tk:(blk - d + 1) * tk]
                if blk == d:
                    sb = jnp.where(tri, sb, NEG)
                mb = jnp.max(sb, axis=0, keepdims=True)
                if c is not None:
                    mb = mb + c
                m_new = jnp.maximum(ms[blk], mb)
                shift = m_new if c is None else m_new - c
                p = jnp.exp2(sb - shift).astype(BF16)
                accs[blk] = jnp.exp2(ms[blk] - m_new) * accs[blk] + _dot(vt, p)
                ms[blk] = m_new
            out.append((ms, accs))
        return out

    state = [([jnp.full((1, tk), NEG, F32)] * ratio, [jnp.zeros((rows, tk), F32)] * ratio) for _ in chains]
    for d in range(ratio):
        state = diag_step(d, state)
    diag = tuple((jnp.concatenate(ms, axis=1), jnp.concatenate(accs, axis=1)) for ms, accs in state)
    for ci in range(len(chains)):
        scratch[ci] = jnp.zeros((tk, tq), BF16)
    one, low = jnp.ones((1, tq), F32), jnp.full((1, tq), NEG, F32)
    groups = n_full // MAIN_UNROLL
    start = tuple((m, one, acc, low) for m, acc in diag)
    if reach is None:
        done, fast = groups, lax.fori_loop(0, groups, fast_group, start)
    else:
        def live(state):
            g, c = state
            nxt = tile_bias(n_full - 1 - g * MAIN_UNROLL)
            gap = functools.reduce(jnp.maximum, [jnp.max(b + nxt) - jnp.min(m)
                                                 for b, (m, _, _, _) in zip(reach, c)])
            return jnp.logical_and(g < groups, gap > UNDERFLOW_LOG2)

        done, fast = lax.while_loop(live, lambda st: (st[0] + 1, fast_group(st[0], st[1])), (jnp.int32(0), start))
    last = n_full - done * MAIN_UNROLL
    accs = [(acc + _dot(load_vt(last), scratch[ci])) * alpha
            for ci, ((_, _, load_vt), (_, alpha, acc, _)) in enumerate(zip(chains, fast))]
    worst = functools.reduce(jnp.maximum, [jnp.max(jump) for _, _, _, jump in fast])

    def redo():
        carry = lax.fori_loop(0, n_full, safe_step, diag)
        return [acc for _, acc in carry]

    return lax.cond(worst > MAX_JUMP, redo, lambda: accs)


def _k_tile(k_ref, j, lanes=slice(None)):
    return k_ref[pl.ds(pl.multiple_of(j * ATT_TK, ATT_TK), ATT_TK), lanes]


def _mla_kernel(q_ref, k_ref, vt_ref, o_ref, p_ref):
    i = pl.program_id(1)
    rows = MLA_V + BF16_ROWS
    def chain(hh):
        lanes = slice(hh * LANES, (hh + 1) * LANES)
        return (q_ref[:, lanes].astype(F32).T.astype(BF16), lambda j: _k_tile(k_ref, j, lanes),
                lambda j: vt_ref[j, hh * rows:(hh + 1) * rows, :])

    accs = _flash_cols([chain(0), chain(1)], i, rows, scratch=p_ref)
    halves = [acc[0:MLA_V] / acc[MLA_V:MLA_V + 1] for acc in accs]
    o_ref[...] = jnp.concatenate(halves, axis=0).T.astype(BF16)


def _mla_attention(qm, km, vmt):
    s = qm.shape[0]
    tq = ATT_TQ
    rows = 2 * (MLA_V + BF16_ROWS)
    return pl.pallas_call(
        _mla_kernel,
        grid=(MLA_HEADS // 2, s // tq),
        in_specs=[pl.BlockSpec((tq, 2 * LANES), lambda p, i: (i, p)),
                  pl.BlockSpec((s, 2 * LANES), lambda p, i: (0, p)),
                  pl.BlockSpec((vmt.shape[0], rows, ATT_TK), lambda p, i: (0, p, 0))],
        out_specs=pl.BlockSpec((tq, LANES), lambda p, i: (i, p)),
        out_shape=jax.ShapeDtypeStruct((s, MLA_HEADS * MLA_V), BF16),
        scratch_shapes=[pltpu.VMEM((2, ATT_TK, tq), BF16)],
        compiler_params=_params(("parallel", "parallel")),
        name="mla_attn",
    )(qm, km, vmt)


def _diff_kernel(lam_ref, slope_ref, kn_ref, qf_ref, g_ref, q_ref, k_ref, vt_ref, o_ref, p_ref, *, lam_init):
    i = pl.program_id(1)
    tq = ATT_TQ
    lane = lax.broadcasted_iota(jnp.int32, (tq, LANES), 1)
    qf32 = q_ref[...].astype(F32)
    feat = qf_ref[...]
    zero = jnp.zeros_like(qf32)
    featt = feat.astype(F32).T.astype(BF16)
    qa = jnp.concatenate([jnp.where(lane < DIFF_QK, qf32, zero).T.astype(BF16), featt], axis=0)
    qb = jnp.concatenate([jnp.where(lane >= DIFF_QK, qf32, zero).T.astype(BF16), featt], axis=0)
    slope = slope_ref[0:1, 0:1]
    rows = DIFF_V + BF16_ROWS
    tile_bias = lambda j: slope * (j * ATT_TK - i * tq).astype(F32)
    load_k = lambda j: _k_tile(k_ref, j)
    load_vt = lambda j: vt_ref[j]
    q2 = qf32 * qf32
    top = lambda z: jnp.max(jnp.sum(z, axis=1, keepdims=True), axis=0, keepdims=True)
    qn = (top(jnp.where(lane < DIFF_QK, q2, zero)), top(jnp.where(lane >= DIFF_QK, q2, zero)))
    reach = [jnp.sqrt(qn[x] * kn_ref[x:x + 1, 0:1]) * 1.001 + slope * ATT_TK for x in range(2)]
    acc1, acc2 = _flash_cols([(qa, load_k, load_vt), (qb, load_k, load_vt)], i, rows, tile_bias, p_ref, reach)
    lp = lam_ref[...]
    s1 = jnp.sum(lp[0:1, :] * lp[1:2, :], axis=1, keepdims=True)
    s2 = jnp.sum(lp[2:3, :] * lp[3:4, :], axis=1, keepdims=True)
    lam = jnp.exp(s1) - jnp.exp(s2) + lam_init
    ot = acc1[0:DIFF_V] / acc1[DIFF_V:DIFF_V + 1] - lam * (acc2[0:DIFF_V] / acc2[DIFF_V:DIFF_V + 1])
    o_ref[...] = (_rms(ot.T, g_ref[...]) * (1.0 - lam_init)).astype(BF16)


def _diff_attention(lam_p, slope_arr, knorm, qfeat, subln_g, dq, dk, dvt, lam_init):
    s = dq.shape[0]
    tq = ATT_TQ
    return pl.pallas_call(
        functools.partial(_diff_kernel, lam_init=lam_init),
        grid=(DIFF_HEADS, s // tq),
        in_specs=[pl.BlockSpec(lam_p.shape, lambda h, i: (0, 0)),
                  pl.BlockSpec((None, SUBLANES, LANES), lambda h, i: (h, 0, 0)),
                  pl.BlockSpec((None, SUBLANES, LANES), lambda h, i: (h, 0, 0)),
                  pl.BlockSpec((None, tq, LANES), lambda h, i: (h, 0, 0)),
                  pl.BlockSpec(subln_g.shape, lambda h, i: (0, 0)),
                  pl.BlockSpec((tq, LANES), lambda h, i: (i, h)),
                  pl.BlockSpec((s, 2 * LANES), lambda h, i: (0, h)),
                  pl.BlockSpec((dvt.shape[0], DIFF_V + BF16_ROWS, ATT_TK), lambda h, i: (0, h, 0))],
        out_specs=pl.BlockSpec((tq, LANES), lambda h, i: (i, h)),
        out_shape=jax.ShapeDtypeStruct((s, DIFF_HEADS * DIFF_V), BF16),
        scratch_shapes=[pltpu.VMEM((2, ATT_TK, tq), BF16)],
        compiler_params=_params(("parallel", "parallel")),
        name="diff_attn",
    )(lam_p, slope_arr, knorm, qfeat, subln_g, dq, dk, dvt)


def _shift_rows(u, k, prev):
    top = jnp.where(lax.broadcasted_iota(jnp.int32, prev.shape, 0) < k,
                    pltpu.roll(prev, k, 0), pltpu.roll(u[0:SUBLANES, :], k, 0))
    return jnp.concatenate([top, pltpu.roll(u, k, 0)[SUBLANES:, :]], axis=0)


def _ffn_tile(x1, fpre, wup_ref, cw_ref, cb_ref, wdn_ref, fpost, carry_ref):
    tm = x1.shape[0]
    hf = _rms(x1, fpre).astype(BF16)
    acts = []
    lo = 0
    for width in FF_CHUNKS:
        ys = []
        for part in range(2):
            c0 = part * D_FF + lo
            u = _dot(hf, wup_ref[:, c0:c0 + width])
            prev = carry_ref[:, c0:c0 + width]
            carry_ref[:, c0:c0 + width] = u[tm - SUBLANES:, :]
            ys.append(cw_ref[2:3, c0:c0 + width] * u
                      + cw_ref[1:2, c0:c0 + width] * _shift_rows(u, 1, prev)
                      + cw_ref[0:1, c0:c0 + width] * _shift_rows(u, 2, prev)
                      + cb_ref[:, c0:c0 + width])
        gate, up = ys
        acts.append((gate * (1.0 / (1.0 + jnp.exp(-gate))) * up).astype(BF16))
        lo += width
    acc = _dot(jnp.concatenate(acts, axis=1), wdn_ref[...])
    return x1 + _rms(acc, fpost)


def _out_ffn_even_kernel(x_ref, a_ref, b_ref, wout_ref, postg_ref, fpre_ref, wup_ref, cw_ref, cb_ref,
                         wdn_ref, fpost_ref, o_ref, carry_ref):
    @pl.when(pl.program_id(0) == 0)
    def _():
        carry_ref[...] = jnp.zeros_like(carry_ref)

    na = a_ref.shape[1]
    mix = _dot(a_ref[...], wout_ref[0:na, :]) + _dot(b_ref[...], wout_ref[na:, :])
    x1 = x_ref[...] + _rms(mix, postg_ref[...])
    o_ref[...] = _ffn_tile(x1, fpre_ref[...], wup_ref, cw_ref, cb_ref, wdn_ref, fpost_ref[...], carry_ref)


def _out_ffn_odd_kernel(x_ref, o0_ref, l0_ref, o1_ref, l1_ref, o2_ref, l2_ref, bt_ref, wout_ref, postg_ref,
                        fpre_ref, wup_ref, cw_ref, cb_ref, wdn_ref, fpost_ref, o_ref, carry_ref, scr):
    @pl.when(pl.program_id(0) == 0)
    def _():
        carry_ref[...] = jnp.zeros_like(carry_ref)

    (l0, o0), (l1, o1), (l2, o2) = [(_classes_to_rows(l_ref, d, scr), _classes_to_rows(og_ref, d, scr))
                                    for (l_ref, og_ref), (_, d) in zip(((l0_ref, o0_ref), (l1_ref, o1_ref),
                                                                        (l2_ref, o2_ref)), DIL_CONFIGS)]
    m = jnp.maximum(jnp.maximum(l0, l1), l2)
    e0, e1, e2 = jnp.exp(l0 - m), jnp.exp(l1 - m), jnp.exp(l2 - m)
    dil = (e0 * o0 + e1 * o1 + e2 * o2) / (e0 + e1 + e2)
    na = dil.shape[1]
    sb = bt_ref[...].T
    mix = _dot(dil.astype(BF16), wout_ref[0:na, :]) + _dot(sb.astype(BF16), wout_ref[na:, :])
    x1 = x_ref[...] + _rms(mix, postg_ref[...])
    o_ref[...] = _ffn_tile(x1, fpre_ref[...], wup_ref, cw_ref, cb_ref, wdn_ref, fpost_ref[...], carry_ref)


def _out_ffn(kernel, name, x, acts, acts_t, consts, scratch=()):
    s = x.shape[0]
    tm = ROW_TILE
    row = lambda c, n=s: pl.BlockSpec((tm * n // s, c), lambda i: (i, 0))
    col = lambda r: pl.BlockSpec((r, tm), lambda i: (0, i))
    return pl.pallas_call(
        kernel,
        grid=(s // tm,),
        in_specs=([row(D_MODEL)] + [row(a.shape[1], a.shape[0]) for a in acts] + [col(a.shape[0]) for a in acts_t]
                  + [_const_spec(c.shape) for c in consts]),
        out_specs=row(D_MODEL),
        out_shape=jax.ShapeDtypeStruct((s, D_MODEL), F32),
        scratch_shapes=[pltpu.VMEM((SUBLANES, 2 * D_FF), F32), *scratch],
        compiler_params=_params(("arbitrary",)),
        name=name,
    )(x, *acts, *acts_t, *consts)


def _proj_odd_kernel(x_ref, g_ref, win_ref, *refs):
    out_refs, scr = refs[:-1], refs[-1]
    h = _rms(x_ref[...], g_ref[...]).astype(BF16)
    proj = _dot(h, win_ref[...])
    c = DIL_HEADS_PER_GROUP * HEAD_DIM
    scale = HEAD_DIM ** -0.5
    for n, ref in enumerate(out_refs[:-1]):
        blk = proj[:, n * c:(n + 1) * c]
        if n < 3 or n == 9:
            blk = blk * scale
        if n < 9:
            blk = _rows_to_classes(blk, DIL_CONFIGS[n % 3][1], scr)
        ref[...] = blk.astype(BF16)
    _store_transposed_tiles(out_refs[-1], proj[:, 11 * c:12 * c], SB_HEADS, HEAD_DIM, 0)


def _proj_odd(x, g, win):
    s = x.shape[0]
    tm, tk = ROW_TILE, ATT_TK
    c = DIL_HEADS_PER_GROUP * HEAD_DIM
    row = lambda w, d=1: pl.BlockSpec((tm // d, d * w), lambda i: (i, 0))
    return pl.pallas_call(
        _proj_odd_kernel,
        grid=(s // tm,),
        in_specs=[row(D_MODEL), _const_spec(g.shape), _const_spec(win.shape)],
        out_specs=([row(c, d) for _ in range(3) for _, d in DIL_CONFIGS] + [row(c)] * 2
                   + [pl.BlockSpec((tm // tk, c, tk), lambda i: (i, 0, 0))]),
        out_shape=([jax.ShapeDtypeStruct((s // d, d * c), BF16) for _ in range(3) for _, d in DIL_CONFIGS]
                   + [jax.ShapeDtypeStruct((s, c), BF16)] * 2 + [jax.ShapeDtypeStruct((s // tk, c, tk), BF16)]),
        scratch_shapes=[pltpu.VMEM((c // LANES, tm, LANES), F32)],
        compiler_params=_params(("parallel",)),
        name="proj_odd",
    )(x, g, win)


def _dilated_kernel(bias_ref, q_ref, kp_ref, kc_ref, vp_ref, vc_ref, o_ref, lse_ref, *, nub):
    t, tb = DIL_T, DIL_TB
    ub = pl.program_id(0) % nub
    c = DIL_HEADS_PER_GROUP * HEAD_DIM
    qf = q_ref[...].astype(F32)
    kcat = jnp.concatenate([kp_ref[...], kc_ref[...]], axis=0)
    vcat = jnp.concatenate([vp_ref[...], vc_ref[...]], axis=0)
    lane = lax.broadcasted_iota(jnp.int32, (tb, c), 1)
    col = lax.broadcasted_iota(jnp.int32, (tb, t + tb), 1)
    no_prev = col < jnp.where(ub == 0, t, 0)
    out = jnp.zeros((tb, c), F32)
    lse = jnp.zeros((tb, c), F32)
    for hh in range(DIL_HEADS_PER_GROUP):
        mine = jnp.logical_and(lane >= hh * HEAD_DIM, lane < (hh + 1) * HEAD_DIM)
        qh = jnp.where(mine, qf, 0.0).astype(BF16)
        s = _dot_t(qh, kcat) + bias_ref[hh]
        s = jnp.where(no_prev, NEG, s)
        m = jnp.max(s, axis=1, keepdims=True)
        e = jnp.exp(s - m)
        den = jnp.sum(e, axis=1, keepdims=True)
        oh = _dot(e.astype(BF16), vcat) / den
        out = jnp.where(mine, oh, out)
        lse = jnp.where(mine, m + jnp.log(den), lse)
    o_ref[...] = out
    lse_ref[...] = lse


def _dilated_group(q, k, v, dil, slopes):
    c = q.shape[1] // dil
    s = q.shape[0] * dil
    t, tb = DIL_T, DIL_TB
    nub = s // dil // tb
    a = np.arange(tb)[:, None]
    cc = np.arange(t + tb)[None, :]
    steps = t + a - cc
    ok = (steps >= 0) & (steps <= t)
    dist = (steps * dil).astype(np.float32)
    bias = np.where(ok[None], -np.asarray(slopes, np.float32)[:, None, None] * dist[None], np.float32(NEG))
    bias = jnp.asarray(bias, F32)
    cur = pl.BlockSpec((tb, c), lambda b: (b % nub, b // nub))
    prev = pl.BlockSpec((t, c), lambda b: (jnp.maximum(b % nub * (tb // t) - 1, 0), b // nub))
    o, lse = pl.pallas_call(
        functools.partial(_dilated_kernel, nub=nub),
        grid=(dil * nub,),
        in_specs=[_const_spec(bias.shape), cur, prev, cur, prev, cur],
        out_specs=[cur, cur],
        out_shape=[jax.ShapeDtypeStruct((s // dil, dil * c), F32)] * 2,
        compiler_params=_params(("parallel",)),
        name=f"dilated_d{dil}",
    )(bias, q, k, k, v, v)
    return o, lse


def _sb_kernel(tri_ref, q_ref, k_ref, vt_ref, o_ref):
    hp = pl.program_id(0)
    i = pl.program_id(1)
    tq, tk = SB_TQ, ATT_TK
    ratio = tq // tk
    c = SB_HEADS * HEAD_DIM
    tri = tri_ref[...]
    lane = lax.broadcasted_iota(jnp.int32, (tq, c), 1)
    qf = q_ref[...].astype(F32)
    qhs = []
    for t in range(SB_GROUP):
        hh = hp * SB_GROUP + t
        mine = jnp.logical_and(lane >= hh * HEAD_DIM, lane < (hh + 1) * HEAD_DIM)
        qhs.append(jnp.where(mine, qf, 0.0).astype(BF16))
    strict = lax.broadcasted_iota(jnp.int32, (tk, tk), 0) < lax.broadcasted_iota(jnp.int32, (tk, tk), 1)
    n_full = i * ratio

    def weights(z, r, masked):
        sp = jnp.log1p(jnp.exp(-jnp.abs(z)))
        log_beta = jnp.minimum(z, 0.0) - sp
        log_keep = log_beta - z
        if masked:
            log_keep = jnp.where(strict, log_keep, 0.0)
        hi = log_keep.astype(BF16)
        lo = (log_keep - hi.astype(F32)).astype(BF16)
        aft = _dot(tri, jnp.concatenate([hi, lo], axis=0))
        a = jnp.exp(log_beta + aft[0:tk] + r)
        if masked:
            a = jnp.where(strict, a, 0.0)
        return a.astype(BF16), r + aft[tk:tk + 1]

    def diag_step(d, state):
        kt = _k_tile(k_ref, n_full + d)
        vt = vt_ref[n_full + d]
        out = []
        for t, (rs, accs) in enumerate(state):
            z = _dot_t(kt, qhs[t][d * tk:, :])
            rs, accs = list(rs), list(accs)
            for blk in range(d, ratio):
                a, rs[blk] = weights(z[:, (blk - d) * tk:(blk - d + 1) * tk], rs[blk], blk == d)
                accs[blk] = accs[blk] + _dot(vt[t * HEAD_DIM:(t + 1) * HEAD_DIM, :], a)
            out.append((rs, accs))
        return out

    state = [([jnp.zeros((1, tk), F32)] * ratio, [jnp.zeros((HEAD_DIM, tk), F32)] * ratio) for _ in range(SB_GROUP)]
    for d in reversed(range(ratio)):
        state = diag_step(d, state)
    carry = tuple((jnp.concatenate(rs, axis=1), jnp.concatenate(accs, axis=1)) for rs, accs in state)

    def step(j, cr):
        kt = _k_tile(k_ref, j)
        vt = vt_ref[j]
        out = []
        for t, (r, acc) in enumerate(cr):
            a, r = weights(_dot_t(kt, qhs[t]), r, False)
            out.append((r, acc + _dot(vt[t * HEAD_DIM:(t + 1) * HEAD_DIM, :], a)))
        return tuple(out)

    def more(state):
        n, cr = state
        top = functools.reduce(jnp.maximum, [jnp.max(r) for r, _ in cr])
        return jnp.logical_and(n < n_full, top > SB_STOP)

    def walk(state):
        n, cr = state
        return n + 1, step(n_full - 1 - n, cr)

    _, carry = lax.while_loop(more, walk, (jnp.int32(0), carry))
    o_ref[...] = jnp.concatenate([acc for _, acc in carry], axis=0)


def _stickbreak(q, k, vt):
    s, c = q.shape
    tq, tk = SB_TQ, ATT_TK
    j = np.arange(tk)
    later = (j[None, :] > j[:, None]).astype(np.float32)
    tri = np.concatenate([np.concatenate([later, later], axis=1), np.ones((BF16_ROWS, 2 * tk), np.float32)], axis=0)
    tri = jnp.asarray(tri, BF16)
    return pl.pallas_call(
        _sb_kernel,
        grid=(SB_HEADS // SB_GROUP, s // tq),
        in_specs=[_const_spec(tri.shape),
                  pl.BlockSpec((tq, c), lambda h, i: (i, 0)),
                  _const_spec((s, c)),
                  pl.BlockSpec((s // tk, SB_GROUP * HEAD_DIM, tk), lambda h, i: (0, h, 0))],
        out_specs=pl.BlockSpec((SB_GROUP * HEAD_DIM, tq), lambda h, i: (h, i)),
        out_shape=jax.ShapeDtypeStruct((c, s), F32),
        compiler_params=_params(("parallel", "parallel")),
        name="stickbreak",
    )(tri, q, k, vt)


def _alibi_slopes(n):
    return 2.0 ** (-8.0 * np.arange(1, n + 1) / n)


def _pad_cols(w, lo, width):
    return jnp.pad(w, ((0, 0), (lo, width - lo - w.shape[1])))


def _rot_half_cols(w):
    half = w.shape[1] // 2
    return jnp.concatenate([-w[:, half:], w[:, :half]], axis=1)


def _even_weights(w_in, w_uq, w_ukv):
    o_kr = MLA_Q_RANK + MLA_KV_RANK
    w_kr = w_in[:, o_kr:o_kr + MLA_ROPE]
    win = jnp.concatenate([w_in[:, :o_kr], _pad_cols(w_kr, MLA_NOPE, LANES),
                           _pad_cols(_rot_half_cols(w_kr), MLA_NOPE, LANES),
                           w_in[:, o_kr + MLA_ROPE:]], axis=1)
    hd = MLA_NOPE + MLA_ROPE
    q_plain, q_rot = [], []
    for hh in range(MLA_HEADS):
        wh = w_uq[:, hh * hd:(hh + 1) * hd]
        q_plain.append(_pad_cols(wh, 0, LANES))
        q_rot.append(_pad_cols(_rot_half_cols(wh[:, MLA_NOPE:]), MLA_NOPE, LANES))
    wuq = jnp.concatenate(q_plain + q_rot, axis=1)
    hk = MLA_NOPE + MLA_V
    k_cols = [_pad_cols(w_ukv[:, hh * hk:hh * hk + MLA_NOPE], 0, LANES) for hh in range(MLA_HEADS)]
    v_cols = [w_ukv[:, hh * hk + MLA_NOPE:(hh + 1) * hk] for hh in range(MLA_HEADS)]
    wkv = jnp.concatenate(k_cols + v_cols, axis=1)
    return win.astype(BF16), wuq.astype(BF16), wkv.astype(BF16)


def _rope_tables(s):
    half = MLA_ROPE // 2
    inv = ROPE_BASE ** (-jnp.arange(half, dtype=F32) / half)
    ang = jnp.arange(s).astype(F32)[:, None] * inv
    cos2 = jnp.tile(jnp.cos(ang), (1, 2))
    sin2 = jnp.tile(jnp.sin(ang), (1, 2))
    scale = (MLA_NOPE + MLA_ROPE) ** -0.5 * LOG2E
    tail = LANES - MLA_NOPE - MLA_ROPE
    ones, zeros, ztail = jnp.ones((s, MLA_NOPE), F32), jnp.zeros((s, MLA_NOPE), F32), jnp.zeros((s, tail), F32)
    cosq = jnp.concatenate([ones, cos2, ztail], axis=1) * scale
    sinq = jnp.concatenate([zeros, sin2, ztail], axis=1) * scale
    cosk = jnp.concatenate([zeros, cos2, ztail], axis=1)
    sink = jnp.concatenate([zeros, sin2, ztail], axis=1)
    return cosq, sinq, cosk, sink


def _row2d(v):
    return v.reshape(1, -1).astype(F32)


def kernel(x, ev_pre_g, ev_w_in, ev_cq_g, ev_w_uq, ev_ckv_g, ev_w_ukv, ev_lam_q1, ev_lam_k1, ev_lam_q2,
           ev_lam_k2, ev_subln_g, ev_w_out, ev_post_g, od_pre_g, od_w_in, od_w_out, od_post_g, ffn_pre_g,
           ffn_w_up, ffn_conv_w, ffn_conv_b, ffn_w_down, ffn_post_g):
    b, s, _ = x.shape
    assert b == 1 and s % max(ROW_TILE, ATT_TQ, SB_TQ, DIL_TB * DIL_CONFIGS[-1][1]) == 0
    assert ROW_TILE % ATT_TK == 0 and ATT_TQ % ATT_TK == 0 and SB_TQ % ATT_TK == 0
    assert (ATT_TQ // ATT_TK) % MAIN_UNROLL == 0 and DIL_TB % DIL_T == 0
    assert sum(FF_CHUNKS) == D_FF
    assert ATT_TK <= 256
    xs = x[0]

    def ffn_consts(i):
        return [_row2d(ffn_pre_g[i]), ffn_w_up[i].astype(BF16), ffn_conv_w[i].astype(F32),
                _row2d(ffn_conv_b[i]), ffn_w_down[i].astype(BF16), _row2d(ffn_post_g[i])]

    win, wuq, wkv = _even_weights(ev_w_in[0], ev_w_uq[0], ev_w_ukv[0])
    cosq, sinq, cosk, sink = _rope_tables(s)
    pos_col = (jnp.arange(ROW_TILE) % ATT_TK).astype(BF16)[:, None]
    pos_tile = jnp.pad(jnp.tile(pos_col, (1, SLOPE_PARTS)), ((0, 0), (0, LANES - SLOPE_PARTS)))
    qm, km, vmt, dq, dk, dvt, kn_tiles = _proj_even(xs, _row2d(ev_pre_g[0]), win, _row2d(ev_cq_g[0]), wuq,
                                          _row2d(ev_ckv_g[0]), wkv, cosq, sinq, cosk, sink, pos_tile)
    o_mla = _mla_attention(qm, km, vmt)
    lam_p = jnp.pad(jnp.stack([ev_lam_q1[0], ev_lam_k1[0], ev_lam_q2[0], ev_lam_k2[0]]).astype(F32),
                    ((0, SUBLANES - 4), (0, LANES - DIFF_QK)))
    d_slopes = _alibi_slopes(DIFF_HEADS) * LOG2E
    slope_arr = jnp.asarray(np.broadcast_to(d_slopes[:, None, None], (DIFF_HEADS, SUBLANES, LANES)), F32)
    qfeat = jnp.zeros((DIFF_HEADS, ATT_TQ, LANES), F32)
    rest = jnp.asarray(d_slopes, F32)
    for part in range(SLOPE_PARTS):
        piece = rest.astype(BF16).astype(F32)
        qfeat = qfeat.at[:, :, part].set(piece[:, None])
        rest = rest - piece
    lam_init = 0.8 - 0.6 * math.exp(-0.3 * 0)
    knorm = jnp.pad(jnp.max(kn_tiles, axis=0).reshape(DIFF_HEADS, 2, LANES), ((0, 0), (0, SUBLANES - 2), (0, 0)))
    o_diff = _diff_attention(lam_p, slope_arr, knorm, qfeat.astype(BF16), _row2d(ev_subln_g[0]),
                             dq, dk, dvt, lam_init)
    x2 = _out_ffn(_out_ffn_even_kernel, "out_ffn_even", xs, [o_mla, o_diff], [],
                  [ev_w_out[0].astype(BF16), _row2d(ev_post_g[0])] + ffn_consts(0))

    p = _proj_odd(x2, _row2d(od_pre_g[0]), od_w_in[0].astype(BF16))
    slopes = _alibi_slopes(len(DIL_CONFIGS) * DIL_HEADS_PER_GROUP).reshape(len(DIL_CONFIGS), -1)
    dil_outs = []
    for gi, (_, dil) in enumerate(DIL_CONFIGS):
        dil_outs += list(_dilated_group(p[gi], p[3 + gi], p[6 + gi], dil, slopes[gi]))
    o_sbt = _stickbreak(p[9], p[10], p[11])
    lane_tiles = DIL_HEADS_PER_GROUP * HEAD_DIM // LANES
    x4 = _out_ffn(_out_ffn_odd_kernel, "out_ffn_odd", x2, dil_outs, [o_sbt],
                  [od_w_out[0].astype(BF16), _row2d(od_post_g[0])] + ffn_consts(1),
                  scratch=[pltpu.VMEM((lane_tiles, ROW_TILE, LANES), F32)])
    return x4[None]
```

```python
import functools
import math

import numpy as np
import jax
import jax.numpy as jnp
from jax import lax
from jax.experimental import pallas as pl
from jax.experimental.pallas import tpu as pltpu

F32 = jnp.float32
BF16 = jnp.bfloat16

D_MODEL = 1024
NORM_EPS = 1e-6
MLA_HEADS = 8
MLA_Q_RANK = 256
MLA_KV_RANK = 128
MLA_NOPE = 64
MLA_ROPE = 32
MLA_V = 64
ROPE_BASE = 10000.0
DIFF_HEADS = 4
DIFF_QK = 64
DIFF_V = 128
DIL_CONFIGS = ((128, 1), (512, 4), (2048, 16))
DIL_HEADS_PER_GROUP = 4
HEAD_DIM = 64
SB_HEADS = 4
D_FF = 2816
CONV_WIDTH = 3

LANES = 128
SUBLANES = 8
BF16_ROWS = 16
VMEM_LIMIT_BYTES = 56 * 1024 * 1024
ROW_TILE = 512
ATT_TQ = 2048
MAIN_UNROLL = 8
SB_TQ = 512
SB_GROUP = 4
ATT_TK = 256
DIL_T = 128
DIL_TB = 512
FF_CHUNKS = (512,) * 5 + (256,)
NEG = -1e30
LOG2E = 1.4426950408889634
UNDERFLOW_LOG2 = -152.0
MAX_JUMP = 32.0
SB_STOP = -110.0
SLOPE_PARTS = 3

_TRANS_B = (((1,), (1,)), ((), ()))


def _dot(a, b):
    return jnp.dot(a, b, preferred_element_type=F32)


def _dot_t(a, b):
    return lax.dot_general(a, b, _TRANS_B, preferred_element_type=F32)


def _rms(xf, g):
    ms = jnp.mean(xf * xf, axis=-1, keepdims=True)
    return xf * lax.rsqrt(ms + NORM_EPS) * g


def _params(sem):
    return pltpu.CompilerParams(dimension_semantics=sem, vmem_limit_bytes=VMEM_LIMIT_BYTES)


def _const_spec(shape):
    nd = len(shape)
    return pl.BlockSpec(shape, lambda *_: (0,) * nd, pipeline_mode=pl.Buffered(1))


def _store_transposed_tiles(dst_ref, src, heads, width, ones_rows):
    tk = ATT_TK
    per = width + ones_rows
    for b in range(src.shape[0] // tk):
        vt = src[b * tk:(b + 1) * tk, :].T.astype(BF16)
        for hh in range(heads):
            dst_ref[b, hh * per:hh * per + width, :] = vt[hh * width:(hh + 1) * width, :]
            if ones_rows:
                dst_ref[b, hh * per + width:(hh + 1) * per, :] = jnp.ones((ones_rows, tk), BF16)


def _rows_to_classes(x, dil, scr):
    if dil == 1:
        return x
    tm, c = x.shape
    for h in range(c // LANES):
        scr[h] = x[:, h * LANES:(h + 1) * LANES]
    return jnp.concatenate([scr[h, pl.ds(r, tm // dil, stride=dil), :]
                            for r in range(dil) for h in range(c // LANES)], axis=1)


def _classes_to_rows(ref, dil, scr):
    if dil == 1:
        return ref[...]
    n, c = ref.shape[0], ref.shape[1] // dil
    for r in range(dil):
        for h in range(c // LANES):
            scr[h, pl.ds(r, n, stride=dil), :] = ref[:, r * c + h * LANES:r * c + (h + 1) * LANES]
    return jnp.concatenate([scr[h] for h in range(c // LANES)], axis=1)


def _proj_even_kernel(x_ref, g_ref, win_ref, cqg_ref, wuq_ref, ckvg_ref, wkv_ref,
                      cosq_ref, sinq_ref, cosk_ref, sink_ref, pos_ref,
                      qm_ref, km_ref, vmt_ref, dq_ref, dk_ref, dvt_ref, kn_ref):
    h = _rms(x_ref[...], g_ref[...]).astype(BF16)
    proj = _dot(h, win_ref[...])
    nq = MLA_HEADS * LANES
    cq = _rms(proj[:, 0:MLA_Q_RANK], cqg_ref[...]).astype(BF16)
    qq = _dot(cq, wuq_ref[...])
    cosq, sinq = cosq_ref[...], sinq_ref[...]
    for hh in range(MLA_HEADS):
        a = qq[:, hh * LANES:(hh + 1) * LANES]
        b = qq[:, nq + hh * LANES:nq + (hh + 1) * LANES]
        qm_ref[:, hh * LANES:(hh + 1) * LANES] = (a * cosq + b * sinq).astype(BF16)
    o = MLA_Q_RANK
    ckv = _rms(proj[:, o:o + MLA_KV_RANK], ckvg_ref[...]).astype(BF16)
    kv = _dot(ckv, wkv_ref[...])
    o += MLA_KV_RANK
    krc = proj[:, o:o + LANES] * cosk_ref[...] + proj[:, o + LANES:o + 2 * LANES] * sink_ref[...]
    for hh in range(MLA_HEADS):
        km_ref[:, hh * LANES:(hh + 1) * LANES] = (kv[:, hh * LANES:(hh + 1) * LANES] + krc).astype(BF16)
    _store_transposed_tiles(vmt_ref, kv[:, nq:nq + MLA_HEADS * MLA_V], MLA_HEADS, MLA_V, BF16_ROWS)
    o += 2 * LANES
    nd = DIFF_HEADS * 2 * DIFF_QK
    dq_ref[...] = (proj[:, o:o + nd] * (DIFF_QK ** -0.5 * LOG2E)).astype(BF16)
    o += nd
    pos = pos_ref[...]
    lane = lax.broadcasted_iota(jnp.int32, (x_ref.shape[0], LANES), 1)
    norms = []
    for hh in range(DIFF_HEADS):
        kb = proj[:, o + hh * LANES:o + (hh + 1) * LANES].astype(BF16)
        dk_ref[:, hh * 2 * LANES:hh * 2 * LANES + LANES] = jnp.where(lane < DIFF_QK, kb, pos)
        dk_ref[:, hh * 2 * LANES + LANES:(hh + 1) * 2 * LANES] = jnp.where(lane >= DIFF_QK, kb, pos)
        kk = kb.astype(F32) * kb.astype(F32)
        for first in (True, False):
            part = jnp.where((lane < DIFF_QK) == first, kk, 0.0)
            top = jnp.max(jnp.sum(part, axis=1, keepdims=True), axis=0, keepdims=True)
            norms.append(jnp.broadcast_to(top, (1, LANES)))
    kn_ref[0] = jnp.concatenate(norms, axis=0)
    o += nd
    _store_transposed_tiles(dvt_ref, proj[:, o:o + DIFF_HEADS * DIFF_V], DIFF_HEADS, DIFF_V, BF16_ROWS)


def _proj_even(x, g, win, cqg, wuq, ckvg, wkv, cosq, sinq, cosk, sink, pos):
    s = x.shape[0]
    tm, tk = ROW_TILE, ATT_TK
    row = lambda c: pl.BlockSpec((tm, c), lambda i: (i, 0))
    tile_t = lambda r: pl.BlockSpec((tm // tk, r, tk), lambda i: (i, 0, 0))
    mla_rows = MLA_HEADS * (MLA_V + BF16_ROWS)
    diff_rows = DIFF_HEADS * (DIFF_V + BF16_ROWS)
    flat = lambda c: jax.ShapeDtypeStruct((s, c), BF16)
    tiled = lambda r: jax.ShapeDtypeStruct((s // tk, r, tk), BF16)
    return pl.pallas_call(
        _proj_even_kernel,
        grid=(s // tm,),
        in_specs=[row(D_MODEL), _const_spec(g.shape), _const_spec(win.shape), _const_spec(cqg.shape),
                  _const_spec(wuq.shape), _const_spec(ckvg.shape), _const_spec(wkv.shape),
                  row(LANES), row(LANES), row(LANES), row(LANES), _const_spec(pos.shape)],
        out_specs=[row(MLA_HEADS * LANES), row(MLA_HEADS * LANES), tile_t(mla_rows),
                   row(DIFF_HEADS * 2 * DIFF_QK), row(DIFF_HEADS * 2 * LANES), tile_t(diff_rows),
                   pl.BlockSpec((1, 2 * DIFF_HEADS, LANES), lambda i: (i, 0, 0))],
        out_shape=[flat(MLA_HEADS * LANES), flat(MLA_HEADS * LANES), tiled(mla_rows),
                   flat(DIFF_HEADS * 2 * DIFF_QK), flat(DIFF_HEADS * 2 * LANES), tiled(diff_rows),
                   jax.ShapeDtypeStruct((s // tm, 2 * DIFF_HEADS, LANES), F32)],
        compiler_params=_params(("parallel",)),
        name="proj_even",
    )(x, g, win, cqg, wuq, ckvg, wkv, cosq, sinq, cosk, sink, pos)


def _flash_cols(chains, i, rows, tile_bias=None, scratch=None, reach=None):
    tq, tk = ATT_TQ, ATT_TK
    ratio = tq // tk
    n_full = i * ratio

    def safe_step(j, carry):
        out = []
        for (qh, load_k, load_vt), (m, acc) in zip(chains, carry):
            s = _dot(load_k(j), qh)
            mb = jnp.max(s, axis=0, keepdims=True)
            if tile_bias is not None:
                c = tile_bias(j)
                mb = mb + c
            m_new = jnp.maximum(m, mb)
            alpha = jnp.exp2(m - m_new)
            shift = m_new if tile_bias is None else m_new - c
            p = jnp.exp2(s - shift).astype(BF16)
            out.append((m_new, alpha * acc + _dot(load_vt(j), p)))
        return tuple(out)

    def fast_step(n, carry):
        j = n_full - 1 - n
        out = []
        for ci, ((qh, load_k, load_vt), (m, alpha, acc, jump)) in enumerate(zip(chains, carry)):
            acc = (acc + _dot(load_vt(j + 1), scratch[ci])) * alpha
            s = _dot(load_k(j), qh)
            c = None if tile_bias is None else tile_bias(j)
            shift = m if c is None else m - c
            scratch[ci] = jnp.exp2(s - shift).astype(BF16)
            mb = jnp.max(s, axis=0, keepdims=True)
            if c is not None:
                mb = mb + c
            m_new = jnp.maximum(m, mb)
            out.append((m_new, jnp.exp2(m - m_new), acc, jnp.maximum(jump, mb - m)))
        return tuple(out)

    def fast_group(g, c):
        for t in range(MAIN_UNROLL):
            c = fast_step(g * MAIN_UNROLL + t, c)
        return c

    def diag_step(d, state):
        tri = lax.broadcasted_iota(jnp.int32, (tk, tk), 0) <= lax.broadcasted_iota(jnp.int32, (tk, tk), 1)
        out = []
        for (qh, load_k, load_vt), (ms, accs) in zip(chains, state):
            s = _dot(load_k(n_full + d), qh[:, d * tk:])
            vt = load_vt(n_full + d)
            c = None if tile_bias is None else tile_bias(n_full + d)
            ms, accs = list(ms), list(accs)
            for blk in range(d, ratio):
                sb = s[:, (blk - d) * tk:(blk - d + 1) * tk]
                if blk == d:
                    sb = jnp.where(tri, sb, NEG)
                mb = jnp.max(sb, axis=0, keepdims=True)
                if c is not None:
                    mb = mb + c
                m_new = jnp.maximum(ms[blk], mb)
                shift = m_new if c is None else m_new - c
                p = jnp.exp2(sb - shift).astype(BF16)
                accs[blk] = jnp.exp2(ms[blk] - m_new) * accs[blk] + _dot(vt, p)
                ms[blk] = m_new
            out.append((ms, accs))
        return out

    state = [([jnp.full((1, tk), NEG, F32)] * ratio, [jnp.zeros((rows, tk), F32)] * ratio) for _ in chains]
    for d in range(ratio):
        state = diag_step(d, state)
    diag = tuple((jnp.concatenate(ms, axis=1), jnp.concatenate(accs, axis=1)) for ms, accs in state)
    for ci in range(len(chains)):
        scratch[ci] = jnp.zeros((tk, tq), BF16)
    one, low = jnp.ones((1, tq), F32), jnp.full((1, tq), NEG, F32)
    groups = n_full // MAIN_UNROLL
    start = tuple((m, one, acc, low) for m, acc in diag)
    if reach is None:
        done, fast = groups, lax.fori_loop(0, groups, fast_group, start)
    else:
        def live(state):
            g, c = state
            nxt = tile_bias(n_full - 1 - g * MAIN_UNROLL)
            gap = functools.reduce(jnp.maximum, [jnp.max(b + nxt) - jnp.min(m)
                                                 for b, (m, _, _, _) in zip(reach, c)])
            return jnp.logical_and(g < groups, gap > UNDERFLOW_LOG2)

        done, fast = lax.while_loop(live, lambda st: (st[0] + 1, fast_group(st[0], st[1])), (jnp.int32(0), start))
    last = n_full - done * MAIN_UNROLL
    accs = [(acc + _dot(load_vt(last), scratch[ci])) * alpha
            for ci, ((_, _, load_vt), (_, alpha, acc, _)) in enumerate(zip(chains, fast))]
    worst = functools.reduce(jnp.maximum, [jnp.max(jump) for _, _, _, jump in fast])

    def redo():
        carry = lax.fori_loop(0, n_full, safe_step, diag)
        return [acc for _, acc in carry]

    return lax.cond(worst > MAX_JUMP, redo, lambda: accs)


def _k_tile(k_ref, j, lanes=slice(None)):
    return k_ref[pl.ds(pl.multiple_of(j * ATT_TK, ATT_TK), ATT_TK), lanes]


def _mla_kernel(q_ref, k_ref, vt_ref, o_ref, p_ref):
    i = pl.program_id(1)
    rows = MLA_V + BF16_ROWS
    def chain(hh):
        lanes = slice(hh * LANES, (hh + 1) * LANES)
        return (q_ref[:, lanes].astype(F32).T.astype(BF16), lambda j: _k_tile(k_ref, j, lanes),
                lambda j: vt_ref[j, hh * rows:(hh + 1) * rows, :])

    accs = _flash_cols([chain(0), chain(1)], i, rows, scratch=p_ref)
    halves = [acc[0:MLA_V] / acc[MLA_V:MLA_V + 1] for acc in accs]
    o_ref[...] = jnp.concatenate(halves, axis=0).T.astype(BF16)


def _mla_attention(qm, km, vmt):
    s = qm.shape[0]
    tq = ATT_TQ
    rows = 2 * (MLA_V + BF16_ROWS)
    return pl.pallas_call(
        _mla_kernel,
        grid=(MLA_HEADS // 2, s // tq),
        in_specs=[pl.BlockSpec((tq, 2 * LANES), lambda p, i: (i, p)),
                  pl.BlockSpec((s, 2 * LANES), lambda p, i: (0, p)),
                  pl.BlockSpec((vmt.shape[0], rows, ATT_TK), lambda p, i: (0, p, 0))],
        out_specs=pl.BlockSpec((tq, LANES), lambda p, i: (i, p)),
        out_shape=jax.ShapeDtypeStruct((s, MLA_HEADS * MLA_V), BF16),
        scratch_shapes=[pltpu.VMEM((2, ATT_TK, tq), BF16)],
        compiler_params=_params(("parallel", "parallel")),
        name="mla_attn",
    )(qm, km, vmt)


def _diff_kernel(lam_ref, slope_ref, kn_ref, qf_ref, g_ref, q_ref, k_ref, vt_ref, o_ref, p_ref, *, lam_init):
    i = pl.program_id(1)
    tq = ATT_TQ
    lane = lax.broadcasted_iota(jnp.int32, (tq, LANES), 1)
    qf32 = q_ref[...].astype(F32)
    feat = qf_ref[...]
    zero = jnp.zeros_like(qf32)
    featf = feat.astype(F32)
    qa = jnp.where(lane < DIFF_QK, qf32, featf).T.astype(BF16)
    qb = jnp.where(lane >= DIFF_QK, qf32, featf).T.astype(BF16)
    slope = slope_ref[0:1, 0:1]
    rows = DIFF_V + BF16_ROWS
    tile_bias = lambda j: slope * (j * ATT_TK - i * tq).astype(F32)
    load_k = [lambda j, x=x: _k_tile(k_ref, j, slice(x * LANES, (x + 1) * LANES)) for x in range(2)]
    load_vt = lambda j: vt_ref[j]
    q2 = qf32 * qf32
    top = lambda z: jnp.max(jnp.sum(z, axis=1, keepdims=True), axis=0, keepdims=True)
    qn = (top(jnp.where(lane < DIFF_QK, q2, zero)), top(jnp.where(lane >= DIFF_QK, q2, zero)))
    reach = [jnp.sqrt(qn[x] * kn_ref[x:x + 1, 0:1]) * 1.001 + slope * ATT_TK for x in range(2)]
    acc1, acc2 = _flash_cols([(qa, load_k[0], load_vt), (qb, load_k[1], load_vt)], i, rows, tile_bias, p_ref, reach)
    lp = lam_ref[...]
    s1 = jnp.sum(lp[0:1, :] * lp[1:2, :], axis=1, keepdims=True)
    s2 = jnp.sum(lp[2:3, :] * lp[3:4, :], axis=1, keepdims=True)
    lam = jnp.exp(s1) - jnp.exp(s2) + lam_init
    ot = acc1[0:DIFF_V] / acc1[DIFF_V:DIFF_V + 1] - lam * (acc2[0:DIFF_V] / acc2[DIFF_V:DIFF_V + 1])
    o_ref[...] = (_rms(ot.T, g_ref[...]) * (1.0 - lam_init)).astype(BF16)


def _diff_attention(lam_p, slope_arr, knorm, qfeat, subln_g, dq, dk, dvt, lam_init):
    s = dq.shape[0]
    tq = ATT_TQ
    return pl.pallas_call(
        functools.partial(_diff_kernel, lam_init=lam_init),
        grid=(DIFF_HEADS, s // tq),
        in_specs=[pl.BlockSpec(lam_p.shape, lambda h, i: (0, 0)),
                  pl.BlockSpec((None, SUBLANES, LANES), lambda h, i: (h, 0, 0)),
                  pl.BlockSpec((None, SUBLANES, LANES), lambda h, i: (h, 0, 0)),
                  pl.BlockSpec((None, tq, LANES), lambda h, i: (h, 0, 0)),
                  pl.BlockSpec(subln_g.shape, lambda h, i: (0, 0)),
                  pl.BlockSpec((tq, LANES), lambda h, i: (i, h)),
                  pl.BlockSpec((s, 2 * LANES), lambda h, i: (0, h)),
                  pl.BlockSpec((dvt.shape[0], DIFF_V + BF16_ROWS, ATT_TK), lambda h, i: (0, h, 0))],
        out_specs=pl.BlockSpec((tq, LANES), lambda h, i: (i, h)),
        out_shape=jax.ShapeDtypeStruct((s, DIFF_HEADS * DIFF_V), BF16),
        scratch_shapes=[pltpu.VMEM((2, ATT_TK, tq), BF16)],
        compiler_params=_params(("parallel", "parallel")),
        name="diff_attn",
    )(lam_p, slope_arr, knorm, qfeat, subln_g, dq, dk, dvt)


def _shift_rows(u, k, prev):
    top = jnp.where(lax.broadcasted_iota(jnp.int32, prev.shape, 0) < k,
                    pltpu.roll(prev, k, 0), pltpu.roll(u[0:SUBLANES, :], k, 0))
    return jnp.concatenate([top, pltpu.roll(u, k, 0)[SUBLANES:, :]], axis=0)


def _ffn_tile(x1, fpre, wup_ref, cw_ref, cb_ref, wdn_ref, fpost, carry_ref):
    tm = x1.shape[0]
    hf = _rms(x1, fpre).astype(BF16)
    acts = []
    lo = 0
    for width in FF_CHUNKS:
        ys = []
        for part in range(2):
            c0 = part * D_FF + lo
            u = _dot(hf, wup_ref[:, c0:c0 + width])
            prev = carry_ref[:, c0:c0 + width]
            carry_ref[:, c0:c0 + width] = u[tm - SUBLANES:, :]
            ys.append(cw_ref[2:3, c0:c0 + width] * u
                      + cw_ref[1:2, c0:c0 + width] * _shift_rows(u, 1, prev)
                      + cw_ref[0:1, c0:c0 + width] * _shift_rows(u, 2, prev)
                      + cb_ref[:, c0:c0 + width])
        gate, up = ys
        acts.append((gate * (1.0 / (1.0 + jnp.exp(-gate))) * up).astype(BF16))
        lo += width
    acc = _dot(jnp.concatenate(acts, axis=1), wdn_ref[...])
    return x1 + _rms(acc, fpost)


def _out_ffn_even_kernel(x_ref, a_ref, b_ref, wout_ref, postg_ref, fpre_ref, wup_ref, cw_ref, cb_ref,
                         wdn_ref, fpost_ref, o_ref, carry_ref):
    @pl.when(pl.program_id(0) == 0)
    def _():
        carry_ref[...] = jnp.zeros_like(carry_ref)

    na = a_ref.shape[1]
    mix = _dot(a_ref[...], wout_ref[0:na, :]) + _dot(b_ref[...], wout_ref[na:, :])
    x1 = x_ref[...] + _rms(mix, postg_ref[...])
    o_ref[...] = _ffn_tile(x1, fpre_ref[...], wup_ref, cw_ref, cb_ref, wdn_ref, fpost_ref[...], carry_ref)


def _out_ffn_odd_kernel(x_ref, o0_ref, l0_ref, o1_ref, l1_ref, o2_ref, l2_ref, bt_ref, wout_ref, postg_ref,
                        fpre_ref, wup_ref, cw_ref, cb_ref, wdn_ref, fpost_ref, o_ref, carry_ref, scr):
    @pl.when(pl.program_id(0) == 0)
    def _():
        carry_ref[...] = jnp.zeros_like(carry_ref)

    (l0, o0), (l1, o1), (l2, o2) = [(_classes_to_rows(l_ref, d, scr), _classes_to_rows(og_ref, d, scr))
                                    for (l_ref, og_ref), (_, d) in zip(((l0_ref, o0_ref), (l1_ref, o1_ref),
                                                                        (l2_ref, o2_ref)), DIL_CONFIGS)]
    m = jnp.maximum(jnp.maximum(l0, l1), l2)
    e0, e1, e2 = jnp.exp(l0 - m), jnp.exp(l1 - m), jnp.exp(l2 - m)
    dil = (e0 * o0 + e1 * o1 + e2 * o2) / (e0 + e1 + e2)
    na = dil.shape[1]
    sb = bt_ref[...].T
    mix = _dot(dil.astype(BF16), wout_ref[0:na, :]) + _dot(sb.astype(BF16), wout_ref[na:, :])
    x1 = x_ref[...] + _rms(mix, postg_ref[...])
    o_ref[...] = _ffn_tile(x1, fpre_ref[...], wup_ref, cw_ref, cb_ref, wdn_ref, fpost_ref[...], carry_ref)


def _out_ffn(kernel, name, x, acts, acts_t, consts, scratch=()):
    s = x.shape[0]
    tm = ROW_TILE
    row = lambda c, n=s: pl.BlockSpec((tm * n // s, c), lambda i: (i, 0))
    col = lambda r: pl.BlockSpec((r, tm), lambda i: (0, i))
    return pl.pallas_call(
        kernel,
        grid=(s // tm,),
        in_specs=([row(D_MODEL)] + [row(a.shape[1], a.shape[0]) for a in acts] + [col(a.shape[0]) for a in acts_t]
                  + [_const_spec(c.shape) for c in consts]),
        out_specs=row(D_MODEL),
        out_shape=jax.ShapeDtypeStruct((s, D_MODEL), F32),
        scratch_shapes=[pltpu.VMEM((SUBLANES, 2 * D_FF), F32), *scratch],
        compiler_params=_params(("arbitrary",)),
        name=name,
    )(x, *acts, *acts_t, *consts)


def _proj_odd_kernel(x_ref, g_ref, win_ref, *refs):
    out_refs, scr = refs[:-1], refs[-1]
    h = _rms(x_ref[...], g_ref[...]).astype(BF16)
    proj = _dot(h, win_ref[...])
    c = DIL_HEADS_PER_GROUP * HEAD_DIM
    scale = HEAD_DIM ** -0.5
    for n, ref in enumerate(out_refs[:-1]):
        blk = proj[:, n * c:(n + 1) * c]
        if n < 3 or n == 9:
            blk = blk * scale
        if n < 9:
            blk = _rows_to_classes(blk, DIL_CONFIGS[n % 3][1], scr)
        ref[...] = blk.astype(BF16)
    _store_transposed_tiles(out_refs[-1], proj[:, 11 * c:12 * c], SB_HEADS, HEAD_DIM, 0)


def _proj_odd(x, g, win):
    s = x.shape[0]
    tm, tk = ROW_TILE, ATT_TK
    c = DIL_HEADS_PER_GROUP * HEAD_DIM
    row = lambda w, d=1: pl.BlockSpec((tm // d, d * w), lambda i: (i, 0))
    return pl.pallas_call(
        _proj_odd_kernel,
        grid=(s // tm,),
        in_specs=[row(D_MODEL), _const_spec(g.shape), _const_spec(win.shape)],
        out_specs=([row(c, d) for _ in range(3) for _, d in DIL_CONFIGS] + [row(c)] * 2
                   + [pl.BlockSpec((tm // tk, c, tk), lambda i: (i, 0, 0))]),
        out_shape=([jax.ShapeDtypeStruct((s // d, d * c), BF16) for _ in range(3) for _, d in DIL_CONFIGS]
                   + [jax.ShapeDtypeStruct((s, c), BF16)] * 2 + [jax.ShapeDtypeStruct((s // tk, c, tk), BF16)]),
        scratch_shapes=[pltpu.VMEM((c // LANES, tm, LANES), F32)],
        compiler_params=_params(("parallel",)),
        name="proj_odd",
    )(x, g, win)


def _dilated_kernel(bias_ref, q_ref, kp_ref, kc_ref, vp_ref, vc_ref, o_ref, lse_ref, *, nub):
    t, tb = DIL_T, DIL_TB
    ub = pl.program_id(0) % nub
    c = DIL_HEADS_PER_GROUP * HEAD_DIM
    qf = q_ref[...].astype(F32)
    kcat = jnp.concatenate([kp_ref[...], kc_ref[...]], axis=0)
    vcat = jnp.concatenate([vp_ref[...], vc_ref[...]], axis=0)
    lane = lax.broadcasted_iota(jnp.int32, (tb, c), 1)
    col = lax.broadcasted_iota(jnp.int32, (tb, t + tb), 1)
    no_prev = col < jnp.where(ub == 0, t, 0)
    out = jnp.zeros((tb, c), F32)
    lse = jnp.zeros((tb, c), F32)
    for hh in range(DIL_HEADS_PER_GROUP):
        mine = jnp.logical_and(lane >= hh * HEAD_DIM, lane < (hh + 1) * HEAD_DIM)
        qh = jnp.where(mine, qf, 0.0).astype(BF16)
        s = _dot_t(qh, kcat) + bias_ref[hh]
        s = jnp.where(no_prev, NEG, s)
        m = jnp.max(s, axis=1, keepdims=True)
        e = jnp.exp(s - m)
        den = jnp.sum(e, axis=1, keepdims=True)
        oh = _dot(e.astype(BF16), vcat) / den
        out = jnp.where(mine, oh, out)
        lse = jnp.where(mine, m + jnp.log(den), lse)
    o_ref[...] = out
    lse_ref[...] = lse


def _dilated_group(q, k, v, dil, slopes):
    c = q.shape[1] // dil
    s = q.shape[0] * dil
    t, tb = DIL_T, DIL_TB
    nub = s // dil // tb
    a = np.arange(tb)[:, None]
    cc = np.arange(t + tb)[None, :]
    steps = t + a - cc
    ok = (steps >= 0) & (steps <= t)
    dist = (steps * dil).astype(np.float32)
    bias = np.where(ok[None], -np.asarray(slopes, np.float32)[:, None, None] * dist[None], np.float32(NEG))
    bias = jnp.asarray(bias, F32)
    cur = pl.BlockSpec((tb, c), lambda b: (b % nub, b // nub))
    prev = pl.BlockSpec((t, c), lambda b: (jnp.maximum(b % nub * (tb // t) - 1, 0), b // nub))
    o, lse = pl.pallas_call(
        functools.partial(_dilated_kernel, nub=nub),
        grid=(dil * nub,),
        in_specs=[_const_spec(bias.shape), cur, prev, cur, prev, cur],
        out_specs=[cur, cur],
        out_shape=[jax.ShapeDtypeStruct((s // dil, dil * c), F32)] * 2,
        compiler_params=_params(("parallel",)),
        name=f"dilated_d{dil}",
    )(bias, q, k, k, v, v)
    return o, lse


def _sb_kernel(tri_ref, q_ref, k_ref, vt_ref, o_ref):
    hp = pl.program_id(0)
    i = pl.program_id(1)
    tq, tk = SB_TQ, ATT_TK
    ratio = tq // tk
    c = SB_HEADS * HEAD_DIM
    tri = tri_ref[...]
    lane = lax.broadcasted_iota(jnp.int32, (tq, c), 1)
    qf = q_ref[...].astype(F32)
    qhs = []
    for t in range(SB_GROUP):
        hh = hp * SB_GROUP + t
        mine = jnp.logical_and(lane >= hh * HEAD_DIM, lane < (hh + 1) * HEAD_DIM)
        qhs.append(jnp.where(mine, qf, 0.0).astype(BF16))
    strict = lax.broadcasted_iota(jnp.int32, (tk, tk), 0) < lax.broadcasted_iota(jnp.int32, (tk, tk), 1)
    n_full = i * ratio

    def weights(z, r, masked):
        sp = jnp.log1p(jnp.exp(-jnp.abs(z)))
        log_beta = jnp.minimum(z, 0.0) - sp
        log_keep = log_beta - z
        if masked:
            log_keep = jnp.where(strict, log_keep, 0.0)
        hi = log_keep.astype(BF16)
        lo = (log_keep - hi.astype(F32)).astype(BF16)
        aft = _dot(tri, jnp.concatenate([hi, lo], axis=0))
        a = jnp.exp(log_beta + aft[0:tk] + r)
        if masked:
            a = jnp.where(strict, a, 0.0)
        return a.astype(BF16), r + aft[tk:tk + 1]

    def diag_step(d, state):
        kt = _k_tile(k_ref, n_full + d)
        vt = vt_ref[n_full + d]
        out = []
        for t, (rs, accs) in enumerate(state):
            z = _dot_t(kt, qhs[t][d * tk:, :])
            rs, accs = list(rs), list(accs)
            for blk in range(d, ratio):
                a, rs[blk] = weights(z[:, (blk - d) * tk:(blk - d + 1) * tk], rs[blk], blk == d)
                accs[blk] = accs[blk] + _dot(vt[t * HEAD_DIM:(t + 1) * HEAD_DIM, :], a)
            out.append((rs, accs))
        return out

    state = [([jnp.zeros((1, tk), F32)] * ratio, [jnp.zeros((HEAD_DIM, tk), F32)] * ratio) for _ in range(SB_GROUP)]
    for d in reversed(range(ratio)):
        state = diag_step(d, state)
    carry = tuple((jnp.concatenate(rs, axis=1), jnp.concatenate(accs, axis=1)) for rs, accs in state)

    def step(j, cr):
        kt = _k_tile(k_ref, j)
        vt = vt_ref[j]
        out = []
        for t, (r, acc) in enumerate(cr):
            a, r = weights(_dot_t(kt, qhs[t]), r, False)
            out.append((r, acc + _dot(vt[t * HEAD_DIM:(t + 1) * HEAD_DIM, :], a)))
        return tuple(out)

    def more(state):
        n, cr = state
        top = functools.reduce(jnp.maximum, [jnp.max(r) for r, _ in cr])
        return jnp.logical_and(n < n_full, top > SB_STOP)

    def walk(state):
        n, cr = state
        return n + 1, step(n_full - 1 - n, cr)

    _, carry = lax.while_loop(more, walk, (jnp.int32(0), carry))
    o_ref[...] = jnp.concatenate([acc for _, acc in carry], axis=0)


def _stickbreak(q, k, vt):
    s, c = q.shape
    tq, tk = SB_TQ, ATT_TK
    j = np.arange(tk)
    later = (j[None, :] > j[:, None]).astype(np.float32)
    tri = np.concatenate([np.concatenate([later, later], axis=1), np.ones((BF16_ROWS, 2 * tk), np.float32)], axis=0)
    tri = jnp.asarray(tri, BF16)
    return pl.pallas_call(
        _sb_kernel,
        grid=(SB_HEADS // SB_GROUP, s // tq),
        in_specs=[_const_spec(tri.shape),
                  pl.BlockSpec((tq, c), lambda h, i: (i, 0)),
                  _const_spec((s, c)),
                  pl.BlockSpec((s // tk, SB_GROUP * HEAD_DIM, tk), lambda h, i: (0, h, 0))],
        out_specs=pl.BlockSpec((SB_GROUP * HEAD_DIM, tq), lambda h, i: (h, i)),
        out_shape=jax.ShapeDtypeStruct((c, s), F32),
        compiler_params=_params(("parallel", "parallel")),
        name="stickbreak",
    )(tri, q, k, vt)


def _alibi_slopes(n):
    return 2.0 ** (-8.0 * np.arange(1, n + 1) / n)


def _pad_cols(w, lo, width):
    return jnp.pad(w, ((0, 0), (lo, width - lo - w.shape[1])))


def _rot_half_cols(w):
    half = w.shape[1] // 2
    return jnp.concatenate([-w[:, half:], w[:, :half]], axis=1)


def _even_weights(w_in, w_uq, w_ukv):
    o_kr = MLA_Q_RANK + MLA_KV_RANK
    w_kr = w_in[:, o_kr:o_kr + MLA_ROPE]
    win = jnp.concatenate([w_in[:, :o_kr], _pad_cols(w_kr, MLA_NOPE, LANES),
                           _pad_cols(_rot_half_cols(w_kr), MLA_NOPE, LANES),
                           w_in[:, o_kr + MLA_ROPE:]], axis=1)
    hd = MLA_NOPE + MLA_ROPE
    q_plain, q_rot = [], []
    for hh in range(MLA_HEADS):
        wh = w_uq[:, hh * hd:(hh + 1) * hd]
        q_plain.append(_pad_cols(wh, 0, LANES))
        q_rot.append(_pad_cols(_rot_half_cols(wh[:, MLA_NOPE:]), MLA_NOPE, LANES))
    wuq = jnp.concatenate(q_plain + q_rot, axis=1)
    hk = MLA_NOPE + MLA_V
    k_cols = [_pad_cols(w_ukv[:, hh * hk:hh * hk + MLA_NOPE], 0, LANES) for hh in range(MLA_HEADS)]
    v_cols = [w_ukv[:, hh * hk + MLA_NOPE:(hh + 1) * hk] for hh in range(MLA_HEADS)]
    wkv = jnp.concatenate(k_cols + v_cols, axis=1)
    return win.astype(BF16), wuq.astype(BF16), wkv.astype(BF16)


def _rope_tables(s):
    half = MLA_ROPE // 2
    inv = ROPE_BASE ** (-jnp.arange(half, dtype=F32) / half)
    ang = jnp.arange(s).astype(F32)[:, None] * inv
    cos2 = jnp.tile(jnp.cos(ang), (1, 2))
    sin2 = jnp.tile(jnp.sin(ang), (1, 2))
    scale = (MLA_NOPE + MLA_ROPE) ** -0.5 * LOG2E
    tail = LANES - MLA_NOPE - MLA_ROPE
    ones, zeros, ztail = jnp.ones((s, MLA_NOPE), F32), jnp.zeros((s, MLA_NOPE), F32), jnp.zeros((s, tail), F32)
    cosq = jnp.concatenate([ones, cos2, ztail], axis=1) * scale
    sinq = jnp.concatenate([zeros, sin2, ztail], axis=1) * scale
    cosk = jnp.concatenate([zeros, cos2, ztail], axis=1)
    sink = jnp.concatenate([zeros, sin2, ztail], axis=1)
    return cosq, sinq, cosk, sink


def _row2d(v):
    return v.reshape(1, -1).astype(F32)


def kernel(x, ev_pre_g, ev_w_in, ev_cq_g, ev_w_uq, ev_ckv_g, ev_w_ukv, ev_lam_q1, ev_lam_k1, ev_lam_q2,
           ev_lam_k2, ev_subln_g, ev_w_out, ev_post_g, od_pre_g, od_w_in, od_w_out, od_post_g, ffn_pre_g,
           ffn_w_up, ffn_conv_w, ffn_conv_b, ffn_w_down, ffn_post_g):
    b, s, _ = x.shape
    assert b == 1 and s % max(ROW_TILE, ATT_TQ, SB_TQ, DIL_TB * DIL_CONFIGS[-1][1]) == 0
    assert ROW_TILE % ATT_TK == 0 and ATT_TQ % ATT_TK == 0 and SB_TQ % ATT_TK == 0
    assert (ATT_TQ // ATT_TK) % MAIN_UNROLL == 0 and DIL_TB % DIL_T == 0
    assert sum(FF_CHUNKS) == D_FF
    assert ATT_TK <= 256
    xs = x[0]

    def ffn_consts(i):
        return [_row2d(ffn_pre_g[i]), ffn_w_up[i].astype(BF16), ffn_conv_w[i].astype(F32),
                _row2d(ffn_conv_b[i]), ffn_w_down[i].astype(BF16), _row2d(ffn_post_g[i])]

    win, wuq, wkv = _even_weights(ev_w_in[0], ev_w_uq[0], ev_w_ukv[0])
    cosq, sinq, cosk, sink = _rope_tables(s)
    pos_col = (jnp.arange(ROW_TILE) % ATT_TK).astype(BF16)[:, None]
    pos_cols = jnp.pad(jnp.tile(pos_col, (1, SLOPE_PARTS)), ((0, 0), (0, DIFF_QK - SLOPE_PARTS)))
    pos_tile = jnp.tile(pos_cols, (1, LANES // DIFF_QK))
    qm, km, vmt, dq, dk, dvt, kn_tiles = _proj_even(xs, _row2d(ev_pre_g[0]), win, _row2d(ev_cq_g[0]), wuq,
                                          _row2d(ev_ckv_g[0]), wkv, cosq, sinq, cosk, sink, pos_tile)
    o_mla = _mla_attention(qm, km, vmt)
    lam_p = jnp.pad(jnp.stack([ev_lam_q1[0], ev_lam_k1[0], ev_lam_q2[0], ev_lam_k2[0]]).astype(F32),
                    ((0, SUBLANES - 4), (0, LANES - DIFF_QK)))
    d_slopes = _alibi_slopes(DIFF_HEADS) * LOG2E
    slope_arr = jnp.asarray(np.broadcast_to(d_slopes[:, None, None], (DIFF_HEADS, SUBLANES, LANES)), F32)
    qfeat = jnp.zeros((DIFF_HEADS, ATT_TQ, LANES), F32)
    rest = jnp.asarray(d_slopes, F32)
    for part in range(SLOPE_PARTS):
        piece = rest.astype(BF16).astype(F32)
        qfeat = qfeat.at[:, :, part].set(piece[:, None]).at[:, :, DIFF_QK + part].set(piece[:, None])
        rest = rest - piece
    lam_init = 0.8 - 0.6 * math.exp(-0.3 * 0)
    knorm = jnp.pad(jnp.max(kn_tiles, axis=0).reshape(DIFF_HEADS, 2, LANES), ((0, 0), (0, SUBLANES - 2), (0, 0)))
    o_diff = _diff_attention(lam_p, slope_arr, knorm, qfeat.astype(BF16), _row2d(ev_subln_g[0]),
                             dq, dk, dvt, lam_init)
    x2 = _out_ffn(_out_ffn_even_kernel, "out_ffn_even", xs, [o_mla, o_diff], [],
                  [ev_w_out[0].astype(BF16), _row2d(ev_post_g[0])] + ffn_consts(0))

    p = _proj_odd(x2, _row2d(od_pre_g[0]), od_w_in[0].astype(BF16))
    slopes = _alibi_slopes(len(DIL_CONFIGS) * DIL_HEADS_PER_GROUP).reshape(len(DIL_CONFIGS), -1)
    dil_outs = []
    for gi, (_, dil) in enumerate(DIL_CONFIGS):
        dil_outs += list(_dilated_group(p[gi], p[3 + gi], p[6 + gi], dil, slopes[gi]))
    o_sbt = _stickbreak(p[9], p[10], p[11])
    lane_tiles = DIL_HEADS_PER_GROUP * HEAD_DIM // LANES
    x4 = _out_ffn(_out_ffn_odd_kernel, "out_ffn_odd", x2, dil_outs, [o_sbt],
                  [od_w_out[0].astype(BF16), _row2d(od_post_g[0])] + ffn_consts(1),
                  scratch=[pltpu.VMEM((lane_tiles, ROW_TILE, LANES), F32)])
    return x4[None]
```

```python
import functools
import math

import numpy as np
import jax
import jax.numpy as jnp
from jax import lax
from jax.experimental import pallas as pl
from jax.experimental.pallas import tpu as pltpu

F32 = jnp.float32
BF16 = jnp.bfloat16

D_MODEL = 1024
NORM_EPS = 1e-6
MLA_HEADS = 8
MLA_Q_RANK = 256
MLA_KV_RANK = 128
MLA_NOPE = 64
MLA_ROPE = 32
MLA_V = 64
ROPE_BASE = 10000.0
DIFF_HEADS = 4
DIFF_QK = 64
DIFF_V = 128
DIL_CONFIGS = ((128, 1), (512, 4), (2048, 16))
DIL_HEADS_PER_GROUP = 4
HEAD_DIM = 64
SB_HEADS = 4
D_FF = 2816
CONV_WIDTH = 3

LANES = 128
SUBLANES = 8
BF16_ROWS = 16
VMEM_LIMIT_BYTES = 56 * 1024 * 1024
ROW_TILE = 512
ATT_TQ = 2048
MAIN_UNROLL = 8
SB_TQ = 512
SB_GROUP = 4
ATT_TK = 256
DIL_T = 128
DIL_TB = 512
FF_CHUNKS = (512,) * 5 + (256,)
NEG = -1e30
LOG2E = 1.4426950408889634
UNDERFLOW_LOG2 = -152.0
MAX_JUMP = 32.0
SB_STOP = -110.0
SLOPE_PARTS = 3

_TRANS_B = (((1,), (1,)), ((), ()))


def _dot(a, b):
    return jnp.dot(a, b, preferred_element_type=F32)


def _dot_t(a, b):
    return lax.dot_general(a, b, _TRANS_B, preferred_element_type=F32)


def _rms(xf, g):
    ms = jnp.mean(xf * xf, axis=-1, keepdims=True)
    return xf * lax.rsqrt(ms + NORM_EPS) * g


def _params(sem):
    return pltpu.CompilerParams(dimension_semantics=sem, vmem_limit_bytes=VMEM_LIMIT_BYTES)


def _const_spec(shape):
    nd = len(shape)
    return pl.BlockSpec(shape, lambda *_: (0,) * nd, pipeline_mode=pl.Buffered(1))


def _store_transposed_tiles(dst_ref, src, heads, width, ones_rows):
    tk = ATT_TK
    per = width + ones_rows
    for b in range(src.shape[0] // tk):
        vt = src[b * tk:(b + 1) * tk, :].T.astype(BF16)
        for hh in range(heads):
            dst_ref[b, hh * per:hh * per + width, :] = vt[hh * width:(hh + 1) * width, :]
            if ones_rows:
                dst_ref[b, hh * per + width:(hh + 1) * per, :] = jnp.ones((ones_rows, tk), BF16)


def _rows_to_classes(x, dil, scr):
    if dil == 1:
        return x
    tm, c = x.shape
    for h in range(c // LANES):
        scr[h] = x[:, h * LANES:(h + 1) * LANES]
    return jnp.concatenate([scr[h, pl.ds(r, tm // dil, stride=dil), :]
                            for r in range(dil) for h in range(c // LANES)], axis=1)


def _classes_to_rows(ref, dil, scr):
    if dil == 1:
        return ref[...]
    n, c = ref.shape[0], ref.shape[1] // dil
    for r in range(dil):
        for h in range(c // LANES):
            scr[h, pl.ds(r, n, stride=dil), :] = ref[:, r * c + h * LANES:r * c + (h + 1) * LANES]
    return jnp.concatenate([scr[h] for h in range(c // LANES)], axis=1)


def _proj_even_kernel(x_ref, g_ref, win_ref, cqg_ref, wuq_ref, ckvg_ref, wkv_ref,
                      cosq_ref, sinq_ref, cosk_ref, sink_ref, pos_ref,
                      qm_ref, km_ref, vmt_ref, dq_ref, dk_ref, dvt_ref, kn_ref):
    h = _rms(x_ref[...], g_ref[...]).astype(BF16)
    proj = _dot(h, win_ref[...])
    nq = MLA_HEADS * LANES
    cq = _rms(proj[:, 0:MLA_Q_RANK], cqg_ref[...]).astype(BF16)
    qq = _dot(cq, wuq_ref[...])
    cosq, sinq = cosq_ref[...], sinq_ref[...]
    for hh in range(MLA_HEADS):
        a = qq[:, hh * LANES:(hh + 1) * LANES]
        b = qq[:, nq + hh * LANES:nq + (hh + 1) * LANES]
        qm_ref[:, hh * LANES:(hh + 1) * LANES] = (a * cosq + b * sinq).astype(BF16)
    o = MLA_Q_RANK
    ckv = _rms(proj[:, o:o + MLA_KV_RANK], ckvg_ref[...]).astype(BF16)
    kv = _dot(ckv, wkv_ref[...])
    o += MLA_KV_RANK
    krc = proj[:, o:o + LANES] * cosk_ref[...] + proj[:, o + LANES:o + 2 * LANES] * sink_ref[...]
    for hh in range(MLA_HEADS):
        km_ref[:, hh * LANES:(hh + 1) * LANES] = (kv[:, hh * LANES:(hh + 1) * LANES] + krc).astype(BF16)
    _store_transposed_tiles(vmt_ref, kv[:, nq:nq + MLA_HEADS * MLA_V], MLA_HEADS, MLA_V, BF16_ROWS)
    o += 2 * LANES
    nd = DIFF_HEADS * 2 * DIFF_QK
    dq_ref[...] = (proj[:, o:o + nd] * (DIFF_QK ** -0.5 * LOG2E)).astype(BF16)
    o += nd
    pos = pos_ref[...]
    lane = lax.broadcasted_iota(jnp.int32, (x_ref.shape[0], LANES), 1)
    norms = []
    for hh in range(DIFF_HEADS):
        kb = proj[:, o + hh * LANES:o + (hh + 1) * LANES].astype(BF16)
        dk_ref[:, hh * 2 * LANES:hh * 2 * LANES + LANES] = jnp.where(lane < DIFF_QK, kb, pos)
        dk_ref[:, hh * 2 * LANES + LANES:(hh + 1) * 2 * LANES] = jnp.where(lane >= DIFF_QK, kb, pos)
        kk = kb.astype(F32) * kb.astype(F32)
        for first in (True, False):
            part = jnp.where((lane < DIFF_QK) == first, kk, 0.0)
            top = jnp.max(jnp.sum(part, axis=1, keepdims=True), axis=0, keepdims=True)
            norms.append(jnp.broadcast_to(top, (1, LANES)))
    kn_ref[0] = jnp.concatenate(norms, axis=0)
    o += nd
    _store_transposed_tiles(dvt_ref, proj[:, o:o + DIFF_HEADS * DIFF_V], DIFF_HEADS, DIFF_V, BF16_ROWS)


def _proj_even(x, g, win, cqg, wuq, ckvg, wkv, cosq, sinq, cosk, sink, pos):
    s = x.shape[0]
    tm, tk = ROW_TILE, ATT_TK
    row = lambda c: pl.BlockSpec((tm, c), lambda i: (i, 0))
    tile_t = lambda r: pl.BlockSpec((tm // tk, r, tk), lambda i: (i, 0, 0))
    mla_rows = MLA_HEADS * (MLA_V + BF16_ROWS)
    diff_rows = DIFF_HEADS * (DIFF_V + BF16_ROWS)
    flat = lambda c: jax.ShapeDtypeStruct((s, c), BF16)
    tiled = lambda r: jax.ShapeDtypeStruct((s // tk, r, tk), BF16)
    return pl.pallas_call(
        _proj_even_kernel,
        grid=(s // tm,),
        in_specs=[row(D_MODEL), _const_spec(g.shape), _const_spec(win.shape), _const_spec(cqg.shape),
                  _const_spec(wuq.shape), _const_spec(ckvg.shape), _const_spec(wkv.shape),
                  row(LANES), row(LANES), row(LANES), row(LANES), _const_spec(pos.shape)],
        out_specs=[row(MLA_HEADS * LANES), row(MLA_HEADS * LANES), tile_t(mla_rows),
                   row(DIFF_HEADS * 2 * DIFF_QK), row(DIFF_HEADS * 2 * LANES), tile_t(diff_rows),
                   pl.BlockSpec((1, 2 * DIFF_HEADS, LANES), lambda i: (i, 0, 0))],
        out_shape=[flat(MLA_HEADS * LANES), flat(MLA_HEADS * LANES), tiled(mla_rows),
                   flat(DIFF_HEADS * 2 * DIFF_QK), flat(DIFF_HEADS * 2 * LANES), tiled(diff_rows),
                   jax.ShapeDtypeStruct((s // tm, 2 * DIFF_HEADS, LANES), F32)],
        compiler_params=_params(("parallel",)),
        name="proj_even",
    )(x, g, win, cqg, wuq, ckvg, wkv, cosq, sinq, cosk, sink, pos)


def _flash_cols(chains, i, rows, tile_bias=None, scratch=None, reach=None):
    tq, tk = ATT_TQ, ATT_TK
    ratio = tq // tk
    n_full = i * ratio

    def safe_step(j, carry):
        out = []
        for (qh, load_k, load_vt), (m, acc) in zip(chains, carry):
            s = _dot(load_k(j), qh)
            mb = jnp.max(s, axis=0, keepdims=True)
            if tile_bias is not None:
                c = tile_bias(j)
                mb = mb + c
            m_new = jnp.maximum(m, mb)
            alpha = jnp.exp2(m - m_new)
            shift = m_new if tile_bias is None else m_new - c
            p = jnp.exp2(s - shift).astype(BF16)
            out.append((m_new, alpha * acc + _dot(load_vt(j), p)))
        return tuple(out)

    def fast_step(n, carry):
        j = n_full - 1 - n
        out = []
        for ci, ((qh, load_k, load_vt), (m, alpha, acc, jump)) in enumerate(zip(chains, carry)):
            acc = (acc + _dot(load_vt(j + 1), scratch[ci])) * alpha
            s = _dot(load_k(j), qh)
            c = None if tile_bias is None else tile_bias(j)
            shift = m if c is None else m - c
            scratch[ci] = jnp.exp2(s - shift).astype(BF16)
            mb = jnp.max(s, axis=0, keepdims=True)
            if c is not None:
                mb = mb + c
            m_new = jnp.maximum(m, mb)
            out.append((m_new, jnp.exp2(m - m_new), acc, jnp.maximum(jump, mb - m)))
        return tuple(out)

    def fast_group(g, c):
        for t in range(MAIN_UNROLL):
            c = fast_step(g * MAIN_UNROLL + t, c)
        return c

    def diag_step(d, state):
        tri = lax.broadcasted_iota(jnp.int32, (tk, tk), 0) <= lax.broadcasted_iota(jnp.int32, (tk, tk), 1)
        out = []
        for (qh, load_k, load_vt), (ms, accs) in zip(chains, state):
            s = _dot(load_k(n_full + d), qh[:, d * tk:])
            vt = load_vt(n_full + d)
            c = None if tile_bias is None else tile_bias(n_full + d)
            ms, accs = list(ms), list(accs)
            for blk in range(d, ratio):
                sb = s[:, (blk - d) * tk:(blk - d + 1) * tk]
                if blk == d:
                    sb = jnp.where(tri, sb, NEG)
                mb = jnp.max(sb, axis=0, keepdims=True)
                if c is not None:
                    mb = mb + c
                m_new = jnp.maximum(ms[blk], mb)
                shift = m_new if c is None else m_new - c
                p = jnp.exp2(sb - shift).astype(BF16)
                accs[blk] = jnp.exp2(ms[blk] - m_new) * accs[blk] + _dot(vt, p)
                ms[blk] = m_new
            out.append((ms, accs))
        return out

    state = [([jnp.full((1, tk), NEG, F32)] * ratio, [jnp.zeros((rows, tk), F32)] * ratio) for _ in chains]
    for d in range(ratio):
        state = diag_step(d, state)
    diag = tuple((jnp.concatenate(ms, axis=1), jnp.concatenate(accs, axis=1)) for ms, accs in state)
    for ci in range(len(chains)):
        scratch[ci] = jnp.zeros((tk, tq), BF16)
    one, low = jnp.ones((1, tq), F32), jnp.full((1, tq), NEG, F32)
    groups = n_full // MAIN_UNROLL
    start = tuple((m, one, acc, low) for m, acc in diag)
    if reach is None:
        done, fast = groups, lax.fori_loop(0, groups, fast_group, start)
    else:
        def live(state):
            g, c = state
            nxt = tile_bias(n_full - 1 - g * MAIN_UNROLL)
            gap = functools.reduce(jnp.maximum, [jnp.max(b + nxt) - jnp.min(m)
                                                 for b, (m, _, _, _) in zip(reach, c)])
            return jnp.logical_and(g < groups, gap > UNDERFLOW_LOG2)

        done, fast = lax.while_loop(live, lambda st: (st[0] + 1, fast_group(st[0], st[1])), (jnp.int32(0), start))
    last = n_full - done * MAIN_UNROLL
    accs = [(acc + _dot(load_vt(last), scratch[ci])) * alpha
            for ci, ((_, _, load_vt), (_, alpha, acc, _)) in enumerate(zip(chains, fast))]
    worst = functools.reduce(jnp.maximum, [jnp.max(jump) for _, _, _, jump in fast])

    def redo():
        carry = lax.fori_loop(0, n_full, safe_step, diag)
        return [acc for _, acc in carry]

    return lax.cond(worst > MAX_JUMP, redo, lambda: accs)


def _k_tile(k_ref, j, lanes=slice(None)):
    return k_ref[pl.ds(pl.multiple_of(j * ATT_TK, ATT_TK), ATT_TK), lanes]


def _mla_kernel(q_ref, k_ref, vt_ref, o_ref, p_ref):
    i = pl.program_id(1)
    rows = MLA_V + BF16_ROWS
    def chain(hh):
        lanes = slice(hh * LANES, (hh + 1) * LANES)
        return (q_ref[:, lanes].astype(F32).T.astype(BF16), lambda j: _k_tile(k_ref, j, lanes),
                lambda j: vt_ref[j, hh * rows:(hh + 1) * rows, :])

    accs = _flash_cols([chain(0), chain(1)], i, rows, scratch=p_ref)
    halves = [acc[0:MLA_V] / acc[MLA_V:MLA_V + 1] for acc in accs]
    o_ref[...] = jnp.concatenate(halves, axis=0).T.astype(BF16)


def _mla_attention(qm, km, vmt):
    s = qm.shape[0]
    tq = ATT_TQ
    rows = 2 * (MLA_V + BF16_ROWS)
    return pl.pallas_call(
        _mla_kernel,
        grid=(MLA_HEADS // 2, s // tq),
        in_specs=[pl.BlockSpec((tq, 2 * LANES), lambda p, i: (i, p)),
                  pl.BlockSpec((s, 2 * LANES), lambda p, i: (0, p), pipeline_mode=pl.Buffered(1)),
                  pl.BlockSpec((vmt.shape[0], rows, ATT_TK), lambda p, i: (0, p, 0), pipeline_mode=pl.Buffered(1))],
        out_specs=pl.BlockSpec((tq, LANES), lambda p, i: (i, p)),
        out_shape=jax.ShapeDtypeStruct((s, MLA_HEADS * MLA_V), BF16),
        scratch_shapes=[pltpu.VMEM((2, ATT_TK, tq), BF16)],
        compiler_params=_params(("parallel", "parallel")),
        name="mla_attn",
    )(qm, km, vmt)


def _diff_kernel(lam_ref, slope_ref, kn_ref, qf_ref, g_ref, q_ref, k_ref, vt_ref, o_ref, p_ref, *, lam_init):
    i = pl.program_id(1)
    tq = ATT_TQ
    lane = lax.broadcasted_iota(jnp.int32, (tq, LANES), 1)
    qf32 = q_ref[...].astype(F32)
    feat = qf_ref[...]
    zero = jnp.zeros_like(qf32)
    featf = feat.astype(F32)
    qa = jnp.where(lane < DIFF_QK, qf32, featf).T.astype(BF16)
    qb = jnp.where(lane >= DIFF_QK, qf32, featf).T.astype(BF16)
    slope = slope_ref[0:1, 0:1]
    rows = DIFF_V + BF16_ROWS
    tile_bias = lambda j: slope * (j * ATT_TK - i * tq).astype(F32)
    load_k = [lambda j, x=x: _k_tile(k_ref, j, slice(x * LANES, (x + 1) * LANES)) for x in range(2)]
    load_vt = lambda j: vt_ref[j]
    q2 = qf32 * qf32
    top = lambda z: jnp.max(jnp.sum(z, axis=1, keepdims=True), axis=0, keepdims=True)
    qn = (top(jnp.where(lane < DIFF_QK, q2, zero)), top(jnp.where(lane >= DIFF_QK, q2, zero)))
    reach = [jnp.sqrt(qn[x] * kn_ref[x:x + 1, 0:1]) * 1.001 + slope * ATT_TK for x in range(2)]
    acc1, acc2 = _flash_cols([(qa, load_k[0], load_vt), (qb, load_k[1], load_vt)], i, rows, tile_bias, p_ref, reach)
    lp = lam_ref[...]
    s1 = jnp.sum(lp[0:1, :] * lp[1:2, :], axis=1, keepdims=True)
    s2 = jnp.sum(lp[2:3, :] * lp[3:4, :], axis=1, keepdims=True)
    lam = jnp.exp(s1) - jnp.exp(s2) + lam_init
    ot = acc1[0:DIFF_V] / acc1[DIFF_V:DIFF_V + 1] - lam * (acc2[0:DIFF_V] / acc2[DIFF_V:DIFF_V + 1])
    o_ref[...] = (_rms(ot.T, g_ref[...]) * (1.0 - lam_init)).astype(BF16)


def _diff_attention(lam_p, slope_arr, knorm, qfeat, subln_g, dq, dk, dvt, lam_init):
    s = dq.shape[0]
    tq = ATT_TQ
    return pl.pallas_call(
        functools.partial(_diff_kernel, lam_init=lam_init),
        grid=(DIFF_HEADS, s // tq),
        in_specs=[pl.BlockSpec(lam_p.shape, lambda h, i: (0, 0)),
                  pl.BlockSpec((None, SUBLANES, LANES), lambda h, i: (h, 0, 0)),
                  pl.BlockSpec((None, SUBLANES, LANES), lambda h, i: (h, 0, 0)),
                  pl.BlockSpec((None, tq, LANES), lambda h, i: (h, 0, 0)),
                  pl.BlockSpec(subln_g.shape, lambda h, i: (0, 0)),
                  pl.BlockSpec((tq, LANES), lambda h, i: (i, h)),
                  pl.BlockSpec((s, 2 * LANES), lambda h, i: (0, h), pipeline_mode=pl.Buffered(1)),
                  pl.BlockSpec((dvt.shape[0], DIFF_V + BF16_ROWS, ATT_TK), lambda h, i: (0, h, 0),
                               pipeline_mode=pl.Buffered(1))],
        out_specs=pl.BlockSpec((tq, LANES), lambda h, i: (i, h)),
        out_shape=jax.ShapeDtypeStruct((s, DIFF_HEADS * DIFF_V), BF16),
        scratch_shapes=[pltpu.VMEM((2, ATT_TK, tq), BF16)],
        compiler_params=_params(("parallel", "parallel")),
        name="diff_attn",
    )(lam_p, slope_arr, knorm, qfeat, subln_g, dq, dk, dvt)


def _shift_rows(u, k, prev):
    top = jnp.where(lax.broadcasted_iota(jnp.int32, prev.shape, 0) < k,
                    pltpu.roll(prev, k, 0), pltpu.roll(u[0:SUBLANES, :], k, 0))
    return jnp.concatenate([top, pltpu.roll(u, k, 0)[SUBLANES:, :]], axis=0)


def _ffn_tile(x1, fpre, wup_ref, cw_ref, cb_ref, wdn_ref, fpost, carry_ref):
    tm = x1.shape[0]
    hf = _rms(x1, fpre).astype(BF16)
    acts = []
    lo = 0
    for width in FF_CHUNKS:
        ys = []
        for part in range(2):
            c0 = part * D_FF + lo
            u = _dot(hf, wup_ref[:, c0:c0 + width])
            prev = carry_ref[:, c0:c0 + width]
            carry_ref[:, c0:c0 + width] = u[tm - SUBLANES:, :]
            ys.append(cw_ref[2:3, c0:c0 + width] * u
                      + cw_ref[1:2, c0:c0 + width] * _shift_rows(u, 1, prev)
                      + cw_ref[0:1, c0:c0 + width] * _shift_rows(u, 2, prev)
                      + cb_ref[:, c0:c0 + width])
        gate, up = ys
        acts.append((gate * (1.0 / (1.0 + jnp.exp(-gate))) * up).astype(BF16))
        lo += width
    acc = _dot(jnp.concatenate(acts, axis=1), wdn_ref[...])
    return x1 + _rms(acc, fpost)


def _out_ffn_even_kernel(x_ref, a_ref, b_ref, wout_ref, postg_ref, fpre_ref, wup_ref, cw_ref, cb_ref,
                         wdn_ref, fpost_ref, o_ref, carry_ref):
    @pl.when(pl.program_id(0) == 0)
    def _():
        carry_ref[...] = jnp.zeros_like(carry_ref)

    na = a_ref.shape[1]
    mix = _dot(a_ref[...], wout_ref[0:na, :]) + _dot(b_ref[...], wout_ref[na:, :])
    x1 = x_ref[...] + _rms(mix, postg_ref[...])
    o_ref[...] = _ffn_tile(x1, fpre_ref[...], wup_ref, cw_ref, cb_ref, wdn_ref, fpost_ref[...], carry_ref)


def _out_ffn_odd_kernel(x_ref, o0_ref, l0_ref, o1_ref, l1_ref, o2_ref, l2_ref, bt_ref, wout_ref, postg_ref,
                        fpre_ref, wup_ref, cw_ref, cb_ref, wdn_ref, fpost_ref, o_ref, carry_ref, scr):
    @pl.when(pl.program_id(0) == 0)
    def _():
        carry_ref[...] = jnp.zeros_like(carry_ref)

    (l0, o0), (l1, o1), (l2, o2) = [(_classes_to_rows(l_ref, d, scr), _classes_to_rows(og_ref, d, scr))
                                    for (l_ref, og_ref), (_, d) in zip(((l0_ref, o0_ref), (l1_ref, o1_ref),
                                                                        (l2_ref, o2_ref)), DIL_CONFIGS)]
    m = jnp.maximum(jnp.maximum(l0, l1), l2)
    e0, e1, e2 = jnp.exp(l0 - m), jnp.exp(l1 - m), jnp.exp(l2 - m)
    dil = (e0 * o0 + e1 * o1 + e2 * o2) / (e0 + e1 + e2)
    na = dil.shape[1]
    sb = bt_ref[...].T
    mix = _dot(dil.astype(BF16), wout_ref[0:na, :]) + _dot(sb.astype(BF16), wout_ref[na:, :])
    x1 = x_ref[...] + _rms(mix, postg_ref[...])
    o_ref[...] = _ffn_tile(x1, fpre_ref[...], wup_ref, cw_ref, cb_ref, wdn_ref, fpost_ref[...], carry_ref)


def _out_ffn(kernel, name, x, acts, acts_t, consts, scratch=()):
    s = x.shape[0]
    tm = ROW_TILE
    row = lambda c, n=s: pl.BlockSpec((tm * n // s, c), lambda i: (i, 0))
    col = lambda r: pl.BlockSpec((r, tm), lambda i: (0, i))
    return pl.pallas_call(
        kernel,
        grid=(s // tm,),
        in_specs=([row(D_MODEL)] + [row(a.shape[1], a.shape[0]) for a in acts] + [col(a.shape[0]) for a in acts_t]
                  + [_const_spec(c.shape) for c in consts]),
        out_specs=row(D_MODEL),
        out_shape=jax.ShapeDtypeStruct((s, D_MODEL), F32),
        scratch_shapes=[pltpu.VMEM((SUBLANES, 2 * D_FF), F32), *scratch],
        compiler_params=_params(("arbitrary",)),
        name=name,
    )(x, *acts, *acts_t, *consts)


def _proj_odd_kernel(x_ref, g_ref, win_ref, *refs):
    out_refs, scr = refs[:-1], refs[-1]
    h = _rms(x_ref[...], g_ref[...]).astype(BF16)
    proj = _dot(h, win_ref[...])
    c = DIL_HEADS_PER_GROUP * HEAD_DIM
    scale = HEAD_DIM ** -0.5
    for n, ref in enumerate(out_refs[:-1]):
        blk = proj[:, n * c:(n + 1) * c]
        if n < 3 or n == 9:
            blk = blk * scale
        if n < 9:
            blk = _rows_to_classes(blk, DIL_CONFIGS[n % 3][1], scr)
        ref[...] = blk.astype(BF16)
    _store_transposed_tiles(out_refs[-1], proj[:, 11 * c:12 * c], SB_HEADS, HEAD_DIM, 0)


def _proj_odd(x, g, win):
    s = x.shape[0]
    tm, tk = ROW_TILE, ATT_TK
    c = DIL_HEADS_PER_GROUP * HEAD_DIM
    row = lambda w, d=1: pl.BlockSpec((tm // d, d * w), lambda i: (i, 0))
    return pl.pallas_call(
        _proj_odd_kernel,
        grid=(s // tm,),
        in_specs=[row(D_MODEL), _const_spec(g.shape), _const_spec(win.shape)],
        out_specs=([row(c, d) for _ in range(3) for _, d in DIL_CONFIGS] + [row(c)] * 2
                   + [pl.BlockSpec((tm // tk, c, tk), lambda i: (i, 0, 0))]),
        out_shape=([jax.ShapeDtypeStruct((s // d, d * c), BF16) for _ in range(3) for _, d in DIL_CONFIGS]
                   + [jax.ShapeDtypeStruct((s, c), BF16)] * 2 + [jax.ShapeDtypeStruct((s // tk, c, tk), BF16)]),
        scratch_shapes=[pltpu.VMEM((c // LANES, tm, LANES), F32)],
        compiler_params=_params(("parallel",)),
        name="proj_odd",
    )(x, g, win)


def _dilated_kernel(bias_ref, q_ref, kp_ref, kc_ref, vp_ref, vc_ref, o_ref, lse_ref, *, nub):
    t, tb = DIL_T, DIL_TB
    ub = pl.program_id(0) % nub
    c = DIL_HEADS_PER_GROUP * HEAD_DIM
    qf = q_ref[...].astype(F32)
    kcat = jnp.concatenate([kp_ref[...], kc_ref[...]], axis=0)
    vcat = jnp.concatenate([vp_ref[...], vc_ref[...]], axis=0)
    lane = lax.broadcasted_iota(jnp.int32, (tb, c), 1)
    col = lax.broadcasted_iota(jnp.int32, (tb, t + tb), 1)
    no_prev = col < jnp.where(ub == 0, t, 0)
    out = jnp.zeros((tb, c), F32)
    lse = jnp.zeros((tb, c), F32)
    for hh in range(DIL_HEADS_PER_GROUP):
        mine = jnp.logical_and(lane >= hh * HEAD_DIM, lane < (hh + 1) * HEAD_DIM)
        qh = jnp.where(mine, qf, 0.0).astype(BF16)
        s = _dot_t(qh, kcat) + bias_ref[hh]
        s = jnp.where(no_prev, NEG, s)
        m = jnp.max(s, axis=1, keepdims=True)
        e = jnp.exp(s - m)
        den = jnp.sum(e, axis=1, keepdims=True)
        oh = _dot(e.astype(BF16), vcat) / den
        out = jnp.where(mine, oh, out)
        lse = jnp.where(mine, m + jnp.log(den), lse)
    o_ref[...] = out
    lse_ref[...] = lse


def _dilated_group(q, k, v, dil, slopes):
    c = q.shape[1] // dil
    s = q.shape[0] * dil
    t, tb = DIL_T, DIL_TB
    nub = s // dil // tb
    a = np.arange(tb)[:, None]
    cc = np.arange(t + tb)[None, :]
    steps = t + a - cc
    ok = (steps >= 0) & (steps <= t)
    dist = (steps * dil).astype(np.float32)
    bias = np.where(ok[None], -np.asarray(slopes, np.float32)[:, None, None] * dist[None], np.float32(NEG))
    bias = jnp.asarray(bias, F32)
    cur = pl.BlockSpec((tb, c), lambda b: (b % nub, b // nub))
    prev = pl.BlockSpec((t, c), lambda b: (jnp.maximum(b % nub * (tb // t) - 1, 0), b // nub))
    o, lse = pl.pallas_call(
        functools.partial(_dilated_kernel, nub=nub),
        grid=(dil * nub,),
        in_specs=[_const_spec(bias.shape), cur, prev, cur, prev, cur],
        out_specs=[cur, cur],
        out_shape=[jax.ShapeDtypeStruct((s // dil, dil * c), F32)] * 2,
        compiler_params=_params(("parallel",)),
        name=f"dilated_d{dil}",
    )(bias, q, k, k, v, v)
    return o, lse


def _sb_kernel(tri_ref, q_ref, k_ref, vt_ref, o_ref):
    hp = pl.program_id(0)
    i = pl.program_id(1)
    tq, tk = SB_TQ, ATT_TK
    ratio = tq // tk
    c = SB_HEADS * HEAD_DIM
    tri = tri_ref[...]
    lane = lax.broadcasted_iota(jnp.int32, (tq, c), 1)
    qf = q_ref[...].astype(F32)
    qhs = []
    for t in range(SB_GROUP):
        hh = hp * SB_GROUP + t
        mine = jnp.logical_and(lane >= hh * HEAD_DIM, lane < (hh + 1) * HEAD_DIM)
        qhs.append(jnp.where(mine, qf, 0.0).astype(BF16))
    strict = lax.broadcasted_iota(jnp.int32, (tk, tk), 0) < lax.broadcasted_iota(jnp.int32, (tk, tk), 1)
    n_full = i * ratio

    def weights(z, r, masked):
        sp = jnp.log1p(jnp.exp(-jnp.abs(z)))
        log_beta = jnp.minimum(z, 0.0) - sp
        log_keep = log_beta - z
        if masked:
            log_keep = jnp.where(strict, log_keep, 0.0)
        hi = log_keep.astype(BF16)
        lo = (log_keep - hi.astype(F32)).astype(BF16)
        aft = _dot(tri, jnp.concatenate([hi, lo], axis=0))
        a = jnp.exp(log_beta + aft[0:tk] + r)
        if masked:
            a = jnp.where(strict, a, 0.0)
        return a.astype(BF16), r + aft[tk:tk + 1]

    def diag_step(d, state):
        kt = _k_tile(k_ref, n_full + d)
        vt = vt_ref[n_full + d]
        out = []
        for t, (rs, accs) in enumerate(state):
            z = _dot_t(kt, qhs[t][d * tk:, :])
            rs, accs = list(rs), list(accs)
            for blk in range(d, ratio):
                a, rs[blk] = weights(z[:, (blk - d) * tk:(blk - d + 1) * tk], rs[blk], blk == d)
                accs[blk] = accs[blk] + _dot(vt[t * HEAD_DIM:(t + 1) * HEAD_DIM, :], a)
            out.append((rs, accs))
        return out

    state = [([jnp.zeros((1, tk), F32)] * ratio, [jnp.zeros((HEAD_DIM, tk), F32)] * ratio) for _ in range(SB_GROUP)]
    for d in reversed(range(ratio)):
        state = diag_step(d, state)
    carry = tuple((jnp.concatenate(rs, axis=1), jnp.concatenate(accs, axis=1)) for rs, accs in state)

    def step(j, cr):
        kt = _k_tile(k_ref, j)
        vt = vt_ref[j]
        out = []
        for t, (r, acc) in enumerate(cr):
            a, r = weights(_dot_t(kt, qhs[t]), r, False)
            out.append((r, acc + _dot(vt[t * HEAD_DIM:(t + 1) * HEAD_DIM, :], a)))
        return tuple(out)

    def more(state):
        n, cr = state
        top = functools.reduce(jnp.maximum, [jnp.max(r) for r, _ in cr])
        return jnp.logical_and(n < n_full, top > SB_STOP)

    def walk(state):
        n, cr = state
        return n + 1, step(n_full - 1 - n, cr)

    _, carry = lax.while_loop(more, walk, (jnp.int32(0), carry))
    o_ref[...] = jnp.concatenate([acc for _, acc in carry], axis=0)


def _stickbreak(q, k, vt):
    s, c = q.shape
    tq, tk = SB_TQ, ATT_TK
    j = np.arange(tk)
    later = (j[None, :] > j[:, None]).astype(np.float32)
    tri = np.concatenate([np.concatenate([later, later], axis=1), np.ones((BF16_ROWS, 2 * tk), np.float32)], axis=0)
    tri = jnp.asarray(tri, BF16)
    return pl.pallas_call(
        _sb_kernel,
        grid=(SB_HEADS // SB_GROUP, s // tq),
        in_specs=[_const_spec(tri.shape),
                  pl.BlockSpec((tq, c), lambda h, i: (i, 0)),
                  _const_spec((s, c)),
                  pl.BlockSpec((s // tk, SB_GROUP * HEAD_DIM, tk), lambda h, i: (0, h, 0))],
        out_specs=pl.BlockSpec((SB_GROUP * HEAD_DIM, tq), lambda h, i: (h, i)),
        out_shape=jax.ShapeDtypeStruct((c, s), F32),
        compiler_params=_params(("parallel", "parallel")),
        name="stickbreak",
    )(tri, q, k, vt)


def _alibi_slopes(n):
    return 2.0 ** (-8.0 * np.arange(1, n + 1) / n)


def _pad_cols(w, lo, width):
    return jnp.pad(w, ((0, 0), (lo, width - lo - w.shape[1])))


def _rot_half_cols(w):
    half = w.shape[1] // 2
    return jnp.concatenate([-w[:, half:], w[:, :half]], axis=1)


def _even_weights(w_in, w_uq, w_ukv):
    o_kr = MLA_Q_RANK + MLA_KV_RANK
    w_kr = w_in[:, o_kr:o_kr + MLA_ROPE]
    win = jnp.concatenate([w_in[:, :o_kr], _pad_cols(w_kr, MLA_NOPE, LANES),
                           _pad_cols(_rot_half_cols(w_kr), MLA_NOPE, LANES),
                           w_in[:, o_kr + MLA_ROPE:]], axis=1)
    hd = MLA_NOPE + MLA_ROPE
    q_plain, q_rot = [], []
    for hh in range(MLA_HEADS):
        wh = w_uq[:, hh * hd:(hh + 1) * hd]
        q_plain.append(_pad_cols(wh, 0, LANES))
        q_rot.append(_pad_cols(_rot_half_cols(wh[:, MLA_NOPE:]), MLA_NOPE, LANES))
    wuq = jnp.concatenate(q_plain + q_rot, axis=1)
    hk = MLA_NOPE + MLA_V
    k_cols = [_pad_cols(w_ukv[:, hh * hk:hh * hk + MLA_NOPE], 0, LANES) for hh in range(MLA_HEADS)]
    v_cols = [w_ukv[:, hh * hk + MLA_NOPE:(hh + 1) * hk] for hh in range(MLA_HEADS)]
    wkv = jnp.concatenate(k_cols + v_cols, axis=1)
    return win.astype(BF16), wuq.astype(BF16), wkv.astype(BF16)


def _rope_tables(s):
    half = MLA_ROPE // 2
    inv = ROPE_BASE ** (-jnp.arange(half, dtype=F32) / half)
    ang = jnp.arange(s).astype(F32)[:, None] * inv
    cos2 = jnp.tile(jnp.cos(ang), (1, 2))
    sin2 = jnp.tile(jnp.sin(ang), (1, 2))
    scale = (MLA_NOPE + MLA_ROPE) ** -0.5 * LOG2E
    tail = LANES - MLA_NOPE - MLA_ROPE
    ones, zeros, ztail = jnp.ones((s, MLA_NOPE), F32), jnp.zeros((s, MLA_NOPE), F32), jnp.zeros((s, tail), F32)
    cosq = jnp.concatenate([ones, cos2, ztail], axis=1) * scale
    sinq = jnp.concatenate([zeros, sin2, ztail], axis=1) * scale
    cosk = jnp.concatenate([zeros, cos2, ztail], axis=1)
    sink = jnp.concatenate([zeros, sin2, ztail], axis=1)
    return cosq, sinq, cosk, sink


def _row2d(v):
    return v.reshape(1, -1).astype(F32)


def kernel(x, ev_pre_g, ev_w_in, ev_cq_g, ev_w_uq, ev_ckv_g, ev_w_ukv, ev_lam_q1, ev_lam_k1, ev_lam_q2,
           ev_lam_k2, ev_subln_g, ev_w_out, ev_post_g, od_pre_g, od_w_in, od_w_out, od_post_g, ffn_pre_g,
           ffn_w_up, ffn_conv_w, ffn_conv_b, ffn_w_down, ffn_post_g):
    b, s, _ = x.shape
    assert b == 1 and s % max(ROW_TILE, ATT_TQ, SB_TQ, DIL_TB * DIL_CONFIGS[-1][1]) == 0
    assert ROW_TILE % ATT_TK == 0 and ATT_TQ % ATT_TK == 0 and SB_TQ % ATT_TK == 0
    assert (ATT_TQ // ATT_TK) % MAIN_UNROLL == 0 and DIL_TB % DIL_T == 0
    assert sum(FF_CHUNKS) == D_FF
    assert ATT_TK <= 256
    xs = x[0]

    def ffn_consts(i):
        return [_row2d(ffn_pre_g[i]), ffn_w_up[i].astype(BF16), ffn_conv_w[i].astype(F32),
                _row2d(ffn_conv_b[i]), ffn_w_down[i].astype(BF16), _row2d(ffn_post_g[i])]

    win, wuq, wkv = _even_weights(ev_w_in[0], ev_w_uq[0], ev_w_ukv[0])
    cosq, sinq, cosk, sink = _rope_tables(s)
    pos_col = (jnp.arange(ROW_TILE) % ATT_TK).astype(BF16)[:, None]
    pos_cols = jnp.pad(jnp.tile(pos_col, (1, SLOPE_PARTS)), ((0, 0), (0, DIFF_QK - SLOPE_PARTS)))
    pos_tile = jnp.tile(pos_cols, (1, LANES // DIFF_QK))
    qm, km, vmt, dq, dk, dvt, kn_tiles = _proj_even(xs, _row2d(ev_pre_g[0]), win, _row2d(ev_cq_g[0]), wuq,
                                          _row2d(ev_ckv_g[0]), wkv, cosq, sinq, cosk, sink, pos_tile)
    o_mla = _mla_attention(qm, km, vmt)
    lam_p = jnp.pad(jnp.stack([ev_lam_q1[0], ev_lam_k1[0], ev_lam_q2[0], ev_lam_k2[0]]).astype(F32),
                    ((0, SUBLANES - 4), (0, LANES - DIFF_QK)))
    d_slopes = _alibi_slopes(DIFF_HEADS) * LOG2E
    slope_arr = jnp.asarray(np.broadcast_to(d_slopes[:, None, None], (DIFF_HEADS, SUBLANES, LANES)), F32)
    qfeat = jnp.zeros((DIFF_HEADS, ATT_TQ, LANES), F32)
    rest = jnp.asarray(d_slopes, F32)
    for part in range(SLOPE_PARTS):
        piece = rest.astype(BF16).astype(F32)
        qfeat = qfeat.at[:, :, part].set(piece[:, None]).at[:, :, DIFF_QK + part].set(piece[:, None])
        rest = rest - piece
    lam_init = 0.8 - 0.6 * math.exp(-0.3 * 0)
    knorm = jnp.pad(jnp.max(kn_tiles, axis=0).reshape(DIFF_HEADS, 2, LANES), ((0, 0), (0, SUBLANES - 2), (0, 0)))
    o_diff = _diff_attention(lam_p, slope_arr, knorm, qfeat.astype(BF16), _row2d(ev_subln_g[0]),
                             dq, dk, dvt, lam_init)
    x2 = _out_ffn(_out_ffn_even_kernel, "out_ffn_even", xs, [o_mla, o_diff], [],
                  [ev_w_out[0].astype(BF16), _row2d(ev_post_g[0])] + ffn_consts(0))

    p = _proj_odd(x2, _row2d(od_pre_g[0]), od_w_in[0].astype(BF16))
    slopes = _alibi_slopes(len(DIL_CONFIGS) * DIL_HEADS_PER_GROUP).reshape(len(DIL_CONFIGS), -1)
    dil_outs = []
    for gi, (_, dil) in enumerate(DIL_CONFIGS):
        dil_outs += list(_dilated_group(p[gi], p[3 + gi], p[6 + gi], dil, slopes[gi]))
    o_sbt = _stickbreak(p[9], p[10], p[11])
    lane_tiles = DIL_HEADS_PER_GROUP * HEAD_DIM // LANES
    x4 = _out_ffn(_out_ffn_odd_kernel, "out_ffn_odd", x2, dil_outs, [o_sbt],
                  [od_w_out[0].astype(BF16), _row2d(od_post_g[0])] + ffn_consts(1),
                  scratch=[pltpu.VMEM((lane_tiles, ROW_TILE, LANES), F32)])
    return x4[None]
```
